```python
import math
import jax
import jax.numpy as jnp
from jax import lax
import numpy as np

D_MODEL = 1024
BATCH = 2
SEQ = 8192
DEPTH = 4

GRID_W = 64
CTX_LEN = 256
D_MIX = D_MODEL
W_GROUP = D_MIX // 4
EPS = 1e-6
CONV_W = 3
ROPE_BASE = 10000.0

ML_HEADS = 4
ML_DH = W_GROUP // ML_HEADS
ML_CHUNK = 64

MLA_HEADS = 4
MLA_NOPE = 64
MLA_ROPE = 32
MLA_V = W_GROUP // MLA_HEADS
MLA_Q_LORA = 256
MLA_KV_LORA = 128
ATTN_BLOCK = 128

SSD_HEADDIM = 64
SSD_HEADS = W_GROUP // SSD_HEADDIM
SSD_GROUPS = 2
SSD_STATE = 128
SSD_CHUNK = 64

S5_GROUP = 16
S5_NGROUPS = W_GROUP // S5_GROUP
S5_STATE = 64

D_FF = 2816

ML_COLS = 4 * W_GROUP + 4 * ML_HEADS
MLA_COLS = MLA_Q_LORA + MLA_KV_LORA + MLA_ROPE
SSD_COLS = 2 * W_GROUP + 2 * SSD_GROUPS * SSD_STATE + 2 * SSD_HEADS
S5_COLS = W_GROUP
P_IN = ML_COLS + MLA_COLS + SSD_COLS + S5_COLS
SPLIT_IN = (ML_COLS, ML_COLS + MLA_COLS, ML_COLS + MLA_COLS + SSD_COLS)

kernel_name = 'hybrid_parallel_groups_dit_trunk'


def rmsnorm(x, g):
    x32 = x.astype(jnp.float32)
    y = x32 * lax.rsqrt(jnp.mean(x32 * x32, axis=-1, keepdims=True) + EPS)
    return (y * g.astype(jnp.float32)).astype(x.dtype)


def modulate(h, shift, scale):
    return h * (1 + scale) + shift


def dwconv(x, w, b=None):
    pad = (CONV_W - 1) // 2
    y = lax.conv_general_dilated(x, w[:, None, :].astype(x.dtype), window_strides=(1,),
                                 padding=((pad, pad),), dimension_numbers=('NWC', 'WIO', 'NWC'),
                                 feature_group_count=x.shape[-1])
    return y if b is None else y + b


def axial_rope_tables(n_tokens, dim):
    rows = n_tokens // GRID_W
    row = jnp.broadcast_to(jnp.arange(rows)[:, None], (rows, GRID_W)).reshape(-1).astype(jnp.float32)
    col = jnp.broadcast_to(jnp.arange(GRID_W)[None, :], (rows, GRID_W)).reshape(-1).astype(jnp.float32)
    n_freq = dim // 4
    inv = ROPE_BASE ** (-jnp.arange(n_freq, dtype=jnp.float32) / n_freq)
    ang = jnp.concatenate([row[:, None] * inv, col[:, None] * inv], axis=-1)
    return jnp.cos(ang), jnp.sin(ang)


def apply_rope(x, cos, sin):
    x1, x2 = jnp.split(x.astype(jnp.float32), 2, axis=-1)
    cs, sn = cos[:, None, :], sin[:, None, :]
    return jnp.concatenate([x1 * cs - x2 * sn, x1 * sn + x2 * cs], axis=-1).astype(x.dtype)


def mlstm_chunked(q, k, v, ig, lf, state):
    bsz, nh, T, dh = q.shape
    nc = T // ML_CHUNK

    def to_chunks(a):
        return jnp.moveaxis(a.reshape(a.shape[:2] + (nc, ML_CHUNK) + a.shape[3:]), 2, 0)

    mask = jnp.tril(jnp.ones((ML_CHUNK, ML_CHUNK), dtype=bool))

    def step(carry, blk):
        C, n, m = carry
        qb, kb, vb, ib, fb = blk
        b = jnp.cumsum(fb, axis=-1)
        dmat = jnp.where(mask, b[..., :, None] - b[..., None, :] + ib[..., None, :], -jnp.inf)
        inter = b + m[..., None]
        m_t = jnp.maximum(inter, jnp.max(dmat, axis=-1))
        w_inter = jnp.exp(inter - m_t)
        s = jnp.einsum('bhtd,bhsd->bhts', qb, kb) * jnp.exp(dmat - m_t[..., None])
        num = jnp.einsum('bhts,bhse->bhte', s, vb) + w_inter[..., None] * jnp.einsum('bhtd,bhde->bhte', qb, C)
        den = jnp.sum(s, axis=-1) + w_inter * jnp.einsum('bhtd,bhd->bht', qb, n)
        h = num / jnp.maximum(jnp.abs(den), jnp.exp(-m_t))[..., None]
        b_last = b[..., -1]
        w_log = b_last[..., None] - b + ib
        m_new = jnp.maximum(b_last + m, jnp.max(w_log, axis=-1))
        decay = jnp.exp(b_last + m - m_new)
        w_s = jnp.exp(w_log - m_new[..., None])
        C = decay[..., None, None] * C + jnp.einsum('bhs,bhsd,bhse->bhde', w_s, kb, vb)
        n = decay[..., None] * n + jnp.einsum('bhs,bhsd->bhd', w_s, kb)
        return (C, n, m_new), h

    state, h = lax.scan(step, state, tuple(to_chunks(a) for a in (q, k, v, ig, lf)))
    return jnp.moveaxis(h, 0, 2).reshape(bsz, nh, T, dh), state


def mlstm_mixer(pc, pl, gate_bias, norm_g):
    def prep(p):
        q, k, v, o, g = jnp.split(p, [W_GROUP, 2 * W_GROUP, 3 * W_GROUP, 4 * W_GROUP], axis=-1)
        bsz, T = p.shape[:2]

        def heads(a):
            return jnp.moveaxis(a.astype(jnp.float32).reshape(bsz, T, ML_HEADS, ML_DH), 2, 1)

        g = jnp.moveaxis(g.astype(jnp.float32).reshape(bsz, T, 4, ML_HEADS) + gate_bias, 1, 3)
        return heads(q), heads(k) * ML_DH ** -0.5, heads(v), o, g

    qc, kc, vc, oc, gc = prep(pc)
    ql, kl, vl, ol, gl = prep(pl)
    bsz = pl.shape[0]
    zero = (jnp.zeros((bsz, ML_HEADS, ML_DH, ML_DH), jnp.float32),
            jnp.zeros((bsz, ML_HEADS, ML_DH), jnp.float32),
            jnp.zeros((bsz, ML_HEADS), jnp.float32))
    flip = lambda a: jnp.flip(a, axis=2)
    hc_sum, hl_sum = 0.0, 0.0
    for d in range(2):
        ctx_args = (qc, kc, vc, gc[:, 2 * d], jax.nn.log_sigmoid(gc[:, 2 * d + 1]))
        lat_args = (ql, kl, vl, gl[:, 2 * d], jax.nn.log_sigmoid(gl[:, 2 * d + 1]))
        if d == 1:
            ctx_args = tuple(flip(a) for a in ctx_args)
            lat_args = tuple(flip(a) for a in lat_args)
        hc, st = mlstm_chunked(*ctx_args, zero)
        hl, _ = mlstm_chunked(*lat_args, st)
        if d == 1:
            hc, hl = flip(hc), flip(hl)
        hc_sum = hc_sum + hc
        hl_sum = hl_sum + hl

    def out(h, o):
        bsz_, _, T, _ = h.shape
        h = rmsnorm(jnp.moveaxis(h, 1, 2), norm_g.reshape(ML_HEADS, ML_DH)).reshape(bsz_, T, W_GROUP)
        return (h * jax.nn.sigmoid(o.astype(jnp.float32))).astype(o.dtype)

    return out(hc_sum, oc), out(hl_sum, ol)


def mla_mixer(pc, pl, q_norm, kv_norm, w_uq, w_ukv, cos, sin):
    scale = (MLA_NOPE + MLA_ROPE) ** -0.5

    def project(p, rotate):
        cq, ckv, kr = jnp.split(p, [MLA_Q_LORA, MLA_Q_LORA + MLA_KV_LORA], axis=-1)
        bsz, T = p.shape[:2]
        q = (rmsnorm(cq, q_norm) @ w_uq).reshape(bsz, T, MLA_HEADS, MLA_NOPE + MLA_ROPE)
        kv = (rmsnorm(ckv, kv_norm) @ w_ukv).reshape(bsz, T, MLA_HEADS, MLA_NOPE + MLA_V)
        q_nope, q_rope = jnp.split(q, [MLA_NOPE], axis=-1)
        k_nope, v = jnp.split(kv, [MLA_NOPE], axis=-1)
        kr = kr[:, :, None, :]
        if rotate:
            q_rope = apply_rope(q_rope, cos, sin)
            kr = apply_rope(kr, cos, sin)
        q = jnp.concatenate([q_nope, q_rope], axis=-1)
        k = jnp.concatenate([k_nope, jnp.broadcast_to(kr, k_nope.shape[:-1] + (MLA_ROPE,))], axis=-1)
        return q, k, v

    def attend(q, k, v):
        s = jnp.einsum('bqhd,bkhd->bhqk', q, k).astype(jnp.float32) * scale
        p = jax.nn.softmax(s, axis=-1).astype(v.dtype)
        return jnp.einsum('bhqk,bkhd->bqhd', p, v)

    qc, kc, vc = project(pc, False)
    ql, kl, vl = project(pl, True)
    bsz, Tc = pc.shape[:2]
    T = pl.shape[1]
    yc = attend(qc, kc, vc).reshape(bsz, Tc, W_GROUP)
    k_all = jnp.concatenate([kl, kc], axis=1)
    v_all = jnp.concatenate([vl, vc], axis=1)
    nb = T // ATTN_BLOCK
    qb = jnp.moveaxis(ql.reshape(bsz, nb, ATTN_BLOCK, MLA_HEADS, MLA_NOPE + MLA_ROPE), 1, 0)
    yl = lax.map(lambda qq: attend(qq, k_all, v_all), qb)
    yl = jnp.moveaxis(yl, 0, 1).reshape(bsz, T, W_GROUP)
    return yc, yl


def ssd_chunked(q, k, v, la, S):
    bsz, nh, T, _ = q.shape
    nc = T // SSD_CHUNK

    def to_chunks(a):
        return jnp.moveaxis(a.reshape(a.shape[:2] + (nc, SSD_CHUNK) + a.shape[3:]), 2, 0)

    mask = jnp.tril(jnp.ones((SSD_CHUNK, SSD_CHUNK), dtype=bool))

    def step(S, blk):
        qb, kb, vb, lb = blk
        cs = jnp.cumsum(lb, axis=-1)
        decay = jnp.exp(jnp.where(mask, cs[..., :, None] - cs[..., None, :], -jnp.inf))
        y = jnp.einsum('bhts,bhsp->bhtp', jnp.einsum('bhtn,bhsn->bhts', qb, kb) * decay, vb) \
            + jnp.exp(cs)[..., None] * jnp.einsum('bhtn,bhnp->bhtp', qb, S)
        w_s = jnp.exp(cs[..., -1:] - cs)
        S = jnp.exp(cs[..., -1])[..., None, None] * S + jnp.einsum('bhs,bhsn,bhsp->bhnp', w_s, kb, vb)
        return S, y

    S, y = lax.scan(step, S, tuple(to_chunks(a) for a in (q, k, v, la)))
    return jnp.moveaxis(y, 0, 2).reshape(bsz, nh, T, v.shape[-1]), S


def ssd_mixer(pc, pl, conv_w, conv_b, a_log, dt_bias, d_skip, norm_g):
    gn = SSD_GROUPS * SSD_STATE
    rep = SSD_HEADS // SSD_GROUPS

    def prep(p):
        z, xbc, dt = jnp.split(p, [W_GROUP, 2 * W_GROUP + 2 * gn], axis=-1)
        xbc = jax.nn.silu(dwconv(xbc, conv_w, conv_b)).astype(jnp.float32)
        xs, bm, cm = jnp.split(xbc, [W_GROUP, W_GROUP + gn], axis=-1)
        bsz, T = p.shape[:2]
        xs = jnp.moveaxis(xs.reshape(bsz, T, SSD_HEADS, SSD_HEADDIM), 2, 1)
        bm = jnp.repeat(jnp.moveaxis(bm.reshape(bsz, T, SSD_GROUPS, SSD_STATE), 2, 1), rep, axis=1)
        cm = jnp.repeat(jnp.moveaxis(cm.reshape(bsz, T, SSD_GROUPS, SSD_STATE), 2, 1), rep, axis=1)
        dt = jax.nn.softplus(dt.astype(jnp.float32).reshape(bsz, T, 2, SSD_HEADS) + dt_bias)
        return z, xs, bm, cm, jnp.moveaxis(dt, 1, 3)

    zc, xc, bc, cc, dtc = prep(pc)
    zl, xl, bl, cl, dtl = prep(pl)
    bsz = pl.shape[0]
    zero = jnp.zeros((bsz, SSD_HEADS, SSD_STATE, SSD_HEADDIM), jnp.float32)
    flip = lambda a: jnp.flip(a, axis=2)
    yc = d_skip[:, None, None] * xc
    yl = d_skip[:, None, None] * xl
    for d in range(2):
        A = -jnp.exp(a_log[d])[:, None]
        ctx_args = (cc, bc * dtc[:, d, ..., None], xc, dtc[:, d] * A)
        lat_args = (cl, bl * dtl[:, d, ..., None], xl, dtl[:, d] * A)
        if d == 1:
            ctx_args = tuple(flip(a) for a in ctx_args)
            lat_args = tuple(flip(a) for a in lat_args)
        hc, S = ssd_chunked(*ctx_args, zero)
        hl, _ = ssd_chunked(*lat_args, S)
        if d == 1:
            hc, hl = flip(hc), flip(hl)
        yc = yc + hc
        yl = yl + hl

    def out(y, z):
        bsz_, _, T, _ = y.shape
        y = jnp.moveaxis(y, 1, 2).reshape(bsz_, T, W_GROUP)
        return rmsnorm(y * jax.nn.silu(z.astype(jnp.float32)), norm_g).astype(z.dtype)

    return out(yc, zc), out(yl, zl)


def diag_scan(ab_re, ab_im, bu_re, bu_im, x0_re, x0_im):
    bu_re = bu_re.at[:, 0].add(ab_re * x0_re - ab_im * x0_im)
    bu_im = bu_im.at[:, 0].add(ab_re * x0_im + ab_im * x0_re)
    a_re = jnp.broadcast_to(ab_re, bu_re.shape)
    a_im = jnp.broadcast_to(ab_im, bu_im.shape)

    def op(e1, e2):
        a1r, a1i, b1r, b1i = e1
        a2r, a2i, b2r, b2i = e2
        return (a2r * a1r - a2i * a1i, a2r * a1i + a2i * a1r,
                a2r * b1r - a2i * b1i + b2r, a2r * b1i + a2i * b1r + b2i)

    _, _, xr, xi = lax.associative_scan(op, (a_re, a_im, bu_re, bu_im), axis=1)
    return xr, xi


def s5_mixer(uc, ul, a_re, a_im, log_dt, b_re, b_im, c_re, c_im, d_skip, w_glu):
    def groups(u):
        return u.astype(jnp.float32).reshape(u.shape[0], u.shape[1], S5_NGROUPS, S5_GROUP)

    def readout(s_re, s_im):
        return jnp.einsum('gjn,btgn->btgj', c_re, s_re) - jnp.einsum('gjn,btgn->btgj', c_im, s_im)

    ugc, ugl = groups(uc), groups(ul)
    zero = jnp.zeros((uc.shape[0], S5_NGROUPS, S5_STATE), jnp.float32)
    dg = d_skip.reshape(S5_NGROUPS, S5_GROUP)
    yc = dg * ugc
    yl = dg * ugl
    for d in range(2):
        lam_re = jnp.minimum(a_re[d], -1e-4)
        lam_im = a_im[d]
        dt = jnp.exp(log_dt[d])[:, None]
        mag = jnp.exp(lam_re * dt)
        ab_re, ab_im = mag * jnp.cos(lam_im * dt), mag * jnp.sin(lam_im * dt)
        den = lam_re * lam_re + lam_im * lam_im
        f_re = ((ab_re - 1) * lam_re + ab_im * lam_im) / den
        f_im = (ab_im * lam_re - (ab_re - 1) * lam_im) / den
        bb_re = f_re[..., None] * b_re - f_im[..., None] * b_im
        bb_im = f_re[..., None] * b_im + f_im[..., None] * b_re

        def drive(ug):
            return (jnp.einsum('gnj,btgj->btgn', bb_re, ug), jnp.einsum('gnj,btgj->btgn', bb_im, ug))

        uc_d, ul_d = (ugc, ugl) if d == 0 else (jnp.flip(ugc, 1), jnp.flip(ugl, 1))
        sc_re, sc_im = diag_scan(ab_re, ab_im, *drive(uc_d), zero, zero)
        sl_re, sl_im = diag_scan(ab_re, ab_im, *drive(ul_d), sc_re[:, -1], sc_im[:, -1])
        rc, rl = readout(sc_re, sc_im), readout(sl_re, sl_im)
        if d == 1:
            rc, rl = jnp.flip(rc, 1), jnp.flip(rl, 1)
        yc = yc + rc
        yl = yl + rl

    def glu(y, like):
        y = jax.nn.gelu(y.reshape(y.shape[0], y.shape[1], W_GROUP))
        return (y * jax.nn.sigmoid(y @ w_glu.astype(jnp.float32))).astype(like.dtype)

    return glu(yc, uc), glu(yl, ul)


def conv_ffn(h, w_up, conv_w, w_down):
    u, g = jnp.split(h @ w_up, 2, axis=-1)
    return (jax.nn.silu(dwconv(g, conv_w)) * u) @ w_down


def setup_inputs(seed: int = 0) -> dict:
    key = jax.random.key(seed)
    ks = iter(jax.random.split(key, 48))
    f32 = jnp.float32
    L, D = DEPTH, D_MODEL

    def nrm(shape, scale):
        return scale * jax.random.normal(next(ks), shape, f32)

    def gain(shape):
        return 1.0 + 0.02 * jax.random.normal(next(ks), shape, f32)

    def unif(shape, lo, hi):
        return jax.random.uniform(next(ks), shape, f32, lo, hi)

    ssd_conv_ch = W_GROUP + 2 * SSD_GROUPS * SSD_STATE
    fgate = jnp.linspace(3.0, 6.0, ML_HEADS, dtype=f32)
    igate = jnp.zeros((ML_HEADS,), f32)
    ml_gate_bias = jnp.stack([igate, fgate, igate, fgate])[None] + nrm((L, 4, ML_HEADS), 0.1)
    dt0 = jnp.exp(unif((L, 2, SSD_HEADS), math.log(1e-3), math.log(1e-1)))
    return {
        'x': nrm((BATCH, SEQ, D), 1.0),
        'c': nrm((BATCH, D), 1.0),
        'ctx': nrm((BATCH, CTX_LEN, D), 1.0),
        'c_ctx': nrm((D,), 1.0),
        'w_mod': nrm((L, D, 6 * D), 0.5 * D ** -0.5),
        'b_mod': nrm((L, 6 * D), 0.01),
        'norm1': gain((L, D)),
        'norm2': gain((L, D)),
        'w_in': nrm((L, D, P_IN), D ** -0.5),
        'ml_gate_bias': ml_gate_bias,
        'ml_norm': gain((L, W_GROUP)),
        'mla_q_norm': gain((L, MLA_Q_LORA)),
        'mla_kv_norm': gain((L, MLA_KV_LORA)),
        'mla_w_uq': nrm((L, MLA_Q_LORA, MLA_HEADS * (MLA_NOPE + MLA_ROPE)), MLA_Q_LORA ** -0.5),
        'mla_w_ukv': nrm((L, MLA_KV_LORA, MLA_HEADS * (MLA_NOPE + MLA_V)), MLA_KV_LORA ** -0.5),
        'ssd_conv_w': nrm((L, CONV_W, ssd_conv_ch), CONV_W ** -0.5),
        'ssd_conv_b': nrm((L, ssd_conv_ch), 0.01),
        'ssd_a_log': jnp.log(unif((L, 2, SSD_HEADS), 1.0, 16.0)),
        'ssd_dt_bias': dt0 + jnp.log(-jnp.expm1(-dt0)),
        'ssd_d': gain((L, SSD_HEADS)),
        'ssd_norm': gain((L, W_GROUP)),
        's5_a_re': -0.5 + nrm((L, 2, S5_NGROUPS, S5_STATE), 0.01),
        's5_a_im': jnp.broadcast_to(math.pi * jnp.arange(S5_STATE, dtype=f32), (L, 2, S5_NGROUPS, S5_STATE)),
        's5_log_dt': unif((L, 2, S5_NGROUPS), math.log(1e-3), math.log(1e-1)),
        's5_b_re': nrm((L, S5_NGROUPS, S5_STATE, S5_GROUP), (2 * S5_GROUP) ** -0.5),
        's5_b_im': nrm((L, S5_NGROUPS, S5_STATE, S5_GROUP), (2 * S5_GROUP) ** -0.5),
        's5_c_re': nrm((L, S5_NGROUPS, S5_GROUP, S5_STATE), S5_STATE ** -0.5),
        's5_c_im': nrm((L, S5_NGROUPS, S5_GROUP, S5_STATE), S5_STATE ** -0.5),
        's5_d': gain((L, W_GROUP)),
        's5_w_glu': nrm((L, W_GROUP, W_GROUP), W_GROUP ** -0.5),
        'w_out': nrm((L, D_MIX, D), D_MIX ** -0.5),
        'ffn_w_up': nrm((L, D, 2 * D_FF), D ** -0.5),
        'ffn_conv_w': nrm((L, CONV_W, D_FF), CONV_W ** -0.5),
        'ffn_w_down': nrm((L, D_FF, D), D_FF ** -0.5),
        'final_norm': gain((D,)),
    }


def reference(x, c, ctx, c_ctx, w_mod, b_mod, norm1, norm2, w_in, ml_gate_bias, ml_norm,
              mla_q_norm, mla_kv_norm, mla_w_uq, mla_w_ukv, ssd_conv_w, ssd_conv_b, ssd_a_log,
              ssd_dt_bias, ssd_d, ssd_norm, s5_a_re, s5_a_im, s5_log_dt, s5_b_re, s5_b_im,
              s5_c_re, s5_c_im, s5_d, s5_w_glu, w_out, ffn_w_up, ffn_conv_w, ffn_w_down, final_norm):
    cos, sin = axial_rope_tables(x.shape[1], MLA_ROPE)
    s_lat = jax.nn.silu(c)
    s_ctx = jax.nn.silu(c_ctx)
    for l in range(DEPTH):
        ml = jnp.split(s_lat @ w_mod[l] + b_mod[l], 6, axis=-1)
        mc = jnp.split(s_ctx @ w_mod[l] + b_mod[l], 6, axis=-1)
        hl = modulate(rmsnorm(x, norm1[l]), ml[0][:, None], ml[1][:, None])
        hc = modulate(rmsnorm(ctx, norm1[l]), mc[0], mc[1])
        pl = jnp.split(hl @ w_in[l], SPLIT_IN, axis=-1)
        pc = jnp.split(hc @ w_in[l], SPLIT_IN, axis=-1)
        ya_c, ya_l = mlstm_mixer(pc[0], pl[0], ml_gate_bias[l], ml_norm[l])
        yb_c, yb_l = mla_mixer(pc[1], pl[1], mla_q_norm[l], mla_kv_norm[l], mla_w_uq[l], mla_w_ukv[l], cos, sin)
        yc_c, yc_l = ssd_mixer(pc[2], pl[2], ssd_conv_w[l], ssd_conv_b[l], ssd_a_log[l], ssd_dt_bias[l],
                               ssd_d[l], ssd_norm[l])
        yd_c, yd_l = s5_mixer(pc[3], pl[3], s5_a_re[l], s5_a_im[l], s5_log_dt[l], s5_b_re[l], s5_b_im[l],
                              s5_c_re[l], s5_c_im[l], s5_d[l], s5_w_glu[l])
        y_lat = jnp.concatenate([ya_l, yb_l, yc_l, yd_l], axis=-1) @ w_out[l]
        x = x + ml[2][:, None] * y_lat
        h2 = modulate(rmsnorm(x, norm2[l]), ml[3][:, None], ml[4][:, None])
        x = x + ml[5][:, None] * conv_ffn(h2, ffn_w_up[l], ffn_conv_w[l], ffn_w_down[l])
        if l < DEPTH - 1:
            y_ctx = jnp.concatenate([ya_c, yb_c, yc_c, yd_c], axis=-1) @ w_out[l]
            ctx = ctx + mc[2] * y_ctx
            hc2 = modulate(rmsnorm(ctx, norm2[l]), mc[3], mc[4])
            ctx = ctx + mc[5] * conv_ffn(hc2, ffn_w_up[l], ffn_conv_w[l], ffn_w_down[l])
    return rmsnorm(x, final_norm)
```

```python
import functools
import math

import numpy as np
import jax
import jax.numpy as jnp
from jax import lax
from jax.experimental import pallas as pl
from jax.experimental.pallas import tpu as pltpu

F32 = jnp.float32
BF16 = jnp.bfloat16
HI = lax.Precision.HIGHEST

D_MODEL = 1024
W_GROUP = 256
EPS = 1e-6
GRID_W = 64
ROPE_BASE = 10000.0

ML_HEADS = 4
ML_DH = 64

MLA_HEADS = 4
MLA_NOPE = 64
MLA_ROPE = 32
MLA_V = 64
MLA_Q_LORA = 256
MLA_KV_LORA = 128

SSD_HEADS = 4
SSD_HEADDIM = 64
SSD_GROUPS = 2
SSD_STATE = 128

S5_GROUP = 16
S5_NGROUPS = 16
S5_STATE = 64
S5_CHUNK = 64

D_FF = 2816
FF_CHUNK = 256

ROW_TILE = 512
SCAN_CHUNK = 256
ATTN_TQ = 512
ATTN_TK = 512
HALO = 16
LANES = 128
VMEM_LIMIT = 56 * 1024 * 1024

MISC_GATE = 0
MISC_DT = 16
MISC_KR = 24


def _cparams(sem):
    return pltpu.CompilerParams(dimension_semantics=sem, vmem_limit_bytes=VMEM_LIMIT)


def _dot(a, b):
    return jnp.dot(a, b, preferred_element_type=F32)


def _dot_nt(a, b):
    return lax.dot_general(a, b, (((1,), (1,)), ((), ())), preferred_element_type=F32)


def _dot_tn(a, b):
    return lax.dot_general(a, b, (((0,), (0,)), ((), ())), preferred_element_type=F32)


def _dot_hi(a, b):
    return jnp.dot(a, b, preferred_element_type=F32, precision=HI)


def _sigmoid(x):
    return 1.0 / (1.0 + jnp.exp(-x))


def _silu(x):
    return x * _sigmoid(x)


def _log_sigmoid(x):
    return jnp.minimum(x, 0.0) - jnp.log(1.0 + jnp.exp(-jnp.abs(x)))


def _softplus(x):
    return jnp.maximum(x, 0.0) + jnp.log(1.0 + jnp.exp(-jnp.abs(x)))


def _gelu_tanh(x):
    return 0.5 * x * (1.0 + jnp.tanh(math.sqrt(2.0 / math.pi) * (x + 0.044715 * x * x * x)))


def _rms(x, g):
    return x * lax.rsqrt(jnp.mean(x * x, axis=-1, keepdims=True) + EPS) * g


def _segment_masks(row0, n_rows, bounds):
    r = row0 + lax.broadcasted_iota(jnp.int32, (n_rows, 1), 0)
    first = r == bounds[0]
    last = r == bounds[1] - 1
    for s in bounds[1:-1]:
        first = first | (r == s)
    for e in bounds[2:]:
        last = last | (r == e - 1)
    return first, last


def _mod_kernel(c_ref, w_ref, b_ref, o_ref):
    c = c_ref[...]
    s = _silu(c).astype(BF16)
    o_ref[...] = _dot(s, w_ref[...].astype(BF16)) + b_ref[...]


def _modulation(c8, w_mod, b_mod):
    depth, d, d6 = w_mod.shape
    return pl.pallas_call(
        _mod_kernel,
        out_shape=jax.ShapeDtypeStruct((depth, 8, d6), F32),
        grid=(depth, d6 // d),
        in_specs=[
            pl.BlockSpec((8, d), lambda l, j: (0, 0)),
            pl.BlockSpec((None, d, d), lambda l, j: (l, 0, j)),
            pl.BlockSpec((None, 1, d), lambda l, j: (l, 0, j)),
        ],
        out_specs=pl.BlockSpec((None, 8, d), lambda l, j: (l, 0, j)),
        compiler_params=_cparams(("arbitrary", "arbitrary")),
        name="modulation",
    )(c8, w_mod, b_mod.reshape(depth, 1, d6))


_IN_SPLITS = (("q", 256), ("k", 256), ("v1", 512), ("og", 256), ("misc", 128),
              ("mlac", 384), ("z", 256), ("xbc", 768), ("u", 256))
_IN_TOTAL = sum(w for _, w in _IN_SPLITS)


def _inproj_perm():
    ml, mla, ssd = 0, 1040, 1456
    s5 = 2488
    cols = []
    cols += list(range(ml, ml + 256))
    cols += list(range(ml + 256, ml + 512))
    for h in range(ML_HEADS):
        cols += list(range(ml + 512 + h * 64, ml + 512 + (h + 1) * 64)) + [-1] * 64
    cols += list(range(ml + 768, ml + 1024))
    misc = list(range(ml + 1024, ml + 1040))
    misc += list(range(ssd + 1024, ssd + 1032))
    misc += list(range(mla + 384, mla + 416))
    cols += misc + [-1] * (LANES - len(misc))
    cols += list(range(mla, mla + 384))
    cols += list(range(ssd, ssd + 256))
    cols += list(range(ssd + 256, ssd + 1024))
    cols += list(range(s5, s5 + 256))
    cols = np.asarray(cols, np.int32)
    scale = np.ones((len(cols),), np.float32)
    scale[256:512] = ML_DH ** -0.5
    scale[cols < 0] = 0.0
    return np.maximum(cols, 0), scale


def _inproj_kernel(x_ref, mod_ref, g_ref, w_ref, q_ref, k_ref, v1_ref, og_ref, misc_ref,
                   mlac_ref, z_ref, xbc_ref, u_ref):
    x = x_ref[...]
    h = _rms(x, g_ref[...]) * (1.0 + mod_ref[1:2, :]) + mod_ref[0:1, :]
    hb = h.astype(BF16)
    outs = (q_ref, k_ref, v1_ref, og_ref, misc_ref, mlac_ref, z_ref, xbc_ref, u_ref)
    off = 0
    for (name, width), o_ref in zip(_IN_SPLITS, outs):
        y = _dot(hb, w_ref[:, off:off + width])
        if name == "v1":
            lane = lax.broadcasted_iota(jnp.int32, (1, width), 1)
            y = y + jnp.where((lane & (LANES - 1)) == MLA_V, 1.0, 0.0)
        o_ref[...] = y.astype(o_ref.dtype)
        off += width


def _in_projection(x, mod, g, w, seg_of_tile):
    r, d = x.shape
    dtypes = dict(q=BF16, k=BF16, v1=BF16, og=F32, misc=F32, mlac=F32, z=F32, xbc=F32, u=F32)
    row = lambda i: (i, 0)
    return pl.pallas_call(
        _inproj_kernel,
        out_shape=[jax.ShapeDtypeStruct((r, width), dtypes[name]) for name, width in _IN_SPLITS],
        grid=(r // ROW_TILE,),
        in_specs=[
            pl.BlockSpec((ROW_TILE, d), row),
            pl.BlockSpec((None, 8, d), lambda i: (seg_of_tile(i), 0, 0)),
            pl.BlockSpec((1, d), lambda i: (0, 0)),
            pl.BlockSpec((d, _IN_TOTAL), lambda i: (0, 0)),
        ],
        out_specs=[pl.BlockSpec((ROW_TILE, width), row) for _, width in _IN_SPLITS],
        compiler_params=_cparams(("arbitrary",)),
        name="in_projection",
    )(x, mod, g, w)


def _tri_masks(n):
    row = lax.broadcasted_iota(jnp.int32, (n, n), 0)
    col = lax.broadcasted_iota(jnp.int32, (n, n), 1)
    return col <= row, col >= row


def _mlstm_kernel(qf_ref, kf_ref, vf_ref, mf_ref, qb_ref, kb_ref, vb_ref, mb_ref,
                  gbr_ref, gbc_ref, hf_ref, hb_ref, st_ref, m_ref):
    n = qf_ref.shape[0]

    @pl.when(pl.program_id(1) == 0)
    def _():
        st_ref[...] = jnp.zeros_like(st_ref)
        m_ref[...] = jnp.zeros_like(m_ref)

    tril, triu = _tri_masks(n)
    trilf = tril.astype(F32)
    triuf = triu.astype(F32)
    dirs = ((qf_ref, kf_ref, vf_ref, mf_ref, hf_ref, tril, trilf, triuf, n - 1),
            (qb_ref, kb_ref, vb_ref, mb_ref, hb_ref, triu, triuf, trilf, 0))
    for d, (q_ref, k_ref, v_ref, misc_ref, out_ref, mask, mask_f, mask_tf, last) in enumerate(dirs):
        g = misc_ref[...] + gbr_ref[...]
        gt = misc_ref[...].T + gbc_ref[...]
        bcol = _dot_hi(mask_f, _log_sigmoid(g))
        brow = _dot_hi(_log_sigmoid(gt[0:16, :]), mask_tf)
        for h in range(ML_HEADS):
            hd = d * ML_HEADS + h
            i_idx = MISC_GATE + 8 * d + h
            f_idx = i_idx + ML_HEADS
            ig_col = g[:, i_idx:i_idx + 1]
            ig_row = gt[i_idx:i_idx + 1, :]
            b_col = bcol[:, f_idx:f_idx + 1]
            b_row = brow[f_idx:f_idx + 1, :]
            b_last = b_col[last:last + 1, :]
            m_prev = m_ref[hd:hd + 1, 0:1]
            q = q_ref[:, h * ML_DH:(h + 1) * ML_DH]
            k = k_ref[:, h * ML_DH:(h + 1) * ML_DH]
            v1 = v_ref[:, h * LANES:(h + 1) * LANES]
            state = st_ref[hd]

            dmat = jnp.where(mask, b_col - b_row + ig_row, -jnp.inf)
            inter = b_col + m_prev
            m_t = jnp.maximum(inter, jnp.max(dmat, axis=-1, keepdims=True))
            w_inter = jnp.exp(inter - m_t)
            s = _dot_nt(q, k) * jnp.exp(dmat - m_t)
            tot = _dot(s.astype(BF16), v1) + w_inter * _dot(q, state.astype(BF16))
            den = tot[:, ML_DH:ML_DH + 1]
            hval = tot[:, 0:ML_DH] / jnp.maximum(jnp.abs(den), jnp.exp(-m_t))
            out_ref[:, h * ML_DH:(h + 1) * ML_DH] = hval

            w_log = b_last - b_col + ig_col
            m_new = jnp.maximum(b_last + m_prev, jnp.max(w_log, axis=0, keepdims=True))
            decay = jnp.exp(b_last + m_prev - m_new)
            kw = (k.astype(F32) * jnp.exp(w_log - m_new)).astype(BF16)
            st_ref[hd] = decay * state + _dot_tn(kw, v1)
            m_ref[hd:hd + 1, :] = jnp.broadcast_to(m_new, (1, LANES))


def _scan_index_maps(n_batch, n_chunks):
    ctx0 = n_batch * n_chunks
    fwd = lambda b, s: (jnp.where(s == 0, ctx0 + b, b * n_chunks + s - 1), 0)
    bwd = lambda b, s: (jnp.where(s == 0, ctx0 + b, b * n_chunks + n_chunks - s), 0)
    return fwd, bwd


def _mlstm(q, k, v1, misc, gb_row, gb_col, n_batch, n_chunks):
    r = q.shape[0]
    n = SCAN_CHUNK
    fwd, bwd = _scan_index_maps(n_batch, n_chunks)
    const = lambda b, s: (0, 0)
    specs = []
    for imap in (fwd, bwd):
        specs += [pl.BlockSpec((n, W_GROUP), imap), pl.BlockSpec((n, W_GROUP), imap),
                  pl.BlockSpec((n, ML_HEADS * LANES), imap), pl.BlockSpec((n, LANES), imap)]
    specs += [pl.BlockSpec((1, LANES), const), pl.BlockSpec((LANES, 1), const)]
    return pl.pallas_call(
        _mlstm_kernel,
        out_shape=[jax.ShapeDtypeStruct((r, W_GROUP), F32)] * 2,
        grid=(n_batch, n_chunks + 1),
        in_specs=specs,
        out_specs=[pl.BlockSpec((n, W_GROUP), fwd), pl.BlockSpec((n, W_GROUP), bwd)],
        scratch_shapes=[pltpu.VMEM((2 * ML_HEADS, ML_DH, LANES), F32),
                        pltpu.VMEM((2 * ML_HEADS, LANES), F32)],
        compiler_params=_cparams(("arbitrary", "arbitrary")),
        name="mlstm_scan",
    )(q, k, v1, misc, q, k, v1, misc, gb_row, gb_col)


def _ssd_prep_kernel(x_ref, xp_ref, xn_ref, w_ref, b_ref, act_ref, xs_ref, *, bounds):
    tm = x_ref.shape[0]
    first, last = _segment_masks(pl.program_id(0) * tm, tm, bounds)
    x = x_ref[...]
    rid = lax.broadcasted_iota(jnp.int32, (tm, 1), 0)
    xp = jnp.where(rid == 0, xp_ref[HALO - 1:HALO, :], pltpu.roll(x, 1, 0))
    xn = jnp.where(rid == tm - 1, xn_ref[0:1, :], pltpu.roll(x, tm - 1, 0))
    xp = jnp.where(first, 0.0, xp)
    xn = jnp.where(last, 0.0, xn)
    y = _silu(w_ref[0:1, :] * xp + w_ref[1:2, :] * x + w_ref[2:3, :] * xn + b_ref[...])
    act_ref[...] = y.astype(act_ref.dtype)
    xs_ref[...] = y[:, 0:W_GROUP]


def _halo_specs(width, n_rows):
    per = ROW_TILE // HALO
    n_blocks = n_rows // HALO
    prev = pl.BlockSpec((HALO, width), lambda i: (jnp.maximum(i * per - 1, 0), 0))
    nxt = pl.BlockSpec((HALO, width), lambda i: (jnp.minimum((i + 1) * per, n_blocks - 1), 0))
    return prev, nxt


def _ssd_prep(xbc, conv_w, conv_b, bounds):
    r, width = xbc.shape
    row = lambda i: (i, 0)
    prev, nxt = _halo_specs(width, r)
    return pl.pallas_call(
        functools.partial(_ssd_prep_kernel, bounds=bounds),
        out_shape=[jax.ShapeDtypeStruct((r, width), BF16), jax.ShapeDtypeStruct((r, W_GROUP), F32)],
        grid=(r // ROW_TILE,),
        in_specs=[pl.BlockSpec((ROW_TILE, width), row), prev, nxt,
                  pl.BlockSpec((8, width), lambda i: (0, 0)),
                  pl.BlockSpec((1, width), lambda i: (0, 0))],
        out_specs=[pl.BlockSpec((ROW_TILE, width), row), pl.BlockSpec((ROW_TILE, W_GROUP), row)],
        compiler_params=_cparams(("arbitrary",)),
        name="ssd_conv",
    )(xbc, xbc, xbc, conv_w, conv_b)


def _ssd_kernel(af_ref, mf_ref, ab_ref, mb_ref, pr_ref, pc_ref, yf_ref, yb_ref, st_ref):
    n = af_ref.shape[0]

    @pl.when(pl.program_id(1) == 0)
    def _():
        st_ref[...] = jnp.zeros_like(st_ref)

    tril, triu = _tri_masks(n)
    trilf = tril.astype(F32)
    triuf = triu.astype(F32)
    dirs = ((af_ref, mf_ref, yf_ref, tril, trilf, triuf, n - 1),
            (ab_ref, mb_ref, yb_ref, triu, triuf, trilf, 0))
    for d, (act_ref, misc_ref, out_ref, mask, mask_f, mask_tf, last) in enumerate(dirs):
        dt_c = _softplus(misc_ref[...] + pr_ref[0:1, :])
        dt_r = _softplus(misc_ref[...].T + pc_ref[:, 0:1])
        cs_c = _dot_hi(mask_f, dt_c * pr_ref[1:2, :])
        cs_r = _dot_hi((dt_r * pc_ref[:, 1:2])[0:32, :], mask_tf)
        for g in range(SSD_GROUPS):
            bm = act_ref[:, W_GROUP + g * SSD_STATE:W_GROUP + (g + 1) * SSD_STATE]
            cm = act_ref[:, W_GROUP + SSD_GROUPS * SSD_STATE + g * SSD_STATE:
                         W_GROUP + SSD_GROUPS * SSD_STATE + (g + 1) * SSD_STATE]
            gmat = _dot_nt(cm, bm)
            for hh in range(SSD_HEADS // SSD_GROUPS):
                h = g * (SSD_HEADS // SSD_GROUPS) + hh
                hd = d * SSD_HEADS + h
                idx = MISC_DT + d * SSD_HEADS + h
                cs_col = cs_c[:, idx:idx + 1]
                cs_row = cs_r[idx:idx + 1, :]
                cs_last = cs_col[last:last + 1, :]
                dt_col = dt_c[:, idx:idx + 1]
                dt_row = dt_r[idx:idx + 1, :]
                xh = act_ref[:, h * SSD_HEADDIM:(h + 1) * SSD_HEADDIM]
                state = st_ref[hd]
                decay = jnp.exp(jnp.where(mask, cs_col - cs_row, -jnp.inf))
                scores = (gmat * decay * dt_row).astype(BF16)
                y = _dot(scores, xh) + jnp.exp(cs_col) * _dot(cm, state.astype(BF16))
                out_ref[:, h * SSD_HEADDIM:(h + 1) * SSD_HEADDIM] = y
                wk = (bm.astype(F32) * (jnp.exp(cs_last - cs_col) * dt_col)).astype(BF16)
                st_ref[hd] = jnp.exp(cs_last) * state + _dot_tn(wk, xh)


def _ssd_scan(act, misc, p_row, p_col, n_batch, n_chunks):
    r, width = act.shape
    n = SCAN_CHUNK
    fwd, bwd = _scan_index_maps(n_batch, n_chunks)
    const = lambda b, s: (0, 0)
    specs = []
    for imap in (fwd, bwd):
        specs += [pl.BlockSpec((n, width), imap), pl.BlockSpec((n, LANES), imap)]
    specs += [pl.BlockSpec((8, LANES), const), pl.BlockSpec((LANES, 8), const)]
    return pl.pallas_call(
        _ssd_kernel,
        out_shape=[jax.ShapeDtypeStruct((r, W_GROUP), F32)] * 2,
        grid=(n_batch, n_chunks + 1),
        in_specs=specs,
        out_specs=[pl.BlockSpec((n, W_GROUP), fwd), pl.BlockSpec((n, W_GROUP), bwd)],
        scratch_shapes=[pltpu.VMEM((2 * SSD_HEADS, SSD_STATE, SSD_HEADDIM), F32)],
        compiler_params=_cparams(("arbitrary", "arbitrary")),
        name="ssd_scan",
    )(act, misc, act, misc, p_row, p_col)


def _mla_prep_kernel(c_ref, misc_ref, cos_ref, sin_ref, qn_ref, kvn_ref, wqa_ref, wqb_ref,
                     wka_ref, wv_ref, ea_ref, eb_ref, q_ref, k_ref, v_ref):
    scale = (MLA_NOPE + MLA_ROPE) ** -0.5
    cq = _rms(c_ref[:, 0:MLA_Q_LORA], qn_ref[...]).astype(BF16)
    ckv = _rms(c_ref[:, MLA_Q_LORA:MLA_Q_LORA + MLA_KV_LORA], kvn_ref[...]).astype(BF16)
    cos = cos_ref[...]
    sin = sin_ref[...]
    qa = _dot(cq, wqa_ref[...])
    qb = _dot(cq, wqb_ref[...])
    ka = _dot(ckv, wka_ref[...])
    v = _dot(ckv, wv_ref[...])
    misc = misc_ref[...]
    kr = _dot_hi(misc, ea_ref[...]) * cos + _dot_hi(misc, eb_ref[...]) * sin
    lane = lax.broadcasted_iota(jnp.int32, (1, LANES), 1)
    one = jnp.where(lane == MLA_V, 1.0, 0.0)
    for h in range(MLA_HEADS):
        sl = slice(h * LANES, (h + 1) * LANES)
        q_ref[:, sl] = ((qa[:, sl] * cos + qb[:, sl] * sin) * scale).astype(BF16)
        k_ref[:, sl] = (ka[:, sl] + kr).astype(BF16)
        v_ref[:, sl] = (v[:, sl] + one).astype(BF16)


def _mla_prep(mlac, misc, cos, sin, qn, kvn, wqa, wqb, wka, wv, ea, eb):
    r = mlac.shape[0]
    row = lambda i: (i, 0)
    full = lambda a: pl.BlockSpec(a.shape, lambda i: (0, 0))
    hw = MLA_HEADS * LANES
    return pl.pallas_call(
        _mla_prep_kernel,
        out_shape=[jax.ShapeDtypeStruct((r, hw), BF16)] * 3,
        grid=(r // ROW_TILE,),
        in_specs=[pl.BlockSpec((ROW_TILE, mlac.shape[1]), row), pl.BlockSpec((ROW_TILE, LANES), row),
                  pl.BlockSpec((ROW_TILE, LANES), row), pl.BlockSpec((ROW_TILE, LANES), row),
                  full(qn), full(kvn), full(wqa), full(wqb), full(wka), full(wv), full(ea), full(eb)],
        out_specs=[pl.BlockSpec((ROW_TILE, hw), row)] * 3,
        compiler_params=_cparams(("arbitrary",)),
        name="mla_prep",
    )(mlac, misc, cos, sin, qn, kvn, wqa, wqb, wka, wv, ea, eb)


def _attn_chunk(q, k, v, m_ref, acc_ref):
    s = _dot_nt(q, k)
    m_old = m_ref[...]
    m_new = jnp.maximum(m_old, jnp.max(s, axis=-1, keepdims=True))
    p = jnp.exp(s - m_new).astype(BF16)
    acc_ref[...] = jnp.exp(m_old - m_new) * acc_ref[...] + _dot(p, v)
    m_ref[...] = m_new


def _attn_finish(o_ref, h, acc_ref):
    acc = acc_ref[...]
    o_ref[:, h * MLA_V:(h + 1) * MLA_V] = acc[:, 0:MLA_V] / acc[:, MLA_V:MLA_V + 1]


def _attn_lat_kernel(q_ref, kl_ref, vl_ref, kc_ref, vc_ref, o_ref, m_ref, acc_ref):
    n_chunks = kl_ref.shape[0] // ATTN_TK
    for h in range(MLA_HEADS):
        sl = slice(h * LANES, (h + 1) * LANES)
        q = q_ref[:, sl]
        m_ref[...] = jnp.full_like(m_ref, -jnp.inf)
        acc_ref[...] = jnp.zeros_like(acc_ref)

        def body(c, carry):
            rows = pl.ds(pl.multiple_of(c * ATTN_TK, ATTN_TK), ATTN_TK)
            _attn_chunk(q, kl_ref[rows, sl], vl_ref[rows, sl], m_ref, acc_ref)
            return carry

        lax.fori_loop(0, n_chunks, body, 0)
        _attn_chunk(q, kc_ref[:, sl], vc_ref[:, sl], m_ref, acc_ref)
        _attn_finish(o_ref, h, acc_ref)


def _attn_ctx_kernel(q_ref, kc_ref, vc_ref, prev_ref, o_ref, m_ref, acc_ref):
    del prev_ref
    for h in range(MLA_HEADS):
        sl = slice(h * LANES, (h + 1) * LANES)
        m_ref[...] = jnp.full_like(m_ref, -jnp.inf)
        acc_ref[...] = jnp.zeros_like(acc_ref)
        _attn_chunk(q_ref[:, sl], kc_ref[:, sl], vc_ref[:, sl], m_ref, acc_ref)
        _attn_finish(o_ref, h, acc_ref)


def _attention(q, k, v, n_batch, t_lat, t_ctx):
    r, hw = q.shape
    nq = t_lat // ATTN_TQ
    ctx_blk = n_batch * t_lat // t_ctx
    lat = pl.pallas_call(
        _attn_lat_kernel,
        out_shape=jax.ShapeDtypeStruct((r, W_GROUP), F32),
        grid=(n_batch, nq),
        in_specs=[pl.BlockSpec((ATTN_TQ, hw), lambda b, i: (b * nq + i, 0)),
                  pl.BlockSpec((t_lat, hw), lambda b, i: (b, 0)),
                  pl.BlockSpec((t_lat, hw), lambda b, i: (b, 0)),
                  pl.BlockSpec((t_ctx, hw), lambda b, i: (ctx_blk + b, 0)),
                  pl.BlockSpec((t_ctx, hw), lambda b, i: (ctx_blk + b, 0))],
        out_specs=pl.BlockSpec((ATTN_TQ, W_GROUP), lambda b, i: (b * nq + i, 0)),
        scratch_shapes=[pltpu.VMEM((ATTN_TQ, 1), F32), pltpu.VMEM((ATTN_TQ, LANES), F32)],
        compiler_params=_cparams(("arbitrary", "arbitrary")),
        name="mla_attention_latent",
    )(q, k, v, k, v)
    ctx_map = lambda b: (ctx_blk + b, 0)
    return pl.pallas_call(
        _attn_ctx_kernel,
        out_shape=jax.ShapeDtypeStruct((r, W_GROUP), F32),
        grid=(n_batch,),
        in_specs=[pl.BlockSpec((t_ctx, hw), ctx_map), pl.BlockSpec((t_ctx, hw), ctx_map),
                  pl.BlockSpec((t_ctx, hw), ctx_map), pl.BlockSpec(memory_space=pl.ANY)],
        out_specs=pl.BlockSpec((t_ctx, W_GROUP), ctx_map),
        scratch_shapes=[pltpu.VMEM((t_ctx, 1), F32), pltpu.VMEM((t_ctx, LANES), F32)],
        input_output_aliases={3: 0},
        compiler_params=_cparams(("arbitrary",)),
        name="mla_attention_context",
    )(q, k, v, lat)


def _s5_tables_kernel(p_ref, bre_ref, bim_ref, cre_ref, cim_ref, ctre_ref, ctim_ref,
                      klag_ref, mre_ref, mim_ref, cwre_ref, cwim_ref, pwre_ref, pwim_ref):
    n_lag = S5_CHUNK
    width = n_lag * S5_GROUP
    lr = jnp.minimum(p_ref[:, 0:1], -1e-4)
    li = p_ref[:, 1:2]
    dt = jnp.exp(p_ref[:, 2:3])
    kk = lax.broadcasted_iota(jnp.int32, (1, LANES), 1).astype(F32)
    mag = jnp.exp(lr * dt * kk)
    ang = li * dt * kk
    pw_re = mag * jnp.cos(ang)
    pw_im = mag * jnp.sin(ang)
    pwre_ref[...] = pw_re
    pwim_ref[...] = pw_im
    ab_re = pw_re[:, 1:2]
    ab_im = pw_im[:, 1:2]
    den = lr * lr + li * li
    f_re = ((ab_re - 1.0) * lr + ab_im * li) / den
    f_im = (ab_im * lr - (ab_re - 1.0) * li) / den
    bb_re = f_re * bre_ref[...] - f_im * bim_ref[...]
    bb_im = f_re * bim_ref[...] + f_im * bre_ref[...]

    shift = S5_GROUP.bit_length() - 1
    lane_k = lax.shift_right_logical(lax.broadcasted_iota(jnp.int32, (LANES, width), 1), shift)
    row_k = lax.broadcasted_iota(jnp.int32, (LANES, width), 0)
    e0 = (lane_k == row_k).astype(F32)
    e1 = (lane_k + 1 == row_k).astype(F32)
    lane_j = lax.broadcasted_iota(jnp.int32, (S5_GROUP, width), 1) & (S5_GROUP - 1)
    row_j = lax.broadcasted_iota(jnp.int32, (S5_GROUP, width), 0)
    et = (lane_j == row_j).astype(F32)

    pe_re = _dot_hi(pw_re, e0)
    pe_im = _dot_hi(pw_im, e0)
    bt_re = _dot_hi(bb_re, et)
    bt_im = _dot_hi(bb_im, et)
    m_re = pe_re * bt_re - pe_im * bt_im
    m_im = pe_re * bt_im + pe_im * bt_re
    mre_ref[...] = m_re
    mim_ref[...] = m_im
    klag_ref[...] = _dot_hi(cre_ref[...], m_re) - _dot_hi(cim_ref[...], m_im)
    p1_re = _dot_hi(pw_re, e1)
    p1_im = _dot_hi(pw_im, e1)
    ct_re = _dot_hi(ctre_ref[...], et)
    ct_im = _dot_hi(ctim_ref[...], et)
    cwre_ref[...] = ct_re * p1_re - ct_im * p1_im
    cwim_ref[...] = ct_re * p1_im + ct_im * p1_re


def _s5_tables(params, b_re, b_im, c_re, c_im):
    depth = params.shape[0]
    g, n, j = S5_NGROUPS, S5_STATE, S5_GROUP
    width = S5_CHUNK * j
    n_all = depth * 2 * g
    p = params.reshape(n_all, n, 4)
    bre = b_re.reshape(depth * g, n, j)
    bim = b_im.reshape(depth * g, n, j)
    cre = c_re.reshape(depth * g, j, n)
    cim = c_im.reshape(depth * g, j, n)
    ctre = jnp.swapaxes(cre, 1, 2)
    ctim = jnp.swapaxes(cim, 1, 2)
    shared = lambda i: ((i // (2 * g)) * g + i % g, 0, 0)
    own = lambda i: (i, 0, 0)
    outs = pl.pallas_call(
        _s5_tables_kernel,
        out_shape=[jax.ShapeDtypeStruct((n_all, j, width), F32)]
        + [jax.ShapeDtypeStruct((n_all, n, width), F32)] * 4
        + [jax.ShapeDtypeStruct((n_all, n, LANES), F32)] * 2,
        grid=(n_all,),
        in_specs=[pl.BlockSpec((None, n, 4), own),
                  pl.BlockSpec((None, n, j), shared), pl.BlockSpec((None, n, j), shared),
                  pl.BlockSpec((None, j, n), shared), pl.BlockSpec((None, j, n), shared),
                  pl.BlockSpec((None, n, j), shared), pl.BlockSpec((None, n, j), shared)],
        out_specs=[pl.BlockSpec((None, j, width), own)]
        + [pl.BlockSpec((None, n, width), own)] * 4
        + [pl.BlockSpec((None, n, LANES), own)] * 2,
        compiler_params=_cparams(("arbitrary",)),
        name="s5_tables",
    )(p, bre, bim, cre, cim, ctre, ctim)
    klag, m_re, m_im, cw_re, cw_im, pw_re, pw_im = outs
    lc = S5_CHUNK
    klag = klag.reshape(depth, 2, g, j, lc, j)
    m_re = m_re.reshape(depth, 2, g, n, lc, j)
    m_im = m_im.reshape(depth, 2, g, n, lc, j)
    cw_re = cw_re.reshape(depth, 2, g, n, lc, j)
    cw_im = cw_im.reshape(depth, 2, g, n, lc, j)

    kf = jnp.transpose(klag[:, 0], (0, 1, 3, 2, 4))
    kb = jnp.transpose(klag[:, 1], (0, 1, 3, 2, 4))
    lags = jnp.concatenate([jnp.flip(kb[:, :, 1:], axis=2), kf[:, :, 0:1] + kb[:, :, 0:1], kf[:, :, 1:]], axis=2)
    lags = jnp.pad(lags, ((0, 0), (0, 0), (0, 1), (0, 0), (0, 0)))
    skew = jnp.tile(lags, (1, 1, lc, 1, 1))[:, :, :lc * (2 * lc - 1)]
    skew = skew.reshape(depth, g, lc, 2 * lc - 1, j, j)[:, :, :, lc - 1:]
    toep = jnp.transpose(skew, (0, 1, 2, 5, 3, 4)).reshape(depth, g, width, width).astype(BF16)

    def to_rows(m):
        return jnp.transpose(m, (0, 1, 3, 4, 2)).reshape(depth, g, width, n)

    wz = jnp.concatenate([to_rows(jnp.flip(m_re[:, 0], axis=3)), to_rows(jnp.flip(m_im[:, 0], axis=3)),
                          to_rows(m_re[:, 1]), to_rows(m_im[:, 1])], axis=-1).astype(BF16)

    def to_cols(m):
        return m.reshape(depth, g, n, width)

    wy = jnp.concatenate([to_cols(cw_re[:, 0]), -to_cols(cw_im[:, 0]),
                          to_cols(jnp.flip(cw_re[:, 1], axis=3)), -to_cols(jnp.flip(cw_im[:, 1], axis=3))],
                         axis=2).astype(BF16)

    pw_re = pw_re.reshape(depth, 2, g, n, LANES)[..., lc]
    pw_im = pw_im.reshape(depth, 2, g, n, LANES)[..., lc]
    ca = jnp.concatenate([pw_re, pw_re], axis=-1)
    cb = jnp.concatenate([-pw_im, pw_im], axis=-1)
    ca = jnp.transpose(ca, (0, 2, 1, 3)).reshape(depth, 2 * g, LANES)
    cb = jnp.transpose(cb, (0, 2, 1, 3)).reshape(depth, 2 * g, LANES)
    return toep, wz, wy, ca, cb


def _s5_local_kernel(a_ref, wz_ref, z_ref):
    z_ref[...] = _dot(a_ref[...], wz_ref[...])


def _s5_scan_kernel(z_ref, ca_ref, cb_ref, x_ref, xf_ref, xb_ref, *, n_batch, n_chunks, n_ctx):
    ca = ca_ref[...]
    cb = cb_ref[...]
    zero = jnp.zeros(ca.shape, F32)

    def step(x, z):
        return x * ca + pltpu.roll(x, LANES // 2, 1) * cb + z

    def body(i, carry):
        cf = i
        cbk = jnp.where(i < n_ctx, n_ctx - 1 - i, n_chunks - 1 - (i - n_ctx))
        new = []
        for b in range(n_batch):
            xf, xb = carry[2 * b], carry[2 * b + 1]
            xf_ref[b * n_chunks + cf] = xf
            xb_ref[b * n_chunks + cbk] = xb
            new += [step(xf, z_ref[b * n_chunks + cf]), step(xb, z_ref[b * n_chunks + cbk])]
        return tuple(new)

    lax.fori_loop(0, n_chunks, body, (zero,) * (2 * n_batch))
    row = lax.broadcasted_iota(jnp.int32, x_ref.shape, 1)
    x_ref[...] = jnp.where((row & 1) == 0, xf_ref[...], xb_ref[...])


def _s5_out_kernel(a_ref, t_ref, x_ref, wy_ref, y_ref):
    y_ref[...] = _dot(a_ref[...], t_ref[...]) + _dot(x_ref[...].astype(BF16), wy_ref[...])


def _s5_mix(a, toep, wz, wy, ca, cb, n_batch, n_chunks, n_ctx):
    g, nr, width = a.shape
    sw = 4 * S5_STATE
    z = pl.pallas_call(
        _s5_local_kernel,
        out_shape=jax.ShapeDtypeStruct((nr, g * sw), F32),
        grid=(g,),
        in_specs=[pl.BlockSpec((None, nr, width), lambda i: (i, 0, 0)),
                  pl.BlockSpec((None, width, sw), lambda i: (i, 0, 0))],
        out_specs=pl.BlockSpec((nr, sw), lambda i: (0, i)),
        compiler_params=_cparams(("arbitrary",)),
        name="s5_local_state",
    )(a, wz)
    z3 = z.reshape(nr, 2 * g, LANES)
    whole = lambda shp: pl.BlockSpec(shp, lambda: tuple(0 for _ in shp))
    x3 = pl.pallas_call(
        functools.partial(_s5_scan_kernel, n_batch=n_batch, n_chunks=n_chunks, n_ctx=n_ctx),
        out_shape=jax.ShapeDtypeStruct(z3.shape, F32),
        in_specs=[whole(z3.shape), whole(ca.shape), whole(cb.shape)],
        out_specs=whole(z3.shape),
        scratch_shapes=[pltpu.VMEM(z3.shape, F32), pltpu.VMEM(z3.shape, F32)],
        compiler_params=pltpu.CompilerParams(vmem_limit_bytes=VMEM_LIMIT),
        name="s5_chunk_scan",
    )(z3, ca, cb)
    x = x3.reshape(nr, g * sw)
    return pl.pallas_call(
        _s5_out_kernel,
        out_shape=jax.ShapeDtypeStruct((g, nr, width), F32),
        grid=(g,),
        in_specs=[pl.BlockSpec((None, nr, width), lambda i: (i, 0, 0)),
                  pl.BlockSpec((None, width, width), lambda i: (i, 0, 0)),
                  pl.BlockSpec((nr, sw), lambda i: (0, i)),
                  pl.BlockSpec((None, sw, width), lambda i: (i, 0, 0))],
        out_specs=pl.BlockSpec((None, nr, width), lambda i: (i, 0, 0)),
        compiler_params=_cparams(("arbitrary",)),
        name="s5_output",
    )(a, toep, x, wy)


def _outproj_kernel(x_ref, mod_ref, hf_ref, hb_ref, og_ref, att_ref, yf_ref, yb_ref, xs_ref, z_ref,
                    s5_ref, u_ref, mln_ref, sd_ref, sn_ref, s5d_ref, glu_ref, wo_ref, n2_ref,
                    xo_ref, h2_ref):
    h = hf_ref[...] + hb_ref[...]
    parts = []
    for i in range(ML_HEADS):
        hh = h[:, i * ML_DH:(i + 1) * ML_DH]
        parts.append(hh * lax.rsqrt(jnp.mean(hh * hh, axis=-1, keepdims=True) + EPS))
    ya = jnp.concatenate(parts, axis=-1) * mln_ref[...] * _sigmoid(og_ref[...])
    yc = (sd_ref[...] * xs_ref[...] + yf_ref[...] + yb_ref[...]) * _silu(z_ref[...])
    yc = _rms(yc, sn_ref[...])
    yd = _gelu_tanh(s5d_ref[...] * u_ref[...] + s5_ref[...])
    yd = yd * _sigmoid(_dot(yd.astype(BF16), glu_ref[...]))
    y = (_dot(ya.astype(BF16), wo_ref[0:W_GROUP, :])
         + _dot(att_ref[...].astype(BF16), wo_ref[W_GROUP:2 * W_GROUP, :])
         + _dot(yc.astype(BF16), wo_ref[2 * W_GROUP:3 * W_GROUP, :])
         + _dot(yd.astype(BF16), wo_ref[3 * W_GROUP:4 * W_GROUP, :]))
    x = x_ref[...] + mod_ref[2:3, :] * y
    xo_ref[...] = x
    h2 = _rms(x, n2_ref[...]) * (1.0 + mod_ref[4:5, :]) + mod_ref[3:4, :]
    h2_ref[...] = h2.astype(BF16)


def _out_projection(x, mod, seg_of_tile, acts, params):
    r, d = x.shape
    row = lambda i: (i, 0)
    full = lambda a: pl.BlockSpec(a.shape, lambda i: (0, 0))
    return pl.pallas_call(
        _outproj_kernel,
        out_shape=[jax.ShapeDtypeStruct((r, d), F32), jax.ShapeDtypeStruct((r, d), BF16)],
        grid=(r // ROW_TILE,),
        in_specs=[pl.BlockSpec((ROW_TILE, d), row),
                  pl.BlockSpec((None, 8, d), lambda i: (seg_of_tile(i), 0, 0))]
        + [pl.BlockSpec((ROW_TILE, W_GROUP), row) for _ in acts]
        + [full(p) for p in params],
        out_specs=[pl.BlockSpec((ROW_TILE, d), row)] * 2,
        compiler_params=_cparams(("arbitrary",)),
        name="out_projection",
    )(x, mod, *acts, *params)


def _ffn_kernel(x_ref, mod_ref, h_ref, hp_ref, hn_ref, wup_ref, cw_ref, wdn_ref, o_ref, acc_ref, *, bounds):
    tm = h_ref.shape[0]
    first, last = _segment_masks(pl.program_id(0) * tm, tm, bounds)
    h = h_ref[...]
    h_ext = jnp.concatenate([hp_ref[...], h, hn_ref[...]], axis=0)
    n_ext = tm + 2 * HALO
    acc_ref[...] = jnp.zeros_like(acc_ref)

    def body(j, carry):
        c0 = pl.multiple_of(j * FF_CHUNK, FF_CHUNK)
        u = _dot(h, wup_ref[:, pl.ds(c0, FF_CHUNK)])
        g_ext = _dot(h_ext, wup_ref[:, pl.ds(pl.multiple_of(D_FF + c0, FF_CHUNK), FF_CHUNK)])
        gp = jnp.where(first, 0.0, pltpu.roll(g_ext, 1, 0)[HALO:HALO + tm])
        gn = jnp.where(last, 0.0, pltpu.roll(g_ext, n_ext - 1, 0)[HALO:HALO + tm])
        gc = g_ext[HALO:HALO + tm]
        cw = cw_ref[:, pl.ds(c0, FF_CHUNK)]
        conv = cw[0:1, :] * gp + cw[1:2, :] * gc + cw[2:3, :] * gn
        act = (_silu(conv) * u).astype(BF16)
        acc_ref[...] += _dot(act, wdn_ref[pl.ds(c0, FF_CHUNK), :])
        return carry

    lax.fori_loop(0, D_FF // FF_CHUNK, body, 0)
    o_ref[...] = x_ref[...] + mod_ref[5:6, :] * acc_ref[...]


def _conv_ffn(x, mod, seg_of_tile, h2, w_up, conv_w, w_down, bounds):
    r, d = x.shape
    row = lambda i: (i, 0)
    prev, nxt = _halo_specs(d, r)
    full = lambda a: pl.BlockSpec(a.shape, lambda i: (0, 0))
    return pl.pallas_call(
        functools.partial(_ffn_kernel, bounds=bounds),
        out_shape=jax.ShapeDtypeStruct((r, d), F32),
        grid=(r // ROW_TILE,),
        in_specs=[pl.BlockSpec((ROW_TILE, d), row),
                  pl.BlockSpec((None, 8, d), lambda i: (seg_of_tile(i), 0, 0)),
                  pl.BlockSpec((ROW_TILE, d), row), prev, nxt,
                  full(w_up), full(conv_w), full(w_down)],
        out_specs=pl.BlockSpec((ROW_TILE, d), row),
        scratch_shapes=[pltpu.VMEM((ROW_TILE, d), F32)],
        compiler_params=_cparams(("arbitrary",)),
        name="conv_ffn",
    )(x, mod, h2, h2, h2, w_up, conv_w, w_down)


def _final_norm_kernel(x_ref, g_ref, o_ref):
    o_ref[...] = _rms(x_ref[...], g_ref[...])


def _final_norm(x, g, n_rows):
    d = x.shape[1]
    return pl.pallas_call(
        _final_norm_kernel,
        out_shape=jax.ShapeDtypeStruct((n_rows, d), F32),
        grid=(n_rows // ROW_TILE,),
        in_specs=[pl.BlockSpec((ROW_TILE, d), lambda i: (i, 0)), pl.BlockSpec((1, d), lambda i: (0, 0))],
        out_specs=pl.BlockSpec((ROW_TILE, d), lambda i: (i, 0)),
        compiler_params=_cparams(("arbitrary",)),
        name="final_norm",
    )(x, g)


def _rope_tables(t_lat, n_ctx_rows):
    rows = t_lat // GRID_W
    row = jnp.broadcast_to(jnp.arange(rows)[:, None], (rows, GRID_W)).reshape(-1).astype(F32)
    col = jnp.broadcast_to(jnp.arange(GRID_W)[None, :], (rows, GRID_W)).reshape(-1).astype(F32)
    n_freq = MLA_ROPE // 4
    inv = ROPE_BASE ** (-jnp.arange(n_freq, dtype=F32) / n_freq)
    ang = jnp.concatenate([row[:, None] * inv, col[:, None] * inv], axis=-1)
    half = MLA_ROPE // 2
    pad = LANES - MLA_NOPE - MLA_ROPE

    def table(t, lead):
        return jnp.concatenate([jnp.full((t_lat, MLA_NOPE), lead, F32), t, t, jnp.zeros((t_lat, pad), F32)], axis=-1)

    cos = table(jnp.cos(ang), 1.0)
    sin = table(jnp.sin(ang), 0.0)
    lane = np.arange(LANES)
    cos_ctx = jnp.broadcast_to(jnp.asarray((lane < MLA_NOPE + 2 * half).astype(np.float32)), (n_ctx_rows, LANES))
    sin_ctx = jnp.zeros((n_ctx_rows, LANES), F32)
    return cos, sin, cos_ctx, sin_ctx


def _mla_weights(w_uq, w_ukv):
    half = MLA_ROPE // 2
    qd = MLA_NOPE + MLA_ROPE
    kd = MLA_NOPE + MLA_V
    hw = MLA_HEADS * LANES
    qa_idx = np.zeros((hw,), np.int32); qa_s = np.zeros((hw,), np.float32)
    qb_idx = np.zeros((hw,), np.int32); qb_s = np.zeros((hw,), np.float32)
    ka_idx = np.zeros((hw,), np.int32); ka_s = np.zeros((hw,), np.float32)
    v_idx = np.zeros((hw,), np.int32); v_s = np.zeros((hw,), np.float32)
    for h in range(MLA_HEADS):
        for i in range(qd):
            qa_idx[h * LANES + i] = h * qd + i
            qa_s[h * LANES + i] = 1.0
        for i in range(half):
            qb_idx[h * LANES + MLA_NOPE + i] = h * qd + MLA_NOPE + half + i
            qb_s[h * LANES + MLA_NOPE + i] = -1.0
            qb_idx[h * LANES + MLA_NOPE + half + i] = h * qd + MLA_NOPE + i
            qb_s[h * LANES + MLA_NOPE + half + i] = 1.0
        for i in range(MLA_NOPE):
            ka_idx[h * LANES + i] = h * kd + i
            ka_s[h * LANES + i] = 1.0
        for i in range(MLA_V):
            v_idx[h * LANES + i] = h * kd + MLA_NOPE + i
            v_s[h * LANES + i] = 1.0
    pick = lambda w, idx, s: (w[..., idx] * s).astype(BF16)
    ea = np.zeros((LANES, LANES), np.float32)
    eb = np.zeros((LANES, LANES), np.float32)
    for i in range(MLA_ROPE):
        ea[MISC_KR + i, MLA_NOPE + i] = 1.0
    for i in range(half):
        eb[MISC_KR + half + i, MLA_NOPE + i] = -1.0
        eb[MISC_KR + i, MLA_NOPE + half + i] = 1.0
    return (pick(w_uq, qa_idx, qa_s), pick(w_uq, qb_idx, qb_s), pick(w_ukv, ka_idx, ka_s),
            pick(w_ukv, v_idx, v_s), jnp.asarray(ea), jnp.asarray(eb))


def _lane_rows(vals, offset, n_rows=8):
    k, n = vals.shape
    out = jnp.zeros((n_rows, LANES), F32)
    return out.at[:k, offset:offset + n].set(vals)


def kernel(x, c, ctx, c_ctx, w_mod, b_mod, norm1, norm2, w_in, ml_gate_bias, ml_norm, mla_q_norm, mla_kv_norm, mla_w_uq, mla_w_ukv, ssd_conv_w, ssd_conv_b, ssd_a_log, ssd_dt_bias, ssd_d, ssd_norm, s5_a_re, s5_a_im, s5_log_dt, s5_b_re, s5_b_im, s5_c_re, s5_c_im, s5_d, s5_w_glu, w_out, ffn_w_up, ffn_conv_w, ffn_w_down, final_norm):
    n_batch, t_lat, d = x.shape
    t_ctx = ctx.shape[1]
    depth = w_mod.shape[0]
    assert d == D_MODEL and t_ctx == SCAN_CHUNK and t_lat % ROW_TILE == 0
    assert (n_batch * t_ctx) % ROW_TILE == 0 and t_lat % ATTN_TK == 0 and t_lat % GRID_W == 0
    n_lat = n_batch * t_lat
    n_rows = n_lat + n_batch * t_ctx
    n_scan = t_lat // SCAN_CHUNK
    bounds = tuple(b * t_lat for b in range(n_batch)) + tuple(n_lat + b * t_ctx for b in range(n_batch + 1))
    seg_of_tile = lambda i: jnp.minimum(i * ROW_TILE // t_lat, n_batch)

    c8 = jnp.zeros((8, d), F32).at[:n_batch].set(c).at[n_batch].set(c_ctx)
    mod_all = _modulation(c8, w_mod, b_mod)
    mod_all = mod_all.reshape(depth, 8, 6, d)[:, :n_batch + 1]
    mod_all = jnp.pad(mod_all, ((0, 0), (0, 0), (0, 2), (0, 0)))

    perm, pscale = _inproj_perm()
    w_in_p = (w_in[:, :, perm] * pscale).astype(BF16)
    gate_row = jax.vmap(lambda v: _lane_rows(v.reshape(1, -1), MISC_GATE, 1))(ml_gate_bias)
    gate_col = jnp.swapaxes(gate_row, 1, 2)
    ssd_a = -jnp.exp(ssd_a_log)
    ssd_row = jax.vmap(lambda bvec, avec: _lane_rows(jnp.stack([bvec.reshape(-1), avec.reshape(-1)]), MISC_DT))(
        ssd_dt_bias, ssd_a)
    ssd_col = jnp.swapaxes(ssd_row, 1, 2)
    ssd_cw = jnp.pad(ssd_conv_w, ((0, 0), (0, 8 - ssd_conv_w.shape[1]), (0, 0)))
    ssd_d_row = jnp.repeat(ssd_d, SSD_HEADDIM, axis=-1)[:, None, :]
    ffn_cw = jnp.pad(ffn_conv_w, ((0, 0), (0, 8 - ffn_conv_w.shape[1]), (0, 0)))
    cos_l, sin_l, cos_c, sin_c = _rope_tables(t_lat, n_batch * t_ctx)
    cos = jnp.concatenate([cos_l] * n_batch + [cos_c], axis=0)
    sin = jnp.concatenate([sin_l] * n_batch + [sin_c], axis=0)
    s5_params = jnp.stack([s5_a_re, s5_a_im, jnp.broadcast_to(s5_log_dt[..., None], s5_a_re.shape),
                           jnp.zeros_like(s5_a_re)], axis=-1)
    toep, wz, wy, s5_ca, s5_cb = _s5_tables(s5_params, s5_b_re, s5_b_im, s5_c_re, s5_c_im)
    n_s5_ctx = t_ctx // S5_CHUNK
    n_s5 = (t_ctx + t_lat) // S5_CHUNK

    xf = jnp.concatenate([x.reshape(n_lat, d), ctx.reshape(n_batch * t_ctx, d)], axis=0)
    for l in range(depth):
        mod = mod_all[l]
        q, k, v1, og, misc, mlac, z, xbc, u = _in_projection(xf, mod, norm1[l][None], w_in_p[l], seg_of_tile)
        hf, hb = _mlstm(q, k, v1, misc, gate_row[l], gate_col[l], n_batch, n_scan)
        wqa, wqb, wka, wv, ea, eb = _mla_weights(mla_w_uq[l], mla_w_ukv[l])
        qa, ka, va = _mla_prep(mlac, misc, cos, sin, mla_q_norm[l][None], mla_kv_norm[l][None],
                               wqa, wqb, wka, wv, ea, eb)
        att = _attention(qa, ka, va, n_batch, t_lat, t_ctx)
        act, xs = _ssd_prep(xbc, ssd_cw[l], ssd_conv_b[l][None], bounds)
        yf, yb = _ssd_scan(act, misc, ssd_row[l], ssd_col[l], n_batch, n_scan)
        ub = u.astype(BF16)
        u_lat = ub[:n_lat].reshape(n_batch, t_lat // S5_CHUNK, S5_CHUNK, S5_NGROUPS, S5_GROUP)
        u_ctx = ub[n_lat:].reshape(n_batch, n_s5_ctx, S5_CHUNK, S5_NGROUPS, S5_GROUP)
        a = jnp.concatenate([u_ctx, u_lat], axis=1)
        a = jnp.transpose(a, (3, 0, 1, 2, 4)).reshape(S5_NGROUPS, n_batch * n_s5, S5_CHUNK * S5_GROUP)
        y5 = _s5_mix(a, toep[l], wz[l], wy[l], s5_ca[l], s5_cb[l], n_batch, n_s5, n_s5_ctx)
        y5 = y5.reshape(S5_NGROUPS, n_batch, n_s5, S5_CHUNK, S5_GROUP)
        y5 = jnp.transpose(y5, (1, 2, 3, 0, 4)).reshape(n_batch, n_s5 * S5_CHUNK, W_GROUP)
        y5 = jnp.concatenate([y5[:, t_ctx:].reshape(n_lat, W_GROUP), y5[:, :t_ctx].reshape(-1, W_GROUP)], axis=0)
        acts = (hf, hb, og, att, yf, yb, xs, z, y5, u)
        params = (ml_norm[l][None], ssd_d_row[l], ssd_norm[l][None], s5_d[l][None],
                  s5_w_glu[l].astype(BF16), w_out[l].astype(BF16), norm2[l][None])
        xf, h2 = _out_projection(xf, mod, seg_of_tile, acts, params)
        xf = _conv_ffn(xf, mod, seg_of_tile, h2, ffn_w_up[l].astype(BF16), ffn_cw[l],
                       ffn_w_down[l].astype(BF16), bounds)
    out = _final_norm(xf, final_norm[None], n_lat)
    return out.reshape(n_batch, t_lat, d)
```

```python
import functools
import math

import numpy as np
import jax
import jax.numpy as jnp
from jax import lax
from jax.experimental import pallas as pl
from jax.experimental.pallas import tpu as pltpu

F32 = jnp.float32
BF16 = jnp.bfloat16
HI = lax.Precision.HIGHEST

D_MODEL = 1024
W_GROUP = 256
EPS = 1e-6
GRID_W = 64
ROPE_BASE = 10000.0

ML_HEADS = 4
ML_DH = 64

MLA_HEADS = 4
MLA_NOPE = 64
MLA_ROPE = 32
MLA_V = 64
MLA_Q_LORA = 256
MLA_KV_LORA = 128

SSD_HEADS = 4
SSD_HEADDIM = 64
SSD_GROUPS = 2
SSD_STATE = 128

S5_GROUP = 16
S5_NGROUPS = 16
S5_STATE = 64
S5_CHUNK = 64
S5_SW = 4 * S5_STATE

D_FF = 2816
FF_CHUNK = 256

ROW_TILE = 512
SCAN_CHUNK = 256
ATTN_TQ = 512
ATTN_TK = 512
HALO = 16
LANES = 128
VMEM_LIMIT = 56 * 1024 * 1024

MISC_GATE = 0
MISC_DT = 16
MISC_KR = 24


def _cparams(sem):
    return pltpu.CompilerParams(dimension_semantics=sem, vmem_limit_bytes=VMEM_LIMIT)


def _dot(a, b):
    return jnp.dot(a, b, preferred_element_type=F32)


def _dot_nt(a, b):
    return lax.dot_general(a, b, (((1,), (1,)), ((), ())), preferred_element_type=F32)


def _dot_tn(a, b):
    return lax.dot_general(a, b, (((0,), (0,)), ((), ())), preferred_element_type=F32)


def _dot_hi(a, b):
    return jnp.dot(a, b, preferred_element_type=F32, precision=HI)


def _sigmoid(x):
    return 1.0 / (1.0 + jnp.exp(-x))


def _silu(x):
    return x * _sigmoid(x)


def _log_sigmoid(x):
    return jnp.minimum(x, 0.0) - jnp.log(1.0 + jnp.exp(-jnp.abs(x)))


def _softplus(x):
    return jnp.maximum(x, 0.0) + jnp.log(1.0 + jnp.exp(-jnp.abs(x)))


def _gelu_tanh(x):
    return 0.5 * x * (1.0 + jnp.tanh(math.sqrt(2.0 / math.pi) * (x + 0.044715 * x * x * x)))


def _rms(x, g):
    return x * lax.rsqrt(jnp.mean(x * x, axis=-1, keepdims=True) + EPS) * g


def _segment_masks(row0, n_rows, bounds):
    r = row0 + lax.broadcasted_iota(jnp.int32, (n_rows, 1), 0)
    first = r == bounds[0]
    last = r == bounds[1] - 1
    for s in bounds[1:-1]:
        first = first | (r == s)
    for e in bounds[2:]:
        last = last | (r == e - 1)
    return first, last


def _mod_kernel(c_ref, w_ref, b_ref, o_ref):
    c = c_ref[...]
    s = _silu(c).astype(BF16)
    o_ref[...] = _dot(s, w_ref[...].astype(BF16)) + b_ref[...]


def _modulation(c8, w_mod, b_mod):
    depth, d, d6 = w_mod.shape
    return pl.pallas_call(
        _mod_kernel,
        out_shape=jax.ShapeDtypeStruct((depth, 8, d6), F32),
        grid=(depth, d6 // d),
        in_specs=[
            pl.BlockSpec((8, d), lambda l, j: (0, 0)),
            pl.BlockSpec((None, d, d), lambda l, j: (l, 0, j)),
            pl.BlockSpec((None, 1, d), lambda l, j: (l, 0, j)),
        ],
        out_specs=pl.BlockSpec((None, 8, d), lambda l, j: (l, 0, j)),
        compiler_params=_cparams(("arbitrary", "arbitrary")),
        name="modulation",
    )(c8, w_mod, b_mod.reshape(depth, 1, d6))


_IN_SPLITS = (("q", 256), ("k", 256), ("v1", 512), ("og", 256), ("misc", 128),
              ("mlac", 384), ("z", 256), ("xbc", 768))
_IN_TOTAL = sum(w for _, w in _IN_SPLITS)
_IN_S5 = 2488


def _inproj_perm():
    ml, mla, ssd = 0, 1040, 1456
    cols = []
    cols += list(range(ml, ml + 256))
    cols += list(range(ml + 256, ml + 512))
    for h in range(ML_HEADS):
        cols += list(range(ml + 512 + h * 64, ml + 512 + (h + 1) * 64)) + [-1] * 64
    cols += list(range(ml + 768, ml + 1024))
    misc = list(range(ml + 1024, ml + 1040))
    misc += list(range(ssd + 1024, ssd + 1032))
    misc += list(range(mla + 384, mla + 416))
    cols += misc + [-1] * (LANES - len(misc))
    cols += list(range(mla, mla + 384))
    cols += list(range(ssd, ssd + 256))
    cols += list(range(ssd + 256, ssd + 1024))
    cols = np.asarray(cols, np.int32)
    scale = np.ones((len(cols),), np.float32)
    scale[256:512] = ML_DH ** -0.5
    scale[cols < 0] = 0.0
    return np.maximum(cols, 0), scale


def _inproj_kernel(x_ref, mod_ref, g_ref, w_ref, wut_ref, q_ref, k_ref, v1_ref, og_ref, misc_ref,
                   mlac_ref, z_ref, xbc_ref, ut_ref):
    x = x_ref[...]
    h = _rms(x, g_ref[...]) * (1.0 + mod_ref[1:2, :]) + mod_ref[0:1, :]
    hb = h.astype(BF16)
    outs = (q_ref, k_ref, v1_ref, og_ref, misc_ref, mlac_ref, z_ref, xbc_ref)
    off = 0
    for (name, width), o_ref in zip(_IN_SPLITS, outs):
        y = _dot(hb, w_ref[:, off:off + width])
        if name == "v1":
            lane = lax.broadcasted_iota(jnp.int32, (1, width), 1)
            y = y + jnp.where((lane & (LANES - 1)) == MLA_V, 1.0, 0.0)
        o_ref[...] = y.astype(o_ref.dtype)
        off += width
    ut_ref[...] = _dot_nt(wut_ref[...], hb)


def _in_projection(x, mod, g, w, wut, seg_of_tile):
    r, d = x.shape
    dtypes = dict(q=BF16, k=BF16, v1=BF16, og=F32, misc=F32, mlac=F32, z=F32, xbc=F32)
    row = lambda i: (i, 0)
    return pl.pallas_call(
        _inproj_kernel,
        out_shape=[jax.ShapeDtypeStruct((r, width), dtypes[name]) for name, width in _IN_SPLITS]
        + [jax.ShapeDtypeStruct((W_GROUP, r), F32)],
        grid=(r // ROW_TILE,),
        in_specs=[
            pl.BlockSpec((ROW_TILE, d), row),
            pl.BlockSpec((None, 8, d), lambda i: (seg_of_tile(i), 0, 0)),
            pl.BlockSpec((1, d), lambda i: (0, 0)),
            pl.BlockSpec((d, _IN_TOTAL), lambda i: (0, 0)),
            pl.BlockSpec((W_GROUP, d), lambda i: (0, 0)),
        ],
        out_specs=[pl.BlockSpec((ROW_TILE, width), row) for _, width in _IN_SPLITS]
        + [pl.BlockSpec((W_GROUP, ROW_TILE), lambda i: (0, i))],
        compiler_params=_cparams(("arbitrary",)),
        name="in_projection",
    )(x, mod, g, w, wut)


def _tri_masks(n):
    row = lax.broadcasted_iota(jnp.int32, (n, n), 0)
    col = lax.broadcasted_iota(jnp.int32, (n, n), 1)
    return col <= row, col >= row


def _mlstm_kernel(qf_ref, kf_ref, vf_ref, mf_ref, qb_ref, kb_ref, vb_ref, mb_ref,
                  gbr_ref, gbc_ref, hf_ref, hb_ref, st_ref, m_ref):
    n = qf_ref.shape[0]

    @pl.when(pl.program_id(1) == 0)
    def _():
        st_ref[...] = jnp.zeros_like(st_ref)
        m_ref[...] = jnp.zeros_like(m_ref)

    tril, triu = _tri_masks(n)
    trilf = tril.astype(F32)
    triuf = triu.astype(F32)
    dirs = ((qf_ref, kf_ref, vf_ref, mf_ref, hf_ref, tril, trilf, triuf, n - 1),
            (qb_ref, kb_ref, vb_ref, mb_ref, hb_ref, triu, triuf, trilf, 0))
    for d, (q_ref, k_ref, v_ref, misc_ref, out_ref, mask, mask_f, mask_tf, last) in enumerate(dirs):
        g = misc_ref[...] + gbr_ref[...]
        gt = misc_ref[...].T + gbc_ref[...]
        bcol = _dot_hi(mask_f, _log_sigmoid(g))
        brow = _dot_hi(_log_sigmoid(gt[0:16, :]), mask_tf)
        for h in range(ML_HEADS):
            hd = d * ML_HEADS + h
            i_idx = MISC_GATE + 8 * d + h
            f_idx = i_idx + ML_HEADS
            ig_col = g[:, i_idx:i_idx + 1]
            ig_row = gt[i_idx:i_idx + 1, :]
            b_col = bcol[:, f_idx:f_idx + 1]
            b_row = brow[f_idx:f_idx + 1, :]
            b_last = b_col[last:last + 1, :]
            m_prev = m_ref[hd:hd + 1, 0:1]
            q = q_ref[:, h * ML_DH:(h + 1) * ML_DH]
            k = k_ref[:, h * ML_DH:(h + 1) * ML_DH]
            v1 = v_ref[:, h * LANES:(h + 1) * LANES]
            state = st_ref[hd]

            dmat = jnp.where(mask, b_col - b_row + ig_row, -jnp.inf)
            inter = b_col + m_prev
            m_t = jnp.maximum(inter, jnp.max(dmat, axis=-1, keepdims=True))
            w_inter = jnp.exp(inter - m_t)
            s = _dot_nt(q, k) * jnp.exp(dmat - m_t)
            tot = _dot(s.astype(BF16), v1) + w_inter * _dot(q, state.astype(BF16))
            den = tot[:, ML_DH:ML_DH + 1]
            hval = tot[:, 0:ML_DH] / jnp.maximum(jnp.abs(den), jnp.exp(-m_t))
            out_ref[:, h * ML_DH:(h + 1) * ML_DH] = hval

            w_log = b_last - b_col + ig_col
            m_new = jnp.maximum(b_last + m_prev, jnp.max(w_log, axis=0, keepdims=True))
            decay = jnp.exp(b_last + m_prev - m_new)
            kw = (k.astype(F32) * jnp.exp(w_log - m_new)).astype(BF16)
            st_ref[hd] = decay * state + _dot_tn(kw, v1)
            m_ref[hd:hd + 1, :] = jnp.broadcast_to(m_new, (1, LANES))


def _scan_index_maps(n_batch, n_chunks):
    ctx0 = n_batch * n_chunks
    fwd = lambda b, s: (jnp.where(s == 0, ctx0 + b, b * n_chunks + s - 1), 0)
    bwd = lambda b, s: (jnp.where(s == 0, ctx0 + b, b * n_chunks + n_chunks - s), 0)
    return fwd, bwd


def _mlstm(q, k, v1, misc, gb_row, gb_col, n_batch, n_chunks):
    r = q.shape[0]
    n = SCAN_CHUNK
    fwd, bwd = _scan_index_maps(n_batch, n_chunks)
    const = lambda b, s: (0, 0)
    specs = []
    for imap in (fwd, bwd):
        specs += [pl.BlockSpec((n, W_GROUP), imap), pl.BlockSpec((n, W_GROUP), imap),
                  pl.BlockSpec((n, ML_HEADS * LANES), imap), pl.BlockSpec((n, LANES), imap)]
    specs += [pl.BlockSpec((1, LANES), const), pl.BlockSpec((LANES, 1), const)]
    return pl.pallas_call(
        _mlstm_kernel,
        out_shape=[jax.ShapeDtypeStruct((r, W_GROUP), F32)] * 2,
        grid=(n_batch, n_chunks + 1),
        in_specs=specs,
        out_specs=[pl.BlockSpec((n, W_GROUP), fwd), pl.BlockSpec((n, W_GROUP), bwd)],
        scratch_shapes=[pltpu.VMEM((2 * ML_HEADS, ML_DH, LANES), F32),
                        pltpu.VMEM((2 * ML_HEADS, LANES), F32)],
        compiler_params=_cparams(("arbitrary", "arbitrary")),
        name="mlstm_scan",
    )(q, k, v1, misc, q, k, v1, misc, gb_row, gb_col)


def _ssd_prep_kernel(x_ref, xp_ref, xn_ref, w_ref, b_ref, act_ref, xs_ref, *, bounds):
    tm = x_ref.shape[0]
    first, last = _segment_masks(pl.program_id(0) * tm, tm, bounds)
    x = x_ref[...]
    rid = lax.broadcasted_iota(jnp.int32, (tm, 1), 0)
    xp = jnp.where(rid == 0, xp_ref[HALO - 1:HALO, :], pltpu.roll(x, 1, 0))
    xn = jnp.where(rid == tm - 1, xn_ref[0:1, :], pltpu.roll(x, tm - 1, 0))
    xp = jnp.where(first, 0.0, xp)
    xn = jnp.where(last, 0.0, xn)
    y = _silu(w_ref[0:1, :] * xp + w_ref[1:2, :] * x + w_ref[2:3, :] * xn + b_ref[...])
    act_ref[...] = y.astype(act_ref.dtype)
    xs_ref[...] = y[:, 0:W_GROUP]


def _halo_specs(width, n_rows):
    per = ROW_TILE // HALO
    n_blocks = n_rows // HALO
    prev = pl.BlockSpec((HALO, width), lambda i: (jnp.maximum(i * per - 1, 0), 0))
    nxt = pl.BlockSpec((HALO, width), lambda i: (jnp.minimum((i + 1) * per, n_blocks - 1), 0))
    return prev, nxt


def _ssd_prep(xbc, conv_w, conv_b, bounds):
    r, width = xbc.shape
    row = lambda i: (i, 0)
    prev, nxt = _halo_specs(width, r)
    return pl.pallas_call(
        functools.partial(_ssd_prep_kernel, bounds=bounds),
        out_shape=[jax.ShapeDtypeStruct((r, width), BF16), jax.ShapeDtypeStruct((r, W_GROUP), F32)],
        grid=(r // ROW_TILE,),
        in_specs=[pl.BlockSpec((ROW_TILE, width), row), prev, nxt,
                  pl.BlockSpec((8, width), lambda i: (0, 0)),
                  pl.BlockSpec((1, width), lambda i: (0, 0))],
        out_specs=[pl.BlockSpec((ROW_TILE, width), row), pl.BlockSpec((ROW_TILE, W_GROUP), row)],
        compiler_params=_cparams(("arbitrary",)),
        name="ssd_conv",
    )(xbc, xbc, xbc, conv_w, conv_b)


def _ssd_kernel(af_ref, mf_ref, ab_ref, mb_ref, pr_ref, pc_ref, yf_ref, yb_ref, st_ref):
    n = af_ref.shape[0]

    @pl.when(pl.program_id(1) == 0)
    def _():
        st_ref[...] = jnp.zeros_like(st_ref)

    tril, triu = _tri_masks(n)
    trilf = tril.astype(F32)
    triuf = triu.astype(F32)
    dirs = ((af_ref, mf_ref, yf_ref, tril, trilf, triuf, n - 1),
            (ab_ref, mb_ref, yb_ref, triu, triuf, trilf, 0))
    for d, (act_ref, misc_ref, out_ref, mask, mask_f, mask_tf, last) in enumerate(dirs):
        dt_c = _softplus(misc_ref[...] + pr_ref[0:1, :])
        dt_r = _softplus(misc_ref[...].T + pc_ref[:, 0:1])
        cs_c = _dot_hi(mask_f, dt_c * pr_ref[1:2, :])
        cs_r = _dot_hi((dt_r * pc_ref[:, 1:2])[0:32, :], mask_tf)
        for g in range(SSD_GROUPS):
            bm = act_ref[:, W_GROUP + g * SSD_STATE:W_GROUP + (g + 1) * SSD_STATE]
            cm = act_ref[:, W_GROUP + SSD_GROUPS * SSD_STATE + g * SSD_STATE:
                         W_GROUP + SSD_GROUPS * SSD_STATE + (g + 1) * SSD_STATE]
            gmat = _dot_nt(cm, bm)
            for hh in range(SSD_HEADS // SSD_GROUPS):
                h = g * (SSD_HEADS // SSD_GROUPS) + hh
                hd = d * SSD_HEADS + h
                idx = MISC_DT + d * SSD_HEADS + h
                cs_col = cs_c[:, idx:idx + 1]
                cs_row = cs_r[idx:idx + 1, :]
                cs_last = cs_col[last:last + 1, :]
                dt_col = dt_c[:, idx:idx + 1]
                dt_row = dt_r[idx:idx + 1, :]
                xh = act_ref[:, h * SSD_HEADDIM:(h + 1) * SSD_HEADDIM]
                state = st_ref[hd]
                decay = jnp.exp(jnp.where(mask, cs_col - cs_row, -jnp.inf))
                scores = (gmat * decay * dt_row).astype(BF16)
                y = _dot(scores, xh) + jnp.exp(cs_col) * _dot(cm, state.astype(BF16))
                out_ref[:, h * SSD_HEADDIM:(h + 1) * SSD_HEADDIM] = y
                wk = (bm.astype(F32) * (jnp.exp(cs_last - cs_col) * dt_col)).astype(BF16)
                st_ref[hd] = jnp.exp(cs_last) * state + _dot_tn(wk, xh)


def _ssd_scan(act, misc, p_row, p_col, n_batch, n_chunks):
    r, width = act.shape
    n = SCAN_CHUNK
    fwd, bwd = _scan_index_maps(n_batch, n_chunks)
    const = lambda b, s: (0, 0)
    specs = []
    for imap in (fwd, bwd):
        specs += [pl.BlockSpec((n, width), imap), pl.BlockSpec((n, LANES), imap)]
    specs += [pl.BlockSpec((8, LANES), const), pl.BlockSpec((LANES, 8), const)]
    return pl.pallas_call(
        _ssd_kernel,
        out_shape=[jax.ShapeDtypeStruct((r, W_GROUP), F32)] * 2,
        grid=(n_batch, n_chunks + 1),
        in_specs=specs,
        out_specs=[pl.BlockSpec((n, W_GROUP), fwd), pl.BlockSpec((n, W_GROUP), bwd)],
        scratch_shapes=[pltpu.VMEM((2 * SSD_HEADS, SSD_STATE, SSD_HEADDIM), F32)],
        compiler_params=_cparams(("arbitrary", "arbitrary")),
        name="ssd_scan",
    )(act, misc, act, misc, p_row, p_col)


def _mla_prep_kernel(c_ref, misc_ref, cos_ref, sin_ref, qn_ref, kvn_ref, wqa_ref, wqb_ref,
                     wka_ref, wvt_ref, ea_ref, eb_ref, q_ref, k_ref, vt_ref):
    scale = (MLA_NOPE + MLA_ROPE) ** -0.5 * math.log2(math.e)
    cq = _rms(c_ref[:, 0:MLA_Q_LORA], qn_ref[...]).astype(BF16)
    ckv = _rms(c_ref[:, MLA_Q_LORA:MLA_Q_LORA + MLA_KV_LORA], kvn_ref[...]).astype(BF16)
    cos = cos_ref[...]
    sin = sin_ref[...]
    qa = _dot(cq, wqa_ref[...])
    qb = _dot(cq, wqb_ref[...])
    ka = _dot(ckv, wka_ref[...])
    misc = misc_ref[...]
    kr = _dot_hi(misc, ea_ref[...]) * cos + _dot_hi(misc, eb_ref[...]) * sin
    for h in range(MLA_HEADS):
        sl = slice(h * LANES, (h + 1) * LANES)
        q_ref[:, sl] = ((qa[:, sl] * cos + qb[:, sl] * sin) * scale).astype(BF16)
        k_ref[:, sl] = (ka[:, sl] + kr).astype(BF16)
    vt = _dot_nt(wvt_ref[...], ckv)
    row = lax.broadcasted_iota(jnp.int32, (vt.shape[0], 1), 0)
    vt_ref[...] = (vt + jnp.where((row & (LANES - 1)) == MLA_V, 1.0, 0.0)).astype(BF16)


def _mla_prep(mlac, misc, cos, sin, qn, kvn, wqa, wqb, wka, wvt, ea, eb):
    r = mlac.shape[0]
    row = lambda i: (i, 0)
    full = lambda a: pl.BlockSpec(a.shape, lambda i: (0, 0))
    hw = MLA_HEADS * LANES
    return pl.pallas_call(
        _mla_prep_kernel,
        out_shape=[jax.ShapeDtypeStruct((r, hw), BF16)] * 2 + [jax.ShapeDtypeStruct((hw, r), BF16)],
        grid=(r // ROW_TILE,),
        in_specs=[pl.BlockSpec((ROW_TILE, mlac.shape[1]), row), pl.BlockSpec((ROW_TILE, LANES), row),
                  pl.BlockSpec((ROW_TILE, LANES), row), pl.BlockSpec((ROW_TILE, LANES), row),
                  full(qn), full(kvn), full(wqa), full(wqb), full(wka), full(wvt), full(ea), full(eb)],
        out_specs=[pl.BlockSpec((ROW_TILE, hw), row)] * 2 + [pl.BlockSpec((hw, ROW_TILE), lambda i: (0, i))],
        compiler_params=_cparams(("arbitrary",)),
        name="mla_prep",
    )(mlac, misc, cos, sin, qn, kvn, wqa, wqb, wka, wvt, ea, eb)


def _attn_chunk(q, k, vt, m_ref, acc_ref, h):
    st = _dot_nt(k, q)
    m_old = m_ref[h]
    m_new = jnp.maximum(m_old, jnp.max(st, axis=0, keepdims=True))
    p = jnp.exp2(st - m_new).astype(BF16)
    acc_ref[h] = jnp.exp2(m_old - m_new) * acc_ref[h] + _dot(vt, p)
    m_ref[h] = m_new


def _attn_init(m_ref, acc_ref):
    m_ref[...] = jnp.full_like(m_ref, -jnp.inf)
    acc_ref[...] = jnp.zeros_like(acc_ref)


def _attn_finish(o_ref, acc_ref):
    for h in range(MLA_HEADS):
        acc = acc_ref[h]
        o_ref[h * MLA_V:(h + 1) * MLA_V, :] = (acc[0:MLA_V] / acc[MLA_V:MLA_V + 1]).astype(o_ref.dtype)


def _attn_lat_kernel(q_ref, kl_ref, vlt_ref, kc_ref, vct_ref, o_ref, m_ref, acc_ref):
    n_chunks = kl_ref.shape[0] // ATTN_TK
    _attn_init(m_ref, acc_ref)

    def body(c, carry):
        rows = pl.ds(pl.multiple_of(c * ATTN_TK, ATTN_TK), ATTN_TK)
        for h in range(MLA_HEADS):
            sl = slice(h * LANES, (h + 1) * LANES)
            _attn_chunk(q_ref[:, sl], kl_ref[rows, sl], vlt_ref[sl, rows], m_ref, acc_ref, h)
        return carry

    lax.fori_loop(0, n_chunks, body, 0)
    for h in range(MLA_HEADS):
        sl = slice(h * LANES, (h + 1) * LANES)
        _attn_chunk(q_ref[:, sl], kc_ref[:, sl], vct_ref[sl, :], m_ref, acc_ref, h)
    _attn_finish(o_ref, acc_ref)


def _attn_ctx_kernel(q_ref, kc_ref, vct_ref, prev_ref, o_ref, m_ref, acc_ref):
    del prev_ref
    _attn_init(m_ref, acc_ref)
    for h in range(MLA_HEADS):
        sl = slice(h * LANES, (h + 1) * LANES)
        _attn_chunk(q_ref[:, sl], kc_ref[:, sl], vct_ref[sl, :], m_ref, acc_ref, h)
    _attn_finish(o_ref, acc_ref)


def _attention(q, k, vt, n_batch, t_lat, t_ctx):
    r, hw = q.shape
    nq = t_lat // ATTN_TQ
    ctx_blk = n_batch * t_lat // t_ctx
    scratch = lambda tq: [pltpu.VMEM((MLA_HEADS, 1, tq), F32), pltpu.VMEM((MLA_HEADS, LANES, tq), F32)]
    lat = pl.pallas_call(
        _attn_lat_kernel,
        out_shape=jax.ShapeDtypeStruct((W_GROUP, r), BF16),
        grid=(n_batch, nq),
        in_specs=[pl.BlockSpec((ATTN_TQ, hw), lambda b, i: (b * nq + i, 0)),
                  pl.BlockSpec((t_lat, hw), lambda b, i: (b, 0)),
                  pl.BlockSpec((hw, t_lat), lambda b, i: (0, b)),
                  pl.BlockSpec((t_ctx, hw), lambda b, i: (ctx_blk + b, 0)),
                  pl.BlockSpec((hw, t_ctx), lambda b, i: (0, ctx_blk + b))],
        out_specs=pl.BlockSpec((W_GROUP, ATTN_TQ), lambda b, i: (0, b * nq + i)),
        scratch_shapes=scratch(ATTN_TQ),
        compiler_params=_cparams(("arbitrary", "arbitrary")),
        name="mla_attention_latent",
    )(q, k, vt, k, vt)
    return pl.pallas_call(
        _attn_ctx_kernel,
        out_shape=jax.ShapeDtypeStruct((W_GROUP, r), BF16),
        grid=(n_batch,),
        in_specs=[pl.BlockSpec((t_ctx, hw), lambda b: (ctx_blk + b, 0)),
                  pl.BlockSpec((t_ctx, hw), lambda b: (ctx_blk + b, 0)),
                  pl.BlockSpec((hw, t_ctx), lambda b: (0, ctx_blk + b)),
                  pl.BlockSpec(memory_space=pl.ANY)],
        out_specs=pl.BlockSpec((W_GROUP, t_ctx), lambda b: (0, ctx_blk + b)),
        scratch_shapes=scratch(t_ctx),
        input_output_aliases={3: 0},
        compiler_params=_cparams(("arbitrary",)),
        name="mla_attention_context",
    )(q, k, vt, lat)


def _cis_pow(lr, li, dt, k):
    mag = jnp.exp(lr * dt * k)
    ang = li * dt * k
    return mag * jnp.cos(ang), mag * jnp.sin(ang)


def _s5_tables_kernel(pr_ref, pc_ref, btre_ref, btim_ref, ctre_ref, ctim_ref,
                      toep_ref, wz_ref, wy_ref, decay_ref):
    lc, ng, ns = S5_CHUNK, S5_GROUP, S5_STATE
    lr_r = jnp.minimum(pr_ref[0:1, :], -1e-4)
    li_r = pr_ref[1:2, :]
    dt_r = jnp.exp(pr_ref[2:3, :])
    lr_c = jnp.minimum(pc_ref[:, 0:1], -1e-4)
    li_c = pc_ref[:, 1:2]
    dt_c = jnp.exp(pc_ref[:, 2:3])
    lane4 = lax.broadcasted_iota(jnp.int32, (1, S5_SW), 1)
    odd_lane = (lane4 & ns) != 0
    row4 = lax.broadcasted_iota(jnp.int32, (S5_SW, 1), 0)
    odd_row = (row4 & ns) != 0

    ab_re, ab_im = _cis_pow(lr_r, li_r, dt_r, 1.0)
    den = lr_r * lr_r + li_r * li_r
    f_re = ((ab_re - 1.0) * lr_r + ab_im * li_r) / den
    f_im = (ab_im * lr_r - (ab_re - 1.0) * li_r) / den
    bb_re = f_re * btre_ref[...] - f_im * btim_ref[...]
    bb_im = f_re * btim_ref[...] + f_im * btre_ref[...]

    s_col = lax.broadcasted_iota(jnp.int32, (lc, 1), 0)
    expo = jnp.where(lane4 < 2 * ns, lc - 1 - s_col, s_col).astype(F32)
    pz_re, pz_im = _cis_pow(lr_r, li_r, dt_r, expo)
    xz = jnp.where(odd_lane, bb_im, bb_re)
    yz = jnp.where(odd_lane, bb_re, -bb_im)
    for j in range(ng):
        wz_ref[j] = (pz_re * xz[j:j + 1, :] + pz_im * yz[j:j + 1, :]).astype(BF16)

    al_re, al_im = _cis_pow(lr_r, li_r, dt_r, float(lc))
    al_sw = jnp.where(odd_lane, al_im, -al_im)
    decay_ref[...] = jnp.zeros_like(decay_ref)
    for d in range(2):
        decay_ref[d:d + 1, :] = al_re[:, d * LANES:(d + 1) * LANES]
        decay_ref[2 + d:3 + d, :] = al_sw[:, d * LANES:(d + 1) * LANES]

    t_lane = lax.broadcasted_iota(jnp.int32, (1, LANES), 1) & (lc - 1)
    expo = jnp.where(row4 < 2 * ns, t_lane + 1, lc - t_lane).astype(F32)
    py_re, py_im = _cis_pow(lr_c, li_c, dt_c, expo)
    reps = lc * ng // LANES
    py_re = jnp.tile(py_re, (1, reps))
    py_im = jnp.tile(py_im, (1, reps))
    lane_j = lax.shift_right_logical(lax.broadcasted_iota(jnp.int32, (ng, lc * ng), 1), lc.bit_length() - 1)
    sel = (lane_j == lax.broadcasted_iota(jnp.int32, (ng, lc * ng), 0)).astype(F32)
    ce_re = _dot_hi(ctre_ref[...], sel)
    ce_im = _dot_hi(ctim_ref[...], sel)
    wy_ref[...] = jnp.where(odd_row, -(ce_re * py_im + ce_im * py_re),
                            ce_re * py_re - ce_im * py_im).astype(BF16)

    m_lane = lax.broadcasted_iota(jnp.int32, (1, 2 * lc), 1)
    lane_j2 = lax.shift_right_logical(lax.broadcasted_iota(jnp.int32, (ng, 2 * lc * ng), 1),
                                      (2 * lc).bit_length() - 1)
    sel2 = (lane_j2 == lax.broadcasted_iota(jnp.int32, (ng, 2 * lc * ng), 0)).astype(F32)
    xmat = None
    for d in range(2):
        rows = slice(d * 2 * ns, d * 2 * ns + ns)
        lag = m_lane - (lc - 1) if d == 0 else (lc - 1) - m_lane
        valid = (lag >= 0) & (m_lane < 2 * lc - 1)
        pm_re, pm_im = _cis_pow(lr_c[rows], li_c[rows], dt_c[rows], jnp.where(valid, lag, 0).astype(F32))
        pm_re = jnp.tile(jnp.where(valid, pm_re, 0.0), (1, ng))
        pm_im = jnp.tile(jnp.where(valid, pm_im, 0.0), (1, ng))
        c2_re = _dot_hi(ctre_ref[rows, :], sel2)
        c2_im = _dot_hi(ctim_ref[rows, :], sel2)
        p_re = c2_re * pm_re - c2_im * pm_im
        p_im = c2_re * pm_im + c2_im * pm_re
        lanes = slice(d * 2 * ns, d * 2 * ns + ns)
        term = _dot_hi(bb_re[:, lanes], p_re) - _dot_hi(bb_im[:, lanes], p_im)
        xmat = term if xmat is None else xmat + term

    low_half = lax.broadcasted_iota(jnp.int32, (1, 2 * lc), 1) < lc
    for j in range(ng):
        xb = jnp.broadcast_to(xmat[j:j + 1, :], (lc, 2 * lc * ng))
        cols = []
        for p in range(ng // 2):
            even = xb[:, (2 * p) * 2 * lc:(2 * p + 1) * 2 * lc]
            odd = xb[:, (2 * p + 1) * 2 * lc:(2 * p + 2) * 2 * lc]
            cols.append(jnp.where(low_half,
                                  pltpu.roll(even, lc + 1, 1, stride=1, stride_axis=0),
                                  pltpu.roll(odd, 1, 1, stride=1, stride_axis=0)))
        toep_ref[j] = jnp.concatenate(cols, axis=1).astype(BF16)


def _s5_tables(a_re, a_im, log_dt, b_re, b_im, c_re, c_im):
    depth = a_re.shape[0]
    g, n, j, lc = S5_NGROUPS, S5_STATE, S5_GROUP, S5_CHUNK
    n_all = depth * g

    def parts(v):
        v = jnp.transpose(v, (0, 2, 1, 3))
        return jnp.concatenate([v[:, :, 0], v[:, :, 0], v[:, :, 1], v[:, :, 1]], axis=-1).reshape(n_all, S5_SW)

    rows = jnp.stack([parts(a_re), parts(a_im), parts(jnp.broadcast_to(log_dt[..., None], a_re.shape))], axis=1)
    p_row = jnp.pad(rows, ((0, 0), (0, 5), (0, 0)))
    p_col = jnp.swapaxes(p_row, 1, 2)
    bt = lambda b: jnp.tile(jnp.swapaxes(b.reshape(n_all, n, j), 1, 2), (1, 1, 4))
    ct = lambda c: jnp.tile(jnp.swapaxes(c.reshape(n_all, j, n), 1, 2), (1, 4, 1))
    own = lambda i: (i, 0, 0)
    width = lc * j
    return pl.pallas_call(
        _s5_tables_kernel,
        out_shape=[jax.ShapeDtypeStruct((n_all * j, lc, width), BF16),
                   jax.ShapeDtypeStruct((n_all * j, lc, S5_SW), BF16),
                   jax.ShapeDtypeStruct((n_all, S5_SW, width), BF16),
                   jax.ShapeDtypeStruct((n_all, 8, LANES), F32)],
        grid=(n_all,),
        in_specs=[pl.BlockSpec((None, 8, S5_SW), own), pl.BlockSpec((None, S5_SW, 8), own),
                  pl.BlockSpec((None, j, S5_SW), own), pl.BlockSpec((None, j, S5_SW), own),
                  pl.BlockSpec((None, S5_SW, j), own), pl.BlockSpec((None, S5_SW, j), own)],
        out_specs=[pl.BlockSpec((j, lc, width), own), pl.BlockSpec((j, lc, S5_SW), own),
                   pl.BlockSpec((None, S5_SW, width), own), pl.BlockSpec((None, 8, LANES), own)],
        compiler_params=_cparams(("arbitrary",)),
        name="s5_tables",
    )(p_row, p_col, bt(b_re), bt(b_im), ct(c_re), ct(c_im))


def _s5_local_kernel(u_ref, wz_ref, z_ref):
    acc = None
    for j in range(S5_GROUP):
        term = _dot(u_ref[j].astype(BF16), wz_ref[j])
        acc = term if acc is None else acc + term
    z_ref[...] = acc


def _s5_scan_kernel(z_ref, ca_ref, cb_ref, x_ref, xf_ref, xb_ref, *, n_batch, n_lat, n_ctx):
    ca = ca_ref[...]
    cb = cb_ref[...]
    zero = jnp.zeros(ca.shape, F32)
    ctx0 = n_batch * n_lat

    def step(x, z):
        return x * ca + pltpu.roll(x, LANES // 2, 1) * cb + z

    def body(i, carry):
        new = []
        for b in range(n_batch):
            rf = jnp.where(i < n_ctx, ctx0 + b * n_ctx + i, b * n_lat + i - n_ctx)
            rb = jnp.where(i < n_ctx, ctx0 + b * n_ctx + n_ctx - 1 - i, b * n_lat + n_lat - 1 - (i - n_ctx))
            xf, xb = carry[2 * b], carry[2 * b + 1]
            xf_ref[rf] = xf
            xb_ref[rb] = xb
            new += [step(xf, z_ref[rf]), step(xb, z_ref[rb])]
        return tuple(new)

    lax.fori_loop(0, n_ctx + n_lat, body, (zero,) * (2 * n_batch))
    row = lax.broadcasted_iota(jnp.int32, x_ref.shape, 1)
    x_ref[...] = jnp.where((row & 1) == 0, xf_ref[...], xb_ref[...])


def _s5_out_kernel(u_ref, t_ref, x_ref, wy_ref, y_ref):
    acc = _dot(x_ref[...].astype(BF16), wy_ref[...])
    for j in range(S5_GROUP):
        acc = acc + _dot(u_ref[j].astype(BF16), t_ref[j])
    for j in range(S5_GROUP):
        y_ref[j] = acc[:, j * S5_CHUNK:(j + 1) * S5_CHUNK]


def _s5_mix(ut, toep, wz, wy, ca, cb, n_batch, n_lat, n_ctx):
    g, j, lc = S5_NGROUPS, S5_GROUP, S5_CHUNK
    nr = ut.shape[1] // lc
    u3 = ut.reshape(W_GROUP, nr, lc)
    grp = lambda i: (i, 0, 0)
    z = pl.pallas_call(
        _s5_local_kernel,
        out_shape=jax.ShapeDtypeStruct((nr, g * S5_SW), F32),
        grid=(g,),
        in_specs=[pl.BlockSpec((j, nr, lc), grp), pl.BlockSpec((j, lc, S5_SW), grp)],
        out_specs=pl.BlockSpec((nr, S5_SW), lambda i: (0, i)),
        compiler_params=_cparams(("arbitrary",)),
        name="s5_local_state",
    )(u3, wz)
    z3 = z.reshape(nr, 2 * g, LANES)
    whole = lambda shp: pl.BlockSpec(shp, lambda: tuple(0 for _ in shp))
    x3 = pl.pallas_call(
        functools.partial(_s5_scan_kernel, n_batch=n_batch, n_lat=n_lat, n_ctx=n_ctx),
        out_shape=jax.ShapeDtypeStruct(z3.shape, F32),
        in_specs=[whole(z3.shape), whole(ca.shape), whole(cb.shape)],
        out_specs=whole(z3.shape),
        scratch_shapes=[pltpu.VMEM(z3.shape, F32), pltpu.VMEM(z3.shape, F32)],
        compiler_params=pltpu.CompilerParams(vmem_limit_bytes=VMEM_LIMIT),
        name="s5_chunk_scan",
    )(z3, ca, cb)
    x = x3.reshape(nr, g * S5_SW)
    y3 = pl.pallas_call(
        _s5_out_kernel,
        out_shape=jax.ShapeDtypeStruct((W_GROUP, nr, lc), F32),
        grid=(g,),
        in_specs=[pl.BlockSpec((j, nr, lc), grp), pl.BlockSpec((j, lc, lc * j), grp),
                  pl.BlockSpec((nr, S5_SW), lambda i: (0, i)),
                  pl.BlockSpec((None, S5_SW, lc * j), grp)],
        out_specs=pl.BlockSpec((j, nr, lc), grp),
        compiler_params=_cparams(("arbitrary",)),
        name="s5_output",
    )(u3, toep, x, wy)
    return y3.reshape(W_GROUP, nr * lc)


def _outproj_kernel(x_ref, mod_ref, hf_ref, hb_ref, og_ref, yf_ref, yb_ref, xs_ref, z_ref,
                    att_ref, s5_ref, ut_ref, mln_ref, sd_ref, sn_ref, s5d_ref, glu_ref, wo_ref, n2_ref,
                    xo_ref, h2_ref):
    h = hf_ref[...] + hb_ref[...]
    parts = []
    for i in range(ML_HEADS):
        hh = h[:, i * ML_DH:(i + 1) * ML_DH]
        parts.append(hh * lax.rsqrt(jnp.mean(hh * hh, axis=-1, keepdims=True) + EPS))
    ya = jnp.concatenate(parts, axis=-1) * mln_ref[...] * _sigmoid(og_ref[...])
    yc = (sd_ref[...] * xs_ref[...] + yf_ref[...] + yb_ref[...]) * _silu(z_ref[...])
    yc = _rms(yc, sn_ref[...])
    yd = _gelu_tanh(s5d_ref[...] * ut_ref[...] + s5_ref[...])
    yd = yd * _sigmoid(_dot(glu_ref[...], yd.astype(BF16)))
    y = (_dot(ya.astype(BF16), wo_ref[0:W_GROUP, :])
         + _dot_tn(att_ref[...], wo_ref[W_GROUP:2 * W_GROUP, :])
         + _dot(yc.astype(BF16), wo_ref[2 * W_GROUP:3 * W_GROUP, :])
         + _dot_tn(yd.astype(BF16), wo_ref[3 * W_GROUP:4 * W_GROUP, :]))
    x = x_ref[...] + mod_ref[2:3, :] * y
    xo_ref[...] = x
    h2 = _rms(x, n2_ref[...]) * (1.0 + mod_ref[4:5, :]) + mod_ref[3:4, :]
    h2_ref[...] = h2.astype(BF16)


def _out_projection(x, mod, seg_of_tile, row_acts, col_acts, params):
    r, d = x.shape
    row = lambda i: (i, 0)
    full = lambda a: pl.BlockSpec(a.shape, lambda i: (0, 0))
    return pl.pallas_call(
        _outproj_kernel,
        out_shape=[jax.ShapeDtypeStruct((r, d), F32), jax.ShapeDtypeStruct((r, d), BF16)],
        grid=(r // ROW_TILE,),
        in_specs=[pl.BlockSpec((ROW_TILE, d), row),
                  pl.BlockSpec((None, 8, d), lambda i: (seg_of_tile(i), 0, 0))]
        + [pl.BlockSpec((ROW_TILE, W_GROUP), row) for _ in row_acts]
        + [pl.BlockSpec((W_GROUP, ROW_TILE), lambda i: (0, i)) for _ in col_acts]
        + [full(p) for p in params],
        out_specs=[pl.BlockSpec((ROW_TILE, d), row)] * 2,
        compiler_params=_cparams(("arbitrary",)),
        name="out_projection",
    )(x, mod, *row_acts, *col_acts, *params)


def _ffn_kernel(x_ref, mod_ref, h_ref, hp_ref, hn_ref, wup_ref, cw_ref, wdn_ref, o_ref, acc_ref, *, bounds):
    tm = h_ref.shape[0]
    first, last = _segment_masks(pl.program_id(0) * tm, tm, bounds)
    h = h_ref[...]
    h_ext = jnp.concatenate([hp_ref[...], h, hn_ref[...]], axis=0)
    n_ext = tm + 2 * HALO
    acc_ref[...] = jnp.zeros_like(acc_ref)

    def body(j, carry):
        c0 = pl.multiple_of(j * FF_CHUNK, FF_CHUNK)
        u = _dot(h, wup_ref[:, pl.ds(c0, FF_CHUNK)])
        g_ext = _dot(h_ext, wup_ref[:, pl.ds(pl.multiple_of(D_FF + c0, FF_CHUNK), FF_CHUNK)])
        gp = jnp.where(first, 0.0, pltpu.roll(g_ext, 1, 0)[HALO:HALO + tm])
        gn = jnp.where(last, 0.0, pltpu.roll(g_ext, n_ext - 1, 0)[HALO:HALO + tm])
        gc = g_ext[HALO:HALO + tm]
        cw = cw_ref[:, pl.ds(c0, FF_CHUNK)]
        conv = cw[0:1, :] * gp + cw[1:2, :] * gc + cw[2:3, :] * gn
        act = (_silu(conv) * u).astype(BF16)
        acc_ref[...] += _dot(act, wdn_ref[pl.ds(c0, FF_CHUNK), :])
        return carry

    lax.fori_loop(0, D_FF // FF_CHUNK, body, 0)
    o_ref[...] = x_ref[...] + mod_ref[5:6, :] * acc_ref[...]


def _conv_ffn(x, mod, seg_of_tile, h2, w_up, conv_w, w_down, bounds):
    r, d = x.shape
    row = lambda i: (i, 0)
    prev, nxt = _halo_specs(d, r)
    full = lambda a: pl.BlockSpec(a.shape, lambda i: (0, 0))
    return pl.pallas_call(
        functools.partial(_ffn_kernel, bounds=bounds),
        out_shape=jax.ShapeDtypeStruct((r, d), F32),
        grid=(r // ROW_TILE,),
        in_specs=[pl.BlockSpec((ROW_TILE, d), row),
                  pl.BlockSpec((None, 8, d), lambda i: (seg_of_tile(i), 0, 0)),
                  pl.BlockSpec((ROW_TILE, d), row), prev, nxt,
                  full(w_up), full(conv_w), full(w_down)],
        out_specs=pl.BlockSpec((ROW_TILE, d), row),
        scratch_shapes=[pltpu.VMEM((ROW_TILE, d), F32)],
        compiler_params=_cparams(("arbitrary",)),
        name="conv_ffn",
    )(x, mod, h2, h2, h2, w_up, conv_w, w_down)


def _final_norm_kernel(x_ref, g_ref, o_ref):
    o_ref[...] = _rms(x_ref[...], g_ref[...])


def _final_norm(x, g, n_rows):
    d = x.shape[1]
    return pl.pallas_call(
        _final_norm_kernel,
        out_shape=jax.ShapeDtypeStruct((n_rows, d), F32),
        grid=(n_rows // ROW_TILE,),
        in_specs=[pl.BlockSpec((ROW_TILE, d), lambda i: (i, 0)), pl.BlockSpec((1, d), lambda i: (0, 0))],
        out_specs=pl.BlockSpec((ROW_TILE, d), lambda i: (i, 0)),
        compiler_params=_cparams(("arbitrary",)),
        name="final_norm",
    )(x, g)


def _rope_tables(t_lat, n_ctx_rows):
    rows = t_lat // GRID_W
    row = jnp.broadcast_to(jnp.arange(rows)[:, None], (rows, GRID_W)).reshape(-1).astype(F32)
    col = jnp.broadcast_to(jnp.arange(GRID_W)[None, :], (rows, GRID_W)).reshape(-1).astype(F32)
    n_freq = MLA_ROPE // 4
    inv = ROPE_BASE ** (-jnp.arange(n_freq, dtype=F32) / n_freq)
    ang = jnp.concatenate([row[:, None] * inv, col[:, None] * inv], axis=-1)
    half = MLA_ROPE // 2
    pad = LANES - MLA_NOPE - MLA_ROPE

    def table(t, lead):
        return jnp.concatenate([jnp.full((t_lat, MLA_NOPE), lead, F32), t, t, jnp.zeros((t_lat, pad), F32)], axis=-1)

    cos = table(jnp.cos(ang), 1.0)
    sin = table(jnp.sin(ang), 0.0)
    lane = np.arange(LANES)
    cos_ctx = jnp.broadcast_to(jnp.asarray((lane < MLA_NOPE + 2 * half).astype(np.float32)), (n_ctx_rows, LANES))
    sin_ctx = jnp.zeros((n_ctx_rows, LANES), F32)
    return cos, sin, cos_ctx, sin_ctx


def _mla_weights(w_uq, w_ukv):
    half = MLA_ROPE // 2
    qd = MLA_NOPE + MLA_ROPE
    kd = MLA_NOPE + MLA_V
    hw = MLA_HEADS * LANES
    qa_idx = np.zeros((hw,), np.int32); qa_s = np.zeros((hw,), np.float32)
    qb_idx = np.zeros((hw,), np.int32); qb_s = np.zeros((hw,), np.float32)
    ka_idx = np.zeros((hw,), np.int32); ka_s = np.zeros((hw,), np.float32)
    v_idx = np.zeros((hw,), np.int32); v_s = np.zeros((hw,), np.float32)
    for h in range(MLA_HEADS):
        for i in range(qd):
            qa_idx[h * LANES + i] = h * qd + i
            qa_s[h * LANES + i] = 1.0
        for i in range(half):
            qb_idx[h * LANES + MLA_NOPE + i] = h * qd + MLA_NOPE + half + i
            qb_s[h * LANES + MLA_NOPE + i] = -1.0
            qb_idx[h * LANES + MLA_NOPE + half + i] = h * qd + MLA_NOPE + i
            qb_s[h * LANES + MLA_NOPE + half + i] = 1.0
        for i in range(MLA_NOPE):
            ka_idx[h * LANES + i] = h * kd + i
            ka_s[h * LANES + i] = 1.0
        for i in range(MLA_V):
            v_idx[h * LANES + i] = h * kd + MLA_NOPE + i
            v_s[h * LANES + i] = 1.0
    pick = lambda w, idx, s: (w[..., idx] * s).astype(BF16)
    ea = np.zeros((LANES, LANES), np.float32)
    eb = np.zeros((LANES, LANES), np.float32)
    for i in range(MLA_ROPE):
        ea[MISC_KR + i, MLA_NOPE + i] = 1.0
    for i in range(half):
        eb[MISC_KR + half + i, MLA_NOPE + i] = -1.0
        eb[MISC_KR + i, MLA_NOPE + half + i] = 1.0
    return (pick(w_uq, qa_idx, qa_s), pick(w_uq, qb_idx, qb_s), pick(w_ukv, ka_idx, ka_s),
            pick(w_ukv, v_idx, v_s).T, jnp.asarray(ea), jnp.asarray(eb))


def _lane_rows(vals, offset, n_rows=8):
    k, n = vals.shape
    out = jnp.zeros((n_rows, LANES), F32)
    return out.at[:k, offset:offset + n].set(vals)


def kernel(x, c, ctx, c_ctx, w_mod, b_mod, norm1, norm2, w_in, ml_gate_bias, ml_norm, mla_q_norm, mla_kv_norm, mla_w_uq, mla_w_ukv, ssd_conv_w, ssd_conv_b, ssd_a_log, ssd_dt_bias, ssd_d, ssd_norm, s5_a_re, s5_a_im, s5_log_dt, s5_b_re, s5_b_im, s5_c_re, s5_c_im, s5_d, s5_w_glu, w_out, ffn_w_up, ffn_conv_w, ffn_w_down, final_norm):
    n_batch, t_lat, d = x.shape
    t_ctx = ctx.shape[1]
    depth = w_mod.shape[0]
    assert d == D_MODEL and t_ctx == SCAN_CHUNK and t_lat % ROW_TILE == 0
    assert (n_batch * t_ctx) % ROW_TILE == 0 and t_lat % ATTN_TK == 0 and t_lat % GRID_W == 0
    n_lat = n_batch * t_lat
    n_scan = t_lat // SCAN_CHUNK
    bounds = tuple(b * t_lat for b in range(n_batch)) + tuple(n_lat + b * t_ctx for b in range(n_batch + 1))
    seg_of_tile = lambda i: jnp.minimum(i * ROW_TILE // t_lat, n_batch)

    c8 = jnp.zeros((8, d), F32).at[:n_batch].set(c).at[n_batch].set(c_ctx)
    mod_all = _modulation(c8, w_mod, b_mod)
    mod_all = mod_all.reshape(depth, 8, 6, d)[:, :n_batch + 1]
    mod_all = jnp.pad(mod_all, ((0, 0), (0, 0), (0, 2), (0, 0)))

    perm, pscale = _inproj_perm()
    w_in_p = (w_in[:, :, perm] * pscale).astype(BF16)
    w_in_ut = jnp.swapaxes(w_in[:, :, _IN_S5:_IN_S5 + W_GROUP], 1, 2).astype(BF16)
    gate_row = jax.vmap(lambda v: _lane_rows(v.reshape(1, -1), MISC_GATE, 1))(ml_gate_bias)
    gate_col = jnp.swapaxes(gate_row, 1, 2)
    ssd_a = -jnp.exp(ssd_a_log)
    ssd_row = jax.vmap(lambda bvec, avec: _lane_rows(jnp.stack([bvec.reshape(-1), avec.reshape(-1)]), MISC_DT))(
        ssd_dt_bias, ssd_a)
    ssd_col = jnp.swapaxes(ssd_row, 1, 2)
    ssd_cw = jnp.pad(ssd_conv_w, ((0, 0), (0, 8 - ssd_conv_w.shape[1]), (0, 0)))
    ssd_d_row = jnp.repeat(ssd_d, SSD_HEADDIM, axis=-1)[:, None, :]
    ffn_cw = jnp.pad(ffn_conv_w, ((0, 0), (0, 8 - ffn_conv_w.shape[1]), (0, 0)))
    cos_l, sin_l, cos_c, sin_c = _rope_tables(t_lat, n_batch * t_ctx)
    cos = jnp.concatenate([cos_l] * n_batch + [cos_c], axis=0)
    sin = jnp.concatenate([sin_l] * n_batch + [sin_c], axis=0)
    toep, wz, wy, s5_decay = _s5_tables(s5_a_re, s5_a_im, s5_log_dt, s5_b_re, s5_b_im, s5_c_re, s5_c_im)
    gj = S5_NGROUPS * S5_GROUP
    toep = toep.reshape(depth, gj, S5_CHUNK, S5_CHUNK * S5_GROUP)
    wz = wz.reshape(depth, gj, S5_CHUNK, S5_SW)
    wy = wy.reshape(depth, S5_NGROUPS, S5_SW, S5_CHUNK * S5_GROUP)
    s5_decay = s5_decay.reshape(depth, S5_NGROUPS, 8, LANES)
    s5_ca = s5_decay[:, :, 0:2].reshape(depth, 2 * S5_NGROUPS, LANES)
    s5_cb = s5_decay[:, :, 2:4].reshape(depth, 2 * S5_NGROUPS, LANES)
    n_s5_ctx = t_ctx // S5_CHUNK
    n_s5_lat = t_lat // S5_CHUNK

    xf = jnp.concatenate([x.reshape(n_lat, d), ctx.reshape(n_batch * t_ctx, d)], axis=0)
    for l in range(depth):
        mod = mod_all[l]
        q, k, v1, og, misc, mlac, z, xbc, ut = _in_projection(xf, mod, norm1[l][None], w_in_p[l], w_in_ut[l],
                                                               seg_of_tile)
        hf, hb = _mlstm(q, k, v1, misc, gate_row[l], gate_col[l], n_batch, n_scan)
        wqa, wqb, wka, wvt, ea, eb = _mla_weights(mla_w_uq[l], mla_w_ukv[l])
        qa, ka, vta = _mla_prep(mlac, misc, cos, sin, mla_q_norm[l][None], mla_kv_norm[l][None],
                                wqa, wqb, wka, wvt, ea, eb)
        att = _attention(qa, ka, vta, n_batch, t_lat, t_ctx)
        act, xs = _ssd_prep(xbc, ssd_cw[l], ssd_conv_b[l][None], bounds)
        yf, yb = _ssd_scan(act, misc, ssd_row[l], ssd_col[l], n_batch, n_scan)
        y5 = _s5_mix(ut, toep[l], wz[l], wy[l], s5_ca[l], s5_cb[l], n_batch, n_s5_lat, n_s5_ctx)
        row_acts = (hf, hb, og, yf, yb, xs, z)
        col_acts = (att, y5, ut)
        params = (ml_norm[l][None], ssd_d_row[l], ssd_norm[l][None], s5_d[l][:, None],
                  s5_w_glu[l].T.astype(BF16), w_out[l].astype(BF16), norm2[l][None])
        xf, h2 = _out_projection(xf, mod, seg_of_tile, row_acts, col_acts, params)
        xf = _conv_ffn(xf, mod, seg_of_tile, h2, ffn_w_up[l].astype(BF16), ffn_cw[l],
                       ffn_w_down[l].astype(BF16), bounds)
    out = _final_norm(xf, final_norm[None], n_lat)
    return out.reshape(n_batch, t_lat, d)
```

```python
import functools
import math

import numpy as np
import jax
import jax.numpy as jnp
from jax import lax
from jax.experimental import pallas as pl
from jax.experimental.pallas import tpu as pltpu

F32 = jnp.float32
BF16 = jnp.bfloat16
HI = lax.Precision.HIGHEST

D_MODEL = 1024
W_GROUP = 256
EPS = 1e-6
GRID_W = 64
ROPE_BASE = 10000.0

ML_HEADS = 4
ML_DH = 64

MLA_HEADS = 4
MLA_NOPE = 64
MLA_ROPE = 32
MLA_V = 64
MLA_Q_LORA = 256
MLA_KV_LORA = 128

SSD_HEADS = 4
SSD_HEADDIM = 64
SSD_GROUPS = 2
SSD_STATE = 128

S5_GROUP = 16
S5_NGROUPS = 16
S5_STATE = 64
S5_CHUNK = 64
S5_SW = 4 * S5_STATE

D_FF = 2816
FF_CHUNK = 256

ROW_TILE = 512
SCAN_CHUNK = 256
ATTN_TQ = 512
ATTN_TK = 512
HALO = 16
LANES = 128
VMEM_LIMIT = 56 * 1024 * 1024

MISC_GATE = 0
MISC_DT = 16
MISC_KR = 24


def _cparams(sem):
    return pltpu.CompilerParams(dimension_semantics=sem, vmem_limit_bytes=VMEM_LIMIT)


def _dot(a, b):
    return jnp.dot(a, b, preferred_element_type=F32)


def _dot_nt(a, b):
    return lax.dot_general(a, b, (((1,), (1,)), ((), ())), preferred_element_type=F32)


def _dot_tn(a, b):
    return lax.dot_general(a, b, (((0,), (0,)), ((), ())), preferred_element_type=F32)


def _dot_hi(a, b):
    return jnp.dot(a, b, preferred_element_type=F32, precision=HI)


def _sigmoid(x):
    return 1.0 / (1.0 + jnp.exp(-x))


def _silu(x):
    return x * _sigmoid(x)


def _log_sigmoid(x):
    return jnp.minimum(x, 0.0) - jnp.log(1.0 + jnp.exp(-jnp.abs(x)))


def _softplus(x):
    return jnp.maximum(x, 0.0) + jnp.log(1.0 + jnp.exp(-jnp.abs(x)))


def _gelu_tanh(x):
    return 0.5 * x * (1.0 + jnp.tanh(math.sqrt(2.0 / math.pi) * (x + 0.044715 * x * x * x)))


def _rms(x, g):
    return x * lax.rsqrt(jnp.mean(x * x, axis=-1, keepdims=True) + EPS) * g


def _segment_masks(row0, n_rows, bounds):
    r = row0 + lax.broadcasted_iota(jnp.int32, (n_rows, 1), 0)
    first = r == bounds[0]
    last = r == bounds[1] - 1
    for s in bounds[1:-1]:
        first = first | (r == s)
    for e in bounds[2:]:
        last = last | (r == e - 1)
    return first, last


def _mod_kernel(c_ref, w_ref, b_ref, o_ref):
    c = c_ref[...]
    s = _silu(c).astype(BF16)
    o_ref[...] = _dot(s, w_ref[...].astype(BF16)) + b_ref[...]


def _modulation(c8, w_mod, b_mod):
    depth, d, d6 = w_mod.shape
    return pl.pallas_call(
        _mod_kernel,
        out_shape=jax.ShapeDtypeStruct((depth, 8, d6), F32),
        grid=(depth, d6 // d),
        in_specs=[
            pl.BlockSpec((8, d), lambda l, j: (0, 0)),
            pl.BlockSpec((None, d, d), lambda l, j: (l, 0, j)),
            pl.BlockSpec((None, 1, d), lambda l, j: (l, 0, j)),
        ],
        out_specs=pl.BlockSpec((None, 8, d), lambda l, j: (l, 0, j)),
        compiler_params=_cparams(("arbitrary", "arbitrary")),
        name="modulation",
    )(c8, w_mod, b_mod.reshape(depth, 1, d6))


_IN_SPLITS = (("q", 256), ("k", 256), ("v1", 512), ("og", 256), ("misc", 128),
              ("mlac", 384), ("z", 256), ("xbc", 768))
_IN_TOTAL = sum(w for _, w in _IN_SPLITS)
_IN_S5 = 2488


def _inproj_weight(w_in):
    ml, mla, ssd = 0, 1040, 1456
    cols = lambda a, b: w_in[..., a:b]
    zeros = lambda n: jnp.zeros(w_in.shape[:-1] + (n,), w_in.dtype)
    pieces = [cols(ml, ml + 256), cols(ml + 256, ml + 512) * ML_DH ** -0.5]
    for h in range(ML_HEADS):
        pieces += [cols(ml + 512 + h * 64, ml + 512 + (h + 1) * 64), zeros(64)]
    pieces += [cols(ml + 768, ml + 1024)]
    pieces += [cols(ml + 1024, ml + 1040), cols(ssd + 1024, ssd + 1032),
               cols(mla + 384, mla + 416), zeros(LANES - 56)]
    pieces += [cols(mla, mla + 384), cols(ssd, ssd + 256), cols(ssd + 256, ssd + 1024)]
    return jnp.concatenate(pieces, axis=-1).astype(BF16)


def _inproj_kernel(x_ref, mod_ref, g_ref, w_ref, wut_ref, q_ref, k_ref, v1_ref, og_ref, misc_ref,
                   mlac_ref, z_ref, xbc_ref, ut_ref):
    x = x_ref[...]
    h = _rms(x, g_ref[...]) * (1.0 + mod_ref[1:2, :]) + mod_ref[0:1, :]
    hb = h.astype(BF16)
    outs = (q_ref, k_ref, v1_ref, og_ref, misc_ref, mlac_ref, z_ref, xbc_ref)
    off = 0
    for (name, width), o_ref in zip(_IN_SPLITS, outs):
        y = _dot(hb, w_ref[:, off:off + width])
        if name == "v1":
            lane = lax.broadcasted_iota(jnp.int32, (1, width), 1)
            y = y + jnp.where((lane & (LANES - 1)) == MLA_V, 1.0, 0.0)
        o_ref[...] = y.astype(o_ref.dtype)
        off += width
    ut_ref[...] = _dot_nt(wut_ref[...], hb)


def _in_projection(x, mod, g, w, wut, seg_of_tile):
    r, d = x.shape
    dtypes = dict(q=BF16, k=BF16, v1=BF16, og=F32, misc=F32, mlac=F32, z=F32, xbc=F32)
    row = lambda i: (i, 0)
    return pl.pallas_call(
        _inproj_kernel,
        out_shape=[jax.ShapeDtypeStruct((r, width), dtypes[name]) for name, width in _IN_SPLITS]
        + [jax.ShapeDtypeStruct((W_GROUP, r), F32)],
        grid=(r // ROW_TILE,),
        in_specs=[
            pl.BlockSpec((ROW_TILE, d), row),
            pl.BlockSpec((None, 8, d), lambda i: (seg_of_tile(i), 0, 0)),
            pl.BlockSpec((1, d), lambda i: (0, 0)),
            pl.BlockSpec((d, _IN_TOTAL), lambda i: (0, 0)),
            pl.BlockSpec((W_GROUP, d), lambda i: (0, 0)),
        ],
        out_specs=[pl.BlockSpec((ROW_TILE, width), row) for _, width in _IN_SPLITS]
        + [pl.BlockSpec((W_GROUP, ROW_TILE), lambda i: (0, i))],
        compiler_params=_cparams(("arbitrary",)),
        name="in_projection",
    )(x, mod, g, w, wut)


def _tri_masks(n):
    row = lax.broadcasted_iota(jnp.int32, (n, n), 0)
    col = lax.broadcasted_iota(jnp.int32, (n, n), 1)
    return col <= row, col >= row


def _mlstm_kernel(qf_ref, kf_ref, vf_ref, mf_ref, qb_ref, kb_ref, vb_ref, mb_ref,
                  gbr_ref, gbc_ref, hf_ref, hb_ref, st_ref, m_ref):
    n = qf_ref.shape[0]

    @pl.when(pl.program_id(1) == 0)
    def _():
        st_ref[...] = jnp.zeros_like(st_ref)
        m_ref[...] = jnp.zeros_like(m_ref)

    tril, triu = _tri_masks(n)
    trilf = tril.astype(F32)
    triuf = triu.astype(F32)
    dirs = ((qf_ref, kf_ref, vf_ref, mf_ref, hf_ref, tril, trilf, triuf, n - 1),
            (qb_ref, kb_ref, vb_ref, mb_ref, hb_ref, triu, triuf, trilf, 0))
    for d, (q_ref, k_ref, v_ref, misc_ref, out_ref, mask, mask_f, mask_tf, last) in enumerate(dirs):
        g = misc_ref[...] + gbr_ref[...]
        gt = misc_ref[...].T + gbc_ref[...]
        bcol = _dot_hi(mask_f, _log_sigmoid(g))
        brow = _dot_hi(_log_sigmoid(gt[0:16, :]), mask_tf)
        for h in range(ML_HEADS):
            hd = d * ML_HEADS + h
            i_idx = MISC_GATE + 8 * d + h
            f_idx = i_idx + ML_HEADS
            ig_col = g[:, i_idx:i_idx + 1]
            ig_row = gt[i_idx:i_idx + 1, :]
            b_col = bcol[:, f_idx:f_idx + 1]
            b_row = brow[f_idx:f_idx + 1, :]
            b_last = b_col[last:last + 1, :]
            m_prev = m_ref[hd:hd + 1, 0:1]
            q = q_ref[:, h * ML_DH:(h + 1) * ML_DH]
            k = k_ref[:, h * ML_DH:(h + 1) * ML_DH]
            v1 = v_ref[:, h * LANES:(h + 1) * LANES]
            state = st_ref[hd]

            dmat = jnp.where(mask, b_col - b_row + ig_row, -jnp.inf)
            inter = b_col + m_prev
            m_t = jnp.maximum(inter, jnp.max(dmat, axis=-1, keepdims=True))
            w_inter = jnp.exp(inter - m_t)
            s = _dot_nt(q, k) * jnp.exp(dmat - m_t)
            tot = _dot(s.astype(BF16), v1) + w_inter * _dot(q, state.astype(BF16))
            den = tot[:, ML_DH:ML_DH + 1]
            hval = tot[:, 0:ML_DH] / jnp.maximum(jnp.abs(den), jnp.exp(-m_t))
            out_ref[:, h * ML_DH:(h + 1) * ML_DH] = hval

            w_log = b_last - b_col + ig_col
            m_new = jnp.maximum(b_last + m_prev, jnp.max(w_log, axis=0, keepdims=True))
            decay = jnp.exp(b_last + m_prev - m_new)
            kw = (k.astype(F32) * jnp.exp(w_log - m_new)).astype(BF16)
            st_ref[hd] = decay * state + _dot_tn(kw, v1)
            m_ref[hd:hd + 1, :] = jnp.broadcast_to(m_new, (1, LANES))


def _scan_index_maps(n_batch, n_chunks):
    ctx0 = n_batch * n_chunks
    fwd = lambda b, s: (jnp.where(s == 0, ctx0 + b, b * n_chunks + s - 1), 0)
    bwd = lambda b, s: (jnp.where(s == 0, ctx0 + b, b * n_chunks + n_chunks - s), 0)
    return fwd, bwd


def _mlstm(q, k, v1, misc, gb_row, gb_col, n_batch, n_chunks):
    r = q.shape[0]
    n = SCAN_CHUNK
    fwd, bwd = _scan_index_maps(n_batch, n_chunks)
    const = lambda b, s: (0, 0)
    specs = []
    for imap in (fwd, bwd):
        specs += [pl.BlockSpec((n, W_GROUP), imap), pl.BlockSpec((n, W_GROUP), imap),
                  pl.BlockSpec((n, ML_HEADS * LANES), imap), pl.BlockSpec((n, LANES), imap)]
    specs += [pl.BlockSpec((1, LANES), const), pl.BlockSpec((LANES, 1), const)]
    return pl.pallas_call(
        _mlstm_kernel,
        out_shape=[jax.ShapeDtypeStruct((r, W_GROUP), F32)] * 2,
        grid=(n_batch, n_chunks + 1),
        in_specs=specs,
        out_specs=[pl.BlockSpec((n, W_GROUP), fwd), pl.BlockSpec((n, W_GROUP), bwd)],
        scratch_shapes=[pltpu.VMEM((2 * ML_HEADS, ML_DH, LANES), F32),
                        pltpu.VMEM((2 * ML_HEADS, LANES), F32)],
        compiler_params=_cparams(("arbitrary", "arbitrary")),
        name="mlstm_scan",
    )(q, k, v1, misc, q, k, v1, misc, gb_row, gb_col)


def _ssd_prep_kernel(x_ref, xp_ref, xn_ref, w_ref, b_ref, act_ref, xs_ref, *, bounds):
    tm = x_ref.shape[0]
    first, last = _segment_masks(pl.program_id(0) * tm, tm, bounds)
    x = x_ref[...]
    rid = lax.broadcasted_iota(jnp.int32, (tm, 1), 0)
    xp = jnp.where(rid == 0, xp_ref[HALO - 1:HALO, :], pltpu.roll(x, 1, 0))
    xn = jnp.where(rid == tm - 1, xn_ref[0:1, :], pltpu.roll(x, tm - 1, 0))
    xp = jnp.where(first, 0.0, xp)
    xn = jnp.where(last, 0.0, xn)
    y = _silu(w_ref[0:1, :] * xp + w_ref[1:2, :] * x + w_ref[2:3, :] * xn + b_ref[...])
    act_ref[...] = y.astype(act_ref.dtype)
    xs_ref[...] = y[:, 0:W_GROUP]


def _halo_specs(width, n_rows):
    per = ROW_TILE // HALO
    n_blocks = n_rows // HALO
    prev = pl.BlockSpec((HALO, width), lambda i: (jnp.maximum(i * per - 1, 0), 0))
    nxt = pl.BlockSpec((HALO, width), lambda i: (jnp.minimum((i + 1) * per, n_blocks - 1), 0))
    return prev, nxt


def _ssd_prep(xbc, conv_w, conv_b, bounds):
    r, width = xbc.shape
    row = lambda i: (i, 0)
    prev, nxt = _halo_specs(width, r)
    return pl.pallas_call(
        functools.partial(_ssd_prep_kernel, bounds=bounds),
        out_shape=[jax.ShapeDtypeStruct((r, width), BF16), jax.ShapeDtypeStruct((r, W_GROUP), F32)],
        grid=(r // ROW_TILE,),
        in_specs=[pl.BlockSpec((ROW_TILE, width), row), prev, nxt,
                  pl.BlockSpec((8, width), lambda i: (0, 0)),
                  pl.BlockSpec((1, width), lambda i: (0, 0))],
        out_specs=[pl.BlockSpec((ROW_TILE, width), row), pl.BlockSpec((ROW_TILE, W_GROUP), row)],
        compiler_params=_cparams(("arbitrary",)),
        name="ssd_conv",
    )(xbc, xbc, xbc, conv_w, conv_b)


def _ssd_kernel(af_ref, mf_ref, ab_ref, mb_ref, pr_ref, pc_ref, yf_ref, yb_ref, st_ref):
    n = af_ref.shape[0]

    @pl.when(pl.program_id(1) == 0)
    def _():
        st_ref[...] = jnp.zeros_like(st_ref)

    tril, triu = _tri_masks(n)
    trilf = tril.astype(F32)
    triuf = triu.astype(F32)
    dirs = ((af_ref, mf_ref, yf_ref, tril, trilf, triuf, n - 1),
            (ab_ref, mb_ref, yb_ref, triu, triuf, trilf, 0))
    for d, (act_ref, misc_ref, out_ref, mask, mask_f, mask_tf, last) in enumerate(dirs):
        dt_c = _softplus(misc_ref[...] + pr_ref[0:1, :])
        dt_r = _softplus(misc_ref[...].T + pc_ref[:, 0:1])
        cs_c = _dot_hi(mask_f, dt_c * pr_ref[1:2, :])
        cs_r = _dot_hi((dt_r * pc_ref[:, 1:2])[0:32, :], mask_tf)
        for g in range(SSD_GROUPS):
            bm = act_ref[:, W_GROUP + g * SSD_STATE:W_GROUP + (g + 1) * SSD_STATE]
            cm = act_ref[:, W_GROUP + SSD_GROUPS * SSD_STATE + g * SSD_STATE:
                         W_GROUP + SSD_GROUPS * SSD_STATE + (g + 1) * SSD_STATE]
            gmat = _dot_nt(cm, bm)
            for hh in range(SSD_HEADS // SSD_GROUPS):
                h = g * (SSD_HEADS // SSD_GROUPS) + hh
                hd = d * SSD_HEADS + h
                idx = MISC_DT + d * SSD_HEADS + h
                cs_col = cs_c[:, idx:idx + 1]
                cs_row = cs_r[idx:idx + 1, :]
                cs_last = cs_col[last:last + 1, :]
                dt_col = dt_c[:, idx:idx + 1]
                dt_row = dt_r[idx:idx + 1, :]
                xh = act_ref[:, h * SSD_HEADDIM:(h + 1) * SSD_HEADDIM]
                state = st_ref[hd]
                decay = jnp.exp(jnp.where(mask, cs_col - cs_row, -jnp.inf))
                scores = (gmat * decay * dt_row).astype(BF16)
                y = _dot(scores, xh) + jnp.exp(cs_col) * _dot(cm, state.astype(BF16))
                out_ref[:, h * SSD_HEADDIM:(h + 1) * SSD_HEADDIM] = y
                wk = (bm.astype(F32) * (jnp.exp(cs_last - cs_col) * dt_col)).astype(BF16)
                st_ref[hd] = jnp.exp(cs_last) * state + _dot_tn(wk, xh)


def _ssd_scan(act, misc, p_row, p_col, n_batch, n_chunks):
    r, width = act.shape
    n = SCAN_CHUNK
    fwd, bwd = _scan_index_maps(n_batch, n_chunks)
    const = lambda b, s: (0, 0)
    specs = []
    for imap in (fwd, bwd):
        specs += [pl.BlockSpec((n, width), imap), pl.BlockSpec((n, LANES), imap)]
    specs += [pl.BlockSpec((8, LANES), const), pl.BlockSpec((LANES, 8), const)]
    return pl.pallas_call(
        _ssd_kernel,
        out_shape=[jax.ShapeDtypeStruct((r, W_GROUP), F32)] * 2,
        grid=(n_batch, n_chunks + 1),
        in_specs=specs,
        out_specs=[pl.BlockSpec((n, W_GROUP), fwd), pl.BlockSpec((n, W_GROUP), bwd)],
        scratch_shapes=[pltpu.VMEM((2 * SSD_HEADS, SSD_STATE, SSD_HEADDIM), F32)],
        compiler_params=_cparams(("arbitrary", "arbitrary")),
        name="ssd_scan",
    )(act, misc, act, misc, p_row, p_col)


def _mla_prep_kernel(c_ref, misc_ref, cos_ref, sin_ref, qn_ref, kvn_ref, wqa_ref, wqb_ref,
                     wka_ref, wvt_ref, ea_ref, eb_ref, q_ref, k_ref, vt_ref):
    scale = (MLA_NOPE + MLA_ROPE) ** -0.5 * math.log2(math.e)
    cq = _rms(c_ref[:, 0:MLA_Q_LORA], qn_ref[...]).astype(BF16)
    ckv = _rms(c_ref[:, MLA_Q_LORA:MLA_Q_LORA + MLA_KV_LORA], kvn_ref[...]).astype(BF16)
    cos = cos_ref[...]
    sin = sin_ref[...]
    qa = _dot(cq, wqa_ref[...])
    qb = _dot(cq, wqb_ref[...])
    ka = _dot(ckv, wka_ref[...])
    misc = misc_ref[...]
    kr = _dot_hi(misc, ea_ref[...]) * cos + _dot_hi(misc, eb_ref[...]) * sin
    for h in range(MLA_HEADS):
        sl = slice(h * LANES, (h + 1) * LANES)
        q_ref[:, sl] = ((qa[:, sl] * cos + qb[:, sl] * sin) * scale).astype(BF16)
        k_ref[:, sl] = (ka[:, sl] + kr).astype(BF16)
    vt = _dot_nt(wvt_ref[...], ckv)
    row = lax.broadcasted_iota(jnp.int32, (vt.shape[0], 1), 0)
    vt_ref[...] = (vt + jnp.where((row & (LANES - 1)) == MLA_V, 1.0, 0.0)).astype(BF16)


def _mla_prep(mlac, misc, cos, sin, qn, kvn, wqa, wqb, wka, wvt, ea, eb):
    r = mlac.shape[0]
    row = lambda i: (i, 0)
    full = lambda a: pl.BlockSpec(a.shape, lambda i: (0, 0))
    hw = MLA_HEADS * LANES
    return pl.pallas_call(
        _mla_prep_kernel,
        out_shape=[jax.ShapeDtypeStruct((r, hw), BF16)] * 2 + [jax.ShapeDtypeStruct((hw, r), BF16)],
        grid=(r // ROW_TILE,),
        in_specs=[pl.BlockSpec((ROW_TILE, mlac.shape[1]), row), pl.BlockSpec((ROW_TILE, LANES), row),
                  pl.BlockSpec((ROW_TILE, LANES), row), pl.BlockSpec((ROW_TILE, LANES), row),
                  full(qn), full(kvn), full(wqa), full(wqb), full(wka), full(wvt), full(ea), full(eb)],
        out_specs=[pl.BlockSpec((ROW_TILE, hw), row)] * 2 + [pl.BlockSpec((hw, ROW_TILE), lambda i: (0, i))],
        compiler_params=_cparams(("arbitrary",)),
        name="mla_prep",
    )(mlac, misc, cos, sin, qn, kvn, wqa, wqb, wka, wvt, ea, eb)


_HEAD_LANES = tuple(slice(h * LANES, (h + 1) * LANES) for h in range(MLA_HEADS))


def _attn_scores(q_ref, k_ref, rows, n_keys, s_ref):
    for h, sl in enumerate(_HEAD_LANES):
        s_ref[h, 0:n_keys, :] = _dot_nt(k_ref[rows, sl], q_ref[:, sl])


def _attn_update(s_ref, n_keys, vt_ref, cols, m_ref, acc_ref):
    new = []
    for h, sl in enumerate(_HEAD_LANES):
        st = s_ref[h, 0:n_keys, :]
        m_old = m_ref[h]
        m_new = jnp.maximum(m_old, jnp.max(st, axis=0, keepdims=True))
        p = jnp.exp2(st - m_new).astype(BF16)
        new.append((m_new, jnp.exp2(m_old - m_new) * acc_ref[h] + _dot(vt_ref[sl, cols], p)))
    for h, (m_new, acc) in enumerate(new):
        m_ref[h] = m_new
        acc_ref[h] = acc


def _attn_init(m_ref, acc_ref):
    m_ref[...] = jnp.full_like(m_ref, -jnp.inf)
    acc_ref[...] = jnp.zeros_like(acc_ref)


def _attn_finish(o_ref, acc_ref):
    for h in range(MLA_HEADS):
        acc = acc_ref[h]
        o_ref[h * MLA_V:(h + 1) * MLA_V, :] = (acc[0:MLA_V] / acc[MLA_V:MLA_V + 1]).astype(o_ref.dtype)


def _attn_lat_kernel(q_ref, kl_ref, vlt_ref, kc_ref, vct_ref, o_ref, m_ref, acc_ref, sa_ref, sb_ref):
    tk = ATTN_TK
    n_pairs = kl_ref.shape[0] // (2 * tk)
    n_ctx = kc_ref.shape[0]
    chunk = lambda c: pl.ds(pl.multiple_of(c * tk, tk), tk)
    _attn_init(m_ref, acc_ref)
    _attn_scores(q_ref, kl_ref, chunk(0), tk, sa_ref)

    def body(i, carry):
        c = 2 * i
        _attn_scores(q_ref, kl_ref, chunk(c + 1), tk, sb_ref)
        _attn_update(sa_ref, tk, vlt_ref, chunk(c), m_ref, acc_ref)
        _attn_scores(q_ref, kl_ref, chunk(c + 2), tk, sa_ref)
        _attn_update(sb_ref, tk, vlt_ref, chunk(c + 1), m_ref, acc_ref)
        return carry

    lax.fori_loop(0, n_pairs - 1, body, 0)
    c = 2 * (n_pairs - 1)
    _attn_scores(q_ref, kl_ref, chunk(c + 1), tk, sb_ref)
    _attn_update(sa_ref, tk, vlt_ref, chunk(c), m_ref, acc_ref)
    _attn_scores(q_ref, kc_ref, slice(None), n_ctx, sa_ref)
    _attn_update(sb_ref, tk, vlt_ref, chunk(c + 1), m_ref, acc_ref)
    _attn_update(sa_ref, n_ctx, vct_ref, slice(None), m_ref, acc_ref)
    _attn_finish(o_ref, acc_ref)


def _attn_ctx_kernel(q_ref, kc_ref, vct_ref, prev_ref, o_ref, m_ref, acc_ref, s_ref):
    del prev_ref
    _attn_init(m_ref, acc_ref)
    _attn_scores(q_ref, kc_ref, slice(None), kc_ref.shape[0], s_ref)
    _attn_update(s_ref, kc_ref.shape[0], vct_ref, slice(None), m_ref, acc_ref)
    _attn_finish(o_ref, acc_ref)


def _attention(q, k, vt, n_batch, t_lat, t_ctx):
    r, hw = q.shape
    nq = t_lat // ATTN_TQ
    ctx_blk = n_batch * t_lat // t_ctx
    assert t_lat % (2 * ATTN_TK) == 0 and t_ctx <= ATTN_TK
    scratch = lambda tq: [pltpu.VMEM((MLA_HEADS, 1, tq), F32), pltpu.VMEM((MLA_HEADS, LANES, tq), F32)]
    scores = lambda tk, tq: pltpu.VMEM((MLA_HEADS, tk, tq), F32)
    lat = pl.pallas_call(
        _attn_lat_kernel,
        out_shape=jax.ShapeDtypeStruct((W_GROUP, r), BF16),
        grid=(n_batch, nq),
        in_specs=[pl.BlockSpec((ATTN_TQ, hw), lambda b, i: (b * nq + i, 0)),
                  pl.BlockSpec((t_lat, hw), lambda b, i: (b, 0)),
                  pl.BlockSpec((hw, t_lat), lambda b, i: (0, b)),
                  pl.BlockSpec((t_ctx, hw), lambda b, i: (ctx_blk + b, 0)),
                  pl.BlockSpec((hw, t_ctx), lambda b, i: (0, ctx_blk + b))],
        out_specs=pl.BlockSpec((W_GROUP, ATTN_TQ), lambda b, i: (0, b * nq + i)),
        scratch_shapes=scratch(ATTN_TQ) + [scores(ATTN_TK, ATTN_TQ)] * 2,
        compiler_params=_cparams(("arbitrary", "arbitrary")),
        name="mla_attention_latent",
    )(q, k, vt, k, vt)
    return pl.pallas_call(
        _attn_ctx_kernel,
        out_shape=jax.ShapeDtypeStruct((W_GROUP, r), BF16),
        grid=(n_batch,),
        in_specs=[pl.BlockSpec((t_ctx, hw), lambda b: (ctx_blk + b, 0)),
                  pl.BlockSpec((t_ctx, hw), lambda b: (ctx_blk + b, 0)),
                  pl.BlockSpec((hw, t_ctx), lambda b: (0, ctx_blk + b)),
                  pl.BlockSpec(memory_space=pl.ANY)],
        out_specs=pl.BlockSpec((W_GROUP, t_ctx), lambda b: (0, ctx_blk + b)),
        scratch_shapes=scratch(t_ctx) + [scores(t_ctx, t_ctx)],
        input_output_aliases={3: 0},
        compiler_params=_cparams(("arbitrary",)),
        name="mla_attention_context",
    )(q, k, vt, lat)


def _cis_pow(lr, li, dt, k):
    mag = jnp.exp(lr * dt * k)
    ang = li * dt * k
    return mag * jnp.cos(ang), mag * jnp.sin(ang)


def _s5_tables_kernel(pr_ref, pc_ref, btre_ref, btim_ref, ctre_ref, ctim_ref,
                      toep_ref, wz_ref, wy_ref, decay_ref):
    lc, ng, ns = S5_CHUNK, S5_GROUP, S5_STATE
    lr_r = jnp.minimum(pr_ref[0:1, :], -1e-4)
    li_r = pr_ref[1:2, :]
    dt_r = jnp.exp(pr_ref[2:3, :])
    lr_c = jnp.minimum(pc_ref[:, 0:1], -1e-4)
    li_c = pc_ref[:, 1:2]
    dt_c = jnp.exp(pc_ref[:, 2:3])
    lane4 = lax.broadcasted_iota(jnp.int32, (1, S5_SW), 1)
    odd_lane = (lane4 & ns) != 0
    row4 = lax.broadcasted_iota(jnp.int32, (S5_SW, 1), 0)
    odd_row = (row4 & ns) != 0

    ab_re, ab_im = _cis_pow(lr_r, li_r, dt_r, 1.0)
    den = lr_r * lr_r + li_r * li_r
    f_re = ((ab_re - 1.0) * lr_r + ab_im * li_r) / den
    f_im = (ab_im * lr_r - (ab_re - 1.0) * li_r) / den
    bb_re = f_re * btre_ref[...] - f_im * btim_ref[...]
    bb_im = f_re * btim_ref[...] + f_im * btre_ref[...]

    s_col = lax.broadcasted_iota(jnp.int32, (lc, 1), 0)
    expo = jnp.where(lane4 < 2 * ns, lc - 1 - s_col, s_col).astype(F32)
    pz_re, pz_im = _cis_pow(lr_r, li_r, dt_r, expo)
    xz = jnp.where(odd_lane, bb_im, bb_re)
    yz = jnp.where(odd_lane, bb_re, -bb_im)
    for j in range(ng):
        wz_ref[j] = (pz_re * xz[j:j + 1, :] + pz_im * yz[j:j + 1, :]).astype(BF16)

    al_re, al_im = _cis_pow(lr_r, li_r, dt_r, float(lc))
    al_sw = jnp.where(odd_lane, al_im, -al_im)
    decay_ref[...] = jnp.zeros_like(decay_ref)
    for d in range(2):
        decay_ref[d:d + 1, :] = al_re[:, d * LANES:(d + 1) * LANES]
        decay_ref[2 + d:3 + d, :] = al_sw[:, d * LANES:(d + 1) * LANES]

    t_lane = lax.broadcasted_iota(jnp.int32, (1, LANES), 1) & (lc - 1)
    expo = jnp.where(row4 < 2 * ns, t_lane + 1, lc - t_lane).astype(F32)
    py_re, py_im = _cis_pow(lr_c, li_c, dt_c, expo)
    reps = lc * ng // LANES
    py_re = jnp.tile(py_re, (1, reps))
    py_im = jnp.tile(py_im, (1, reps))
    lane_j = lax.shift_right_logical(lax.broadcasted_iota(jnp.int32, (ng, lc * ng), 1), lc.bit_length() - 1)
    sel = (lane_j == lax.broadcasted_iota(jnp.int32, (ng, lc * ng), 0)).astype(F32)
    ce_re = _dot_hi(ctre_ref[...], sel)
    ce_im = _dot_hi(ctim_ref[...], sel)
    wy_ref[...] = jnp.where(odd_row, -(ce_re * py_im + ce_im * py_re),
                            ce_re * py_re - ce_im * py_im).astype(BF16)

    m_lane = lax.broadcasted_iota(jnp.int32, (1, 2 * lc), 1)
    lane_j2 = lax.shift_right_logical(lax.broadcasted_iota(jnp.int32, (ng, 2 * lc * ng), 1),
                                      (2 * lc).bit_length() - 1)
    sel2 = (lane_j2 == lax.broadcasted_iota(jnp.int32, (ng, 2 * lc * ng), 0)).astype(F32)
    xmat = None
    for d in range(2):
        rows = slice(d * 2 * ns, d * 2 * ns + ns)
        lag = m_lane - (lc - 1) if d == 0 else (lc - 1) - m_lane
        valid = (lag >= 0) & (m_lane < 2 * lc - 1)
        pm_re, pm_im = _cis_pow(lr_c[rows], li_c[rows], dt_c[rows], jnp.where(valid, lag, 0).astype(F32))
        pm_re = jnp.tile(jnp.where(valid, pm_re, 0.0), (1, ng))
        pm_im = jnp.tile(jnp.where(valid, pm_im, 0.0), (1, ng))
        c2_re = _dot_hi(ctre_ref[rows, :], sel2)
        c2_im = _dot_hi(ctim_ref[rows, :], sel2)
        p_re = c2_re * pm_re - c2_im * pm_im
        p_im = c2_re * pm_im + c2_im * pm_re
        lanes = slice(d * 2 * ns, d * 2 * ns + ns)
        term = _dot_hi(bb_re[:, lanes], p_re) - _dot_hi(bb_im[:, lanes], p_im)
        xmat = term if xmat is None else xmat + term

    low_half = lax.broadcasted_iota(jnp.int32, (1, 2 * lc), 1) < lc
    for j in range(ng):
        xb = jnp.broadcast_to(xmat[j:j + 1, :], (lc, 2 * lc * ng))
        cols = []
        for p in range(ng // 2):
            even = xb[:, (2 * p) * 2 * lc:(2 * p + 1) * 2 * lc]
            odd = xb[:, (2 * p + 1) * 2 * lc:(2 * p + 2) * 2 * lc]
            cols.append(jnp.where(low_half,
                                  pltpu.roll(even, lc + 1, 1, stride=1, stride_axis=0),
                                  pltpu.roll(odd, 1, 1, stride=1, stride_axis=0)))
        toep_ref[j] = jnp.concatenate(cols, axis=1).astype(BF16)


def _s5_tables(a_re, a_im, log_dt, b_re, b_im, c_re, c_im):
    depth = a_re.shape[0]
    g, n, j, lc = S5_NGROUPS, S5_STATE, S5_GROUP, S5_CHUNK
    n_all = depth * g

    def parts(v):
        v = jnp.transpose(v, (0, 2, 1, 3))
        return jnp.concatenate([v[:, :, 0], v[:, :, 0], v[:, :, 1], v[:, :, 1]], axis=-1).reshape(n_all, S5_SW)

    rows = jnp.stack([parts(a_re), parts(a_im), parts(jnp.broadcast_to(log_dt[..., None], a_re.shape))], axis=1)
    p_row = jnp.pad(rows, ((0, 0), (0, 5), (0, 0)))
    p_col = jnp.swapaxes(p_row, 1, 2)
    bt = lambda b: jnp.tile(jnp.swapaxes(b.reshape(n_all, n, j), 1, 2), (1, 1, 4))
    ct = lambda c: jnp.tile(jnp.swapaxes(c.reshape(n_all, j, n), 1, 2), (1, 4, 1))
    own = lambda i: (i, 0, 0)
    width = lc * j
    return pl.pallas_call(
        _s5_tables_kernel,
        out_shape=[jax.ShapeDtypeStruct((n_all * j, lc, width), BF16),
                   jax.ShapeDtypeStruct((n_all * j, lc, S5_SW), BF16),
                   jax.ShapeDtypeStruct((n_all, S5_SW, width), BF16),
                   jax.ShapeDtypeStruct((n_all, 8, LANES), F32)],
        grid=(n_all,),
        in_specs=[pl.BlockSpec((None, 8, S5_SW), own), pl.BlockSpec((None, S5_SW, 8), own),
                  pl.BlockSpec((None, j, S5_SW), own), pl.BlockSpec((None, j, S5_SW), own),
                  pl.BlockSpec((None, S5_SW, j), own), pl.BlockSpec((None, S5_SW, j), own)],
        out_specs=[pl.BlockSpec((j, lc, width), own), pl.BlockSpec((j, lc, S5_SW), own),
                   pl.BlockSpec((None, S5_SW, width), own), pl.BlockSpec((None, 8, LANES), own)],
        compiler_params=_cparams(("arbitrary",)),
        name="s5_tables",
    )(p_row, p_col, bt(b_re), bt(b_im), ct(c_re), ct(c_im))


def _s5_local_kernel(u_ref, wz_ref, z_ref):
    acc = None
    for j in range(S5_GROUP):
        term = _dot(u_ref[j].astype(BF16), wz_ref[j])
        acc = term if acc is None else acc + term
    z_ref[...] = acc


def _s5_scan_kernel(z_ref, ca_ref, cb_ref, x_ref, xf_ref, xb_ref, *, n_batch, n_lat, n_ctx):
    ca = ca_ref[...]
    cb = cb_ref[...]
    zero = jnp.zeros(ca.shape, F32)
    ctx0 = n_batch * n_lat

    def step(x, z):
        return x * ca + pltpu.roll(x, LANES // 2, 1) * cb + z

    def body(i, carry):
        new = []
        for b in range(n_batch):
            rf = jnp.where(i < n_ctx, ctx0 + b * n_ctx + i, b * n_lat + i - n_ctx)
            rb = jnp.where(i < n_ctx, ctx0 + b * n_ctx + n_ctx - 1 - i, b * n_lat + n_lat - 1 - (i - n_ctx))
            xf, xb = carry[2 * b], carry[2 * b + 1]
            xf_ref[rf] = xf
            xb_ref[rb] = xb
            new += [step(xf, z_ref[rf]), step(xb, z_ref[rb])]
        return tuple(new)

    lax.fori_loop(0, n_ctx + n_lat, body, (zero,) * (2 * n_batch))
    row = lax.broadcasted_iota(jnp.int32, x_ref.shape, 1)
    x_ref[...] = jnp.where((row & 1) == 0, xf_ref[...], xb_ref[...])


def _s5_out_kernel(u_ref, t_ref, x_ref, wy_ref, y_ref):
    acc = _dot(x_ref[...].astype(BF16), wy_ref[...])
    for j in range(S5_GROUP):
        acc = acc + _dot(u_ref[j].astype(BF16), t_ref[j])
    for j in range(S5_GROUP):
        y_ref[j] = acc[:, j * S5_CHUNK:(j + 1) * S5_CHUNK]


def _s5_mix(ut, toep, wz, wy, ca, cb, layer, n_batch, n_lat, n_ctx):
    g, j, lc = S5_NGROUPS, S5_GROUP, S5_CHUNK
    nr = ut.shape[1] // lc
    u3 = ut.reshape(W_GROUP, nr, lc)
    grp = lambda i: (i, 0, 0)
    tab = lambda i: (layer * g + i, 0, 0)
    z = pl.pallas_call(
        _s5_local_kernel,
        out_shape=jax.ShapeDtypeStruct((nr, g * S5_SW), F32),
        grid=(g,),
        in_specs=[pl.BlockSpec((j, nr, lc), grp), pl.BlockSpec((j, lc, S5_SW), tab)],
        out_specs=pl.BlockSpec((nr, S5_SW), lambda i: (0, i)),
        compiler_params=_cparams(("arbitrary",)),
        name="s5_local_state",
    )(u3, wz)
    z3 = z.reshape(nr, 2 * g, LANES)
    whole = lambda shp: pl.BlockSpec(shp, lambda: tuple(0 for _ in shp))
    x3 = pl.pallas_call(
        functools.partial(_s5_scan_kernel, n_batch=n_batch, n_lat=n_lat, n_ctx=n_ctx),
        out_shape=jax.ShapeDtypeStruct(z3.shape, F32),
        in_specs=[whole(z3.shape), whole(ca.shape), whole(cb.shape)],
        out_specs=whole(z3.shape),
        scratch_shapes=[pltpu.VMEM(z3.shape, F32), pltpu.VMEM(z3.shape, F32)],
        compiler_params=pltpu.CompilerParams(vmem_limit_bytes=VMEM_LIMIT),
        name="s5_chunk_scan",
    )(z3, ca, cb)
    x = x3.reshape(nr, g * S5_SW)
    y3 = pl.pallas_call(
        _s5_out_kernel,
        out_shape=jax.ShapeDtypeStruct((W_GROUP, nr, lc), F32),
        grid=(g,),
        in_specs=[pl.BlockSpec((j, nr, lc), grp), pl.BlockSpec((j, lc, lc * j), tab),
                  pl.BlockSpec((nr, S5_SW), lambda i: (0, i)),
                  pl.BlockSpec((None, S5_SW, lc * j), tab)],
        out_specs=pl.BlockSpec((j, nr, lc), grp),
        compiler_params=_cparams(("arbitrary",)),
        name="s5_output",
    )(u3, toep, x, wy)
    return y3.reshape(W_GROUP, nr * lc)


def _outproj_kernel(x_ref, mod_ref, hf_ref, hb_ref, og_ref, yf_ref, yb_ref, xs_ref, z_ref,
                    att_ref, s5_ref, ut_ref, mln_ref, sd_ref, sn_ref, s5d_ref, glu_ref, wo_ref, n2_ref,
                    xo_ref, h2_ref):
    h = hf_ref[...] + hb_ref[...]
    parts = []
    for i in range(ML_HEADS):
        hh = h[:, i * ML_DH:(i + 1) * ML_DH]
        parts.append(hh * lax.rsqrt(jnp.mean(hh * hh, axis=-1, keepdims=True) + EPS))
    ya = jnp.concatenate(parts, axis=-1) * mln_ref[...] * _sigmoid(og_ref[...])
    yc = (sd_ref[...] * xs_ref[...] + yf_ref[...] + yb_ref[...]) * _silu(z_ref[...])
    yc = _rms(yc, sn_ref[...])
    yd = _gelu_tanh(s5d_ref[...] * ut_ref[...] + s5_ref[...])
    yd = yd * _sigmoid(_dot(glu_ref[...], yd.astype(BF16)))
    y = (_dot(ya.astype(BF16), wo_ref[0:W_GROUP, :])
         + _dot_tn(att_ref[...], wo_ref[W_GROUP:2 * W_GROUP, :])
         + _dot(yc.astype(BF16), wo_ref[2 * W_GROUP:3 * W_GROUP, :])
         + _dot_tn(yd.astype(BF16), wo_ref[3 * W_GROUP:4 * W_GROUP, :]))
    x = x_ref[...] + mod_ref[2:3, :] * y
    xo_ref[...] = x
    h2 = _rms(x, n2_ref[...]) * (1.0 + mod_ref[4:5, :]) + mod_ref[3:4, :]
    h2_ref[...] = h2.astype(BF16)


def _out_projection(x, mod, seg_of_tile, row_acts, col_acts, params):
    r, d = x.shape
    row = lambda i: (i, 0)
    full = lambda a: pl.BlockSpec(a.shape, lambda i: (0, 0))
    return pl.pallas_call(
        _outproj_kernel,
        out_shape=[jax.ShapeDtypeStruct((r, d), F32), jax.ShapeDtypeStruct((r, d), BF16)],
        grid=(r // ROW_TILE,),
        in_specs=[pl.BlockSpec((ROW_TILE, d), row),
                  pl.BlockSpec((None, 8, d), lambda i: (seg_of_tile(i), 0, 0))]
        + [pl.BlockSpec((ROW_TILE, W_GROUP), row) for _ in row_acts]
        + [pl.BlockSpec((W_GROUP, ROW_TILE), lambda i: (0, i)) for _ in col_acts]
        + [full(p) for p in params],
        out_specs=[pl.BlockSpec((ROW_TILE, d), row)] * 2,
        compiler_params=_cparams(("arbitrary",)),
        name="out_projection",
    )(x, mod, *row_acts, *col_acts, *params)


def _ffn_kernel(x_ref, mod_ref, h_ref, hp_ref, hn_ref, wup_ref, cw_ref, wdn_ref, o_ref, acc_ref, *, bounds):
    tm = h_ref.shape[0]
    first, last = _segment_masks(pl.program_id(0) * tm, tm, bounds)
    h = h_ref[...]
    h_ext = jnp.concatenate([hp_ref[...], h, hn_ref[...]], axis=0)
    n_ext = tm + 2 * HALO
    n_chunks = D_FF // FF_CHUNK

    def up(j):
        c0 = j * FF_CHUNK
        return (_dot(h, wup_ref[:, c0:c0 + FF_CHUNK]),
                _dot(h_ext, wup_ref[:, D_FF + c0:D_FF + c0 + FF_CHUNK]))

    def down(j, u, g_ext):
        c0 = j * FF_CHUNK
        gp = jnp.where(first, 0.0, pltpu.roll(g_ext, 1, 0)[HALO:HALO + tm])
        gn = jnp.where(last, 0.0, pltpu.roll(g_ext, n_ext - 1, 0)[HALO:HALO + tm])
        gc = g_ext[HALO:HALO + tm]
        cw = cw_ref[:, c0:c0 + FF_CHUNK]
        conv = cw[0:1, :] * gp + cw[1:2, :] * gc + cw[2:3, :] * gn
        act = (_silu(conv) * u).astype(BF16)
        return _dot(act, wdn_ref[c0:c0 + FF_CHUNK, :])

    nxt = up(0)
    for j in range(n_chunks):
        cur = nxt
        if j + 1 < n_chunks:
            nxt = up(j + 1)
        y = down(j, *cur)
        if j == 0:
            acc_ref[...] = y
        else:
            acc_ref[...] += y
    o_ref[...] = x_ref[...] + mod_ref[5:6, :] * acc_ref[...]


def _conv_ffn(x, mod, seg_of_tile, h2, w_up, conv_w, w_down, bounds):
    r, d = x.shape
    row = lambda i: (i, 0)
    prev, nxt = _halo_specs(d, r)
    full = lambda a: pl.BlockSpec(a.shape, lambda i: (0, 0))
    return pl.pallas_call(
        functools.partial(_ffn_kernel, bounds=bounds),
        out_shape=jax.ShapeDtypeStruct((r, d), F32),
        grid=(r // ROW_TILE,),
        in_specs=[pl.BlockSpec((ROW_TILE, d), row),
                  pl.BlockSpec((None, 8, d), lambda i: (seg_of_tile(i), 0, 0)),
                  pl.BlockSpec((ROW_TILE, d), row), prev, nxt,
                  full(w_up), full(conv_w), full(w_down)],
        out_specs=pl.BlockSpec((ROW_TILE, d), row),
        scratch_shapes=[pltpu.VMEM((ROW_TILE, d), F32)],
        compiler_params=_cparams(("arbitrary",)),
        name="conv_ffn",
    )(x, mod, h2, h2, h2, w_up, conv_w, w_down)


def _final_norm_kernel(x_ref, g_ref, o_ref):
    o_ref[...] = _rms(x_ref[...], g_ref[...])


def _final_norm(x, g, n_rows):
    d = x.shape[1]
    return pl.pallas_call(
        _final_norm_kernel,
        out_shape=jax.ShapeDtypeStruct((n_rows, d), F32),
        grid=(n_rows // ROW_TILE,),
        in_specs=[pl.BlockSpec((ROW_TILE, d), lambda i: (i, 0)), pl.BlockSpec((1, d), lambda i: (0, 0))],
        out_specs=pl.BlockSpec((ROW_TILE, d), lambda i: (i, 0)),
        compiler_params=_cparams(("arbitrary",)),
        name="final_norm",
    )(x, g)


def _rope_tables(t_lat, n_ctx_rows):
    rows = t_lat // GRID_W
    row = jnp.broadcast_to(jnp.arange(rows)[:, None], (rows, GRID_W)).reshape(-1).astype(F32)
    col = jnp.broadcast_to(jnp.arange(GRID_W)[None, :], (rows, GRID_W)).reshape(-1).astype(F32)
    n_freq = MLA_ROPE // 4
    inv = ROPE_BASE ** (-jnp.arange(n_freq, dtype=F32) / n_freq)
    ang = jnp.concatenate([row[:, None] * inv, col[:, None] * inv], axis=-1)
    half = MLA_ROPE // 2
    pad = LANES - MLA_NOPE - MLA_ROPE

    def table(t, lead):
        return jnp.concatenate([jnp.full((t_lat, MLA_NOPE), lead, F32), t, t, jnp.zeros((t_lat, pad), F32)], axis=-1)

    cos = table(jnp.cos(ang), 1.0)
    sin = table(jnp.sin(ang), 0.0)
    lane = np.arange(LANES)
    cos_ctx = jnp.broadcast_to(jnp.asarray((lane < MLA_NOPE + 2 * half).astype(np.float32)), (n_ctx_rows, LANES))
    sin_ctx = jnp.zeros((n_ctx_rows, LANES), F32)
    return cos, sin, cos_ctx, sin_ctx


def _mla_weights(w_uq, w_ukv):
    half = MLA_ROPE // 2
    qd = MLA_NOPE + MLA_ROPE
    kd = MLA_NOPE + MLA_V
    hw = MLA_HEADS * LANES
    qa_idx = np.zeros((hw,), np.int32); qa_s = np.zeros((hw,), np.float32)
    qb_idx = np.zeros((hw,), np.int32); qb_s = np.zeros((hw,), np.float32)
    ka_idx = np.zeros((hw,), np.int32); ka_s = np.zeros((hw,), np.float32)
    v_idx = np.zeros((hw,), np.int32); v_s = np.zeros((hw,), np.float32)
    for h in range(MLA_HEADS):
        for i in range(qd):
            qa_idx[h * LANES + i] = h * qd + i
            qa_s[h * LANES + i] = 1.0
        for i in range(half):
            qb_idx[h * LANES + MLA_NOPE + i] = h * qd + MLA_NOPE + half + i
            qb_s[h * LANES + MLA_NOPE + i] = -1.0
            qb_idx[h * LANES + MLA_NOPE + half + i] = h * qd + MLA_NOPE + i
            qb_s[h * LANES + MLA_NOPE + half + i] = 1.0
        for i in range(MLA_NOPE):
            ka_idx[h * LANES + i] = h * kd + i
            ka_s[h * LANES + i] = 1.0
        for i in range(MLA_V):
            v_idx[h * LANES + i] = h * kd + MLA_NOPE + i
            v_s[h * LANES + i] = 1.0
    pick = lambda w, idx, s: (w[..., idx] * s).astype(BF16)
    ea = np.zeros((LANES, LANES), np.float32)
    eb = np.zeros((LANES, LANES), np.float32)
    for i in range(MLA_ROPE):
        ea[MISC_KR + i, MLA_NOPE + i] = 1.0
    for i in range(half):
        eb[MISC_KR + half + i, MLA_NOPE + i] = -1.0
        eb[MISC_KR + i, MLA_NOPE + half + i] = 1.0
    return (pick(w_uq, qa_idx, qa_s), pick(w_uq, qb_idx, qb_s), pick(w_ukv, ka_idx, ka_s),
            pick(w_ukv, v_idx, v_s).T, jnp.asarray(ea), jnp.asarray(eb))


def _lane_rows(vals, offset, n_rows=8):
    k, n = vals.shape
    out = jnp.zeros((n_rows, LANES), F32)
    return out.at[:k, offset:offset + n].set(vals)


def kernel(x, c, ctx, c_ctx, w_mod, b_mod, norm1, norm2, w_in, ml_gate_bias, ml_norm, mla_q_norm, mla_kv_norm, mla_w_uq, mla_w_ukv, ssd_conv_w, ssd_conv_b, ssd_a_log, ssd_dt_bias, ssd_d, ssd_norm, s5_a_re, s5_a_im, s5_log_dt, s5_b_re, s5_b_im, s5_c_re, s5_c_im, s5_d, s5_w_glu, w_out, ffn_w_up, ffn_conv_w, ffn_w_down, final_norm):
    n_batch, t_lat, d = x.shape
    t_ctx = ctx.shape[1]
    depth = w_mod.shape[0]
    assert d == D_MODEL and t_ctx == SCAN_CHUNK and t_lat % ROW_TILE == 0
    assert (n_batch * t_ctx) % ROW_TILE == 0 and t_lat % ATTN_TK == 0 and t_lat % GRID_W == 0
    n_lat = n_batch * t_lat
    n_scan = t_lat // SCAN_CHUNK
    bounds = tuple(b * t_lat for b in range(n_batch)) + tuple(n_lat + b * t_ctx for b in range(n_batch + 1))
    seg_of_tile = lambda i: jnp.minimum(i * ROW_TILE // t_lat, n_batch)

    c8 = jnp.zeros((8, d), F32).at[:n_batch].set(c).at[n_batch].set(c_ctx)
    mod_all = _modulation(c8, w_mod, b_mod)
    mod_all = mod_all.reshape(depth, 8, 6, d)[:, :n_batch + 1]
    mod_all = jnp.pad(mod_all, ((0, 0), (0, 0), (0, 2), (0, 0)))

    w_in_p = _inproj_weight(w_in)
    assert w_in_p.shape[-1] == _IN_TOTAL
    w_in_ut = jnp.swapaxes(w_in[:, :, _IN_S5:_IN_S5 + W_GROUP], 1, 2).astype(BF16)
    gate_row = jax.vmap(lambda v: _lane_rows(v.reshape(1, -1), MISC_GATE, 1))(ml_gate_bias)
    gate_col = jnp.swapaxes(gate_row, 1, 2)
    ssd_a = -jnp.exp(ssd_a_log)
    ssd_row = jax.vmap(lambda bvec, avec: _lane_rows(jnp.stack([bvec.reshape(-1), avec.reshape(-1)]), MISC_DT))(
        ssd_dt_bias, ssd_a)
    ssd_col = jnp.swapaxes(ssd_row, 1, 2)
    ssd_cw = jnp.pad(ssd_conv_w, ((0, 0), (0, 8 - ssd_conv_w.shape[1]), (0, 0)))
    ssd_d_row = jnp.repeat(ssd_d, SSD_HEADDIM, axis=-1)[:, None, :]
    ffn_cw = jnp.pad(ffn_conv_w, ((0, 0), (0, 8 - ffn_conv_w.shape[1]), (0, 0)))
    cos_l, sin_l, cos_c, sin_c = _rope_tables(t_lat, n_batch * t_ctx)
    cos = jnp.concatenate([cos_l] * n_batch + [cos_c], axis=0)
    sin = jnp.concatenate([sin_l] * n_batch + [sin_c], axis=0)
    toep, wz, wy, s5_decay = _s5_tables(s5_a_re, s5_a_im, s5_log_dt, s5_b_re, s5_b_im, s5_c_re, s5_c_im)
    s5_decay = s5_decay.reshape(depth, S5_NGROUPS, 8, LANES)
    s5_ca = s5_decay[:, :, 0:2].reshape(depth, 2 * S5_NGROUPS, LANES)
    s5_cb = s5_decay[:, :, 2:4].reshape(depth, 2 * S5_NGROUPS, LANES)
    n_s5_ctx = t_ctx // S5_CHUNK
    n_s5_lat = t_lat // S5_CHUNK

    xf = jnp.concatenate([x.reshape(n_lat, d), ctx.reshape(n_batch * t_ctx, d)], axis=0)
    for l in range(depth):
        mod = mod_all[l]
        q, k, v1, og, misc, mlac, z, xbc, ut = _in_projection(xf, mod, norm1[l][None], w_in_p[l], w_in_ut[l],
                                                               seg_of_tile)
        hf, hb = _mlstm(q, k, v1, misc, gate_row[l], gate_col[l], n_batch, n_scan)
        wqa, wqb, wka, wvt, ea, eb = _mla_weights(mla_w_uq[l], mla_w_ukv[l])
        qa, ka, vta = _mla_prep(mlac, misc, cos, sin, mla_q_norm[l][None], mla_kv_norm[l][None],
                                wqa, wqb, wka, wvt, ea, eb)
        att = _attention(qa, ka, vta, n_batch, t_lat, t_ctx)
        act, xs = _ssd_prep(xbc, ssd_cw[l], ssd_conv_b[l][None], bounds)
        yf, yb = _ssd_scan(act, misc, ssd_row[l], ssd_col[l], n_batch, n_scan)
        y5 = _s5_mix(ut, toep, wz, wy, s5_ca[l], s5_cb[l], l, n_batch, n_s5_lat, n_s5_ctx)
        row_acts = (hf, hb, og, yf, yb, xs, z)
        col_acts = (att, y5, ut)
        params = (ml_norm[l][None], ssd_d_row[l], ssd_norm[l][None], s5_d[l][:, None],
                  s5_w_glu[l].T.astype(BF16), w_out[l].astype(BF16), norm2[l][None])
        xf, h2 = _out_projection(xf, mod, seg_of_tile, row_acts, col_acts, params)
        xf = _conv_ffn(xf, mod, seg_of_tile, h2, ffn_w_up[l].astype(BF16), ffn_cw[l],
                       ffn_w_down[l].astype(BF16), bounds)
    out = _final_norm(xf, final_norm[None], n_lat)
    return out.reshape(n_batch, t_lat, d)
```

```python
import functools
import math

import numpy as np
import jax
import jax.numpy as jnp
from jax import lax
from jax.experimental import pallas as pl
from jax.experimental.pallas import tpu as pltpu

F32 = jnp.float32
BF16 = jnp.bfloat16
HI = lax.Precision.HIGHEST

D_MODEL = 1024
W_GROUP = 256
EPS = 1e-6
GRID_W = 64
ROPE_BASE = 10000.0

ML_HEADS = 4
ML_DH = 64

MLA_HEADS = 4
MLA_NOPE = 64
MLA_ROPE = 32
MLA_V = 64
MLA_Q_LORA = 256
MLA_KV_LORA = 128

SSD_HEADS = 4
SSD_HEADDIM = 64
SSD_GROUPS = 2
SSD_STATE = 128

S5_GROUP = 16
S5_NGROUPS = 16
S5_STATE = 64
S5_CHUNK = 64
S5_SW = 4 * S5_STATE

D_FF = 2816
FF_CHUNK = 256

ROW_TILE = 512
SCAN_CHUNK = 256
ATTN_TQ = 512
ATTN_TK = 512
HALO = 16
LANES = 128
VMEM_LIMIT = 56 * 1024 * 1024

MISC_GATE = 0
MISC_DT = 16
MISC_KR = 24


def _cparams(sem):
    return pltpu.CompilerParams(dimension_semantics=sem, vmem_limit_bytes=VMEM_LIMIT)


def _dot(a, b):
    return jnp.dot(a, b, preferred_element_type=F32)


def _dot_nt(a, b):
    return lax.dot_general(a, b, (((1,), (1,)), ((), ())), preferred_element_type=F32)


def _dot_tn(a, b):
    return lax.dot_general(a, b, (((0,), (0,)), ((), ())), preferred_element_type=F32)


def _dot_hi(a, b):
    return jnp.dot(a, b, preferred_element_type=F32, precision=HI)


def _sigmoid(x):
    return 1.0 / (1.0 + jnp.exp(-x))


def _silu(x):
    return x * _sigmoid(x)


def _log_sigmoid(x):
    return jnp.minimum(x, 0.0) - jnp.log(1.0 + jnp.exp(-jnp.abs(x)))


def _softplus(x):
    return jnp.maximum(x, 0.0) + jnp.log(1.0 + jnp.exp(-jnp.abs(x)))


def _gelu_tanh(x):
    return 0.5 * x * (1.0 + jnp.tanh(math.sqrt(2.0 / math.pi) * (x + 0.044715 * x * x * x)))


def _rms(x, g):
    return x * lax.rsqrt(jnp.mean(x * x, axis=-1, keepdims=True) + EPS) * g


def _segment_masks(row0, n_rows, bounds):
    r = row0 + lax.broadcasted_iota(jnp.int32, (n_rows, 1), 0)
    first = r == bounds[0]
    last = r == bounds[1] - 1
    for s in bounds[1:-1]:
        first = first | (r == s)
    for e in bounds[2:]:
        last = last | (r == e - 1)
    return first, last


def _mod_kernel(c_ref, w_ref, b_ref, o_ref):
    c = c_ref[...]
    s = _silu(c).astype(BF16)
    o_ref[...] = _dot(s, w_ref[...].astype(BF16)) + b_ref[...]


def _modulation(c8, w_mod, b_mod):
    depth, d, d6 = w_mod.shape
    return pl.pallas_call(
        _mod_kernel,
        out_shape=jax.ShapeDtypeStruct((depth, 8, d6), F32),
        grid=(depth, d6 // d),
        in_specs=[
            pl.BlockSpec((8, d), lambda l, j: (0, 0)),
            pl.BlockSpec((None, d, d), lambda l, j: (l, 0, j)),
            pl.BlockSpec((None, 1, d), lambda l, j: (l, 0, j)),
        ],
        out_specs=pl.BlockSpec((None, 8, d), lambda l, j: (l, 0, j)),
        compiler_params=_cparams(("arbitrary", "arbitrary")),
        name="modulation",
    )(c8, w_mod, b_mod.reshape(depth, 1, d6))


_IN_SPLITS = (("q", 256, BF16), ("k", 256, BF16), ("misc", 128, F32), ("mlac", 384, F32), ("xbc", 768, F32))
_IN_TOTAL = sum(s[1] for s in _IN_SPLITS)
_IN_T_SPLITS = (("v1t", ML_HEADS * LANES, BF16), ("ogt", 256, F32), ("zt", 256, F32), ("ut", 256, F32))
_IN_T_TOTAL = sum(s[1] for s in _IN_T_SPLITS)


def _inproj_weights(w_in):
    ml, mla, ssd, s5 = 0, 1040, 1456, 2488
    cols = lambda a, b: w_in[..., a:b]
    zeros = lambda n: jnp.zeros(w_in.shape[:-1] + (n,), w_in.dtype)
    tok = [cols(ml, ml + 256), cols(ml + 256, ml + 512) * ML_DH ** -0.5]
    tok += [cols(ml + 1024, ml + 1040), cols(ssd + 1024, ssd + 1032),
            cols(mla + 384, mla + 416), zeros(LANES - 56)]
    tok += [cols(mla, mla + 384), cols(ssd + 256, ssd + 1024)]
    chan = []
    for h in range(ML_HEADS):
        chan += [cols(ml + 512 + h * 64, ml + 512 + (h + 1) * 64), zeros(64)]
    chan += [cols(ml + 768, ml + 1024), cols(ssd, ssd + 256), cols(s5, s5 + 256)]
    w_tok = jnp.concatenate(tok, axis=-1).astype(BF16)
    w_chan = jnp.swapaxes(jnp.concatenate(chan, axis=-1), -1, -2).astype(BF16)
    return w_tok, w_chan


def _inproj_kernel(x_ref, mod_ref, g_ref, w_ref, wt_ref, q_ref, k_ref, misc_ref, mlac_ref, xbc_ref,
                   v1t_ref, ogt_ref, zt_ref, ut_ref):
    x = x_ref[...]
    h = _rms(x, g_ref[...]) * (1.0 + mod_ref[1:2, :]) + mod_ref[0:1, :]
    hb = h.astype(BF16)
    off = 0
    for (_, width, _), o_ref in zip(_IN_SPLITS, (q_ref, k_ref, misc_ref, mlac_ref, xbc_ref)):
        o_ref[...] = _dot(hb, w_ref[:, off:off + width]).astype(o_ref.dtype)
        off += width
    off = 0
    for (name, width, _), o_ref in zip(_IN_T_SPLITS, (v1t_ref, ogt_ref, zt_ref, ut_ref)):
        y = _dot_nt(wt_ref[off:off + width, :], hb)
        if name == "v1t":
            row = lax.broadcasted_iota(jnp.int32, (width, 1), 0)
            y = y + jnp.where((row & (LANES - 1)) == ML_DH, 1.0, 0.0)
        o_ref[...] = y.astype(o_ref.dtype)
        off += width


def _in_projection(x, mod, g, w, wt, layer, seg_of_tile):
    r, d = x.shape
    row = lambda i: (i, 0)
    return pl.pallas_call(
        _inproj_kernel,
        out_shape=[jax.ShapeDtypeStruct((r, width), dt) for _, width, dt in _IN_SPLITS]
        + [jax.ShapeDtypeStruct((width, r), dt) for _, width, dt in _IN_T_SPLITS],
        grid=(r // ROW_TILE,),
        in_specs=[
            pl.BlockSpec((ROW_TILE, d), row),
            pl.BlockSpec((None, 8, d), lambda i: (seg_of_tile(i), 0, 0)),
            pl.BlockSpec((1, d), lambda i: (0, 0)),
            pl.BlockSpec((None, d, _IN_TOTAL), lambda i: (layer, 0, 0)),
            pl.BlockSpec((None, _IN_T_TOTAL, d), lambda i: (layer, 0, 0)),
        ],
        out_specs=[pl.BlockSpec((ROW_TILE, width), row) for _, width, _ in _IN_SPLITS]
        + [pl.BlockSpec((width, ROW_TILE), lambda i: (0, i)) for _, width, _ in _IN_T_SPLITS],
        compiler_params=_cparams(("arbitrary",)),
        name="in_projection",
    )(x, mod, g, w, wt)


def _tri_masks(n):
    row = lax.broadcasted_iota(jnp.int32, (n, n), 0)
    col = lax.broadcasted_iota(jnp.int32, (n, n), 1)
    return col <= row, col >= row


def _col_forms(rows16, n):
    return jnp.concatenate([rows16, jnp.zeros((LANES - rows16.shape[0], n), F32)], axis=0).T


def _mlstm_kernel(qf_ref, kf_ref, vf_ref, mf_ref, qb_ref, kb_ref, vb_ref, mb_ref,
                  gbc_ref, hf_ref, hb_ref, st_ref, m_ref):
    n = qf_ref.shape[0]

    @pl.when(pl.program_id(1) == 0)
    def _():
        st_ref[...] = jnp.zeros_like(st_ref)
        m_ref[...] = jnp.zeros_like(m_ref)

    tril, triu = _tri_masks(n)
    dirs = ((qf_ref, kf_ref, vf_ref, mf_ref, hf_ref, triu, n - 1),
            (qb_ref, kb_ref, vb_ref, mb_ref, hb_ref, tril, 0))
    for d, (q_ref, k_ref, vt_ref, misc_ref, out_ref, mask, last) in enumerate(dirs):
        gt = misc_ref[...].T + gbc_ref[...]
        brow = _dot_hi(_log_sigmoid(gt[0:16, :]), mask.astype(F32))
        ccol = _col_forms(gt[0:16, :] - pltpu.roll(brow, 16 - ML_HEADS, 0), n)
        for h in range(ML_HEADS):
            hd = d * ML_HEADS + h
            i_idx = MISC_GATE + 8 * d + h
            f_idx = i_idx + ML_HEADS
            ig_row = gt[i_idx:i_idx + 1, :]
            b_row = brow[f_idx:f_idx + 1, :]
            c_col = ccol[:, i_idx:i_idx + 1]
            b_last = b_row[:, last:last + 1]
            m_prev = m_ref[hd:hd + 1, 0:1]
            q = q_ref[:, h * ML_DH:(h + 1) * ML_DH]
            k = k_ref[:, h * ML_DH:(h + 1) * ML_DH]
            v1t = vt_ref[h * LANES:(h + 1) * LANES, :]
            state = st_ref[hd]

            dmat = jnp.where(mask, b_row + c_col, -jnp.inf)
            inter = b_row + m_prev
            m_t = jnp.maximum(inter, jnp.max(dmat, axis=0, keepdims=True))
            w_inter = jnp.exp(inter - m_t)
            s = _dot_nt(k, q) * jnp.exp(dmat - m_t)
            tot = _dot(v1t, s.astype(BF16)) + w_inter * _dot_nt(state.astype(BF16), q)
            den = tot[ML_DH:ML_DH + 1, :]
            out_ref[h * ML_DH:(h + 1) * ML_DH, :] = tot[0:ML_DH, :] / jnp.maximum(jnp.abs(den), jnp.exp(-m_t))

            w_log = b_last - b_row + ig_row
            m_new = jnp.maximum(b_last + m_prev, jnp.max(w_log, axis=1, keepdims=True))
            decay = jnp.exp(b_last + m_prev - m_new)
            vw = (v1t.astype(F32) * jnp.exp(w_log - m_new)).astype(BF16)
            st_ref[hd] = decay * state + _dot(vw, k)
            m_ref[hd:hd + 1, :] = jnp.broadcast_to(m_new, (1, LANES))


def _scan_index_maps(n_batch, n_chunks):
    ctx0 = n_batch * n_chunks
    fwd = lambda b, s: (jnp.where(s == 0, ctx0 + b, b * n_chunks + s - 1), 0)
    bwd = lambda b, s: (jnp.where(s == 0, ctx0 + b, b * n_chunks + n_chunks - s), 0)
    return fwd, bwd


def _mlstm(q, k, v1t, misc, gb_col, n_batch, n_chunks):
    r = q.shape[0]
    n = SCAN_CHUNK
    fwd, bwd = _scan_index_maps(n_batch, n_chunks)
    const = lambda b, s: (0, 0)
    specs = []
    for imap in (fwd, bwd):
        cols = lambda b, s, imap=imap: imap(b, s)[::-1]
        specs += [pl.BlockSpec((n, W_GROUP), imap), pl.BlockSpec((n, W_GROUP), imap),
                  pl.BlockSpec((ML_HEADS * LANES, n), cols), pl.BlockSpec((n, LANES), imap)]
    specs += [pl.BlockSpec((LANES, 1), const)]
    fwd_c = lambda b, s: fwd(b, s)[::-1]
    bwd_c = lambda b, s: bwd(b, s)[::-1]
    return pl.pallas_call(
        _mlstm_kernel,
        out_shape=[jax.ShapeDtypeStruct((W_GROUP, r), F32)] * 2,
        grid=(n_batch, n_chunks + 1),
        in_specs=specs,
        out_specs=[pl.BlockSpec((W_GROUP, n), fwd_c), pl.BlockSpec((W_GROUP, n), bwd_c)],
        scratch_shapes=[pltpu.VMEM((2 * ML_HEADS, LANES, ML_DH), F32),
                        pltpu.VMEM((2 * ML_HEADS, LANES), F32)],
        compiler_params=_cparams(("arbitrary", "arbitrary")),
        name="mlstm_scan",
    )(q, k, v1t, misc, q, k, v1t, misc, gb_col)


def _ssd_prep_kernel(x_ref, xp_ref, xn_ref, w_ref, b_ref, act_ref, xs_ref, *, bounds):
    tm = x_ref.shape[0]
    first, last = _segment_masks(pl.program_id(0) * tm, tm, bounds)
    x = x_ref[...]
    rid = lax.broadcasted_iota(jnp.int32, (tm, 1), 0)
    xp = jnp.where(rid == 0, xp_ref[HALO - 1:HALO, :], pltpu.roll(x, 1, 0))
    xn = jnp.where(rid == tm - 1, xn_ref[0:1, :], pltpu.roll(x, tm - 1, 0))
    xp = jnp.where(first, 0.0, xp)
    xn = jnp.where(last, 0.0, xn)
    y = _silu(w_ref[0:1, :] * xp + w_ref[1:2, :] * x + w_ref[2:3, :] * xn + b_ref[...])
    act_ref[...] = y.astype(act_ref.dtype)
    xs_ref[...] = y[:, 0:W_GROUP].T


def _halo_specs(width, n_rows):
    per = ROW_TILE // HALO
    n_blocks = n_rows // HALO
    prev = pl.BlockSpec((HALO, width), lambda i: (jnp.maximum(i * per - 1, 0), 0))
    nxt = pl.BlockSpec((HALO, width), lambda i: (jnp.minimum((i + 1) * per, n_blocks - 1), 0))
    return prev, nxt


def _ssd_prep(xbc, conv_w, conv_b, bounds):
    r, width = xbc.shape
    row = lambda i: (i, 0)
    prev, nxt = _halo_specs(width, r)
    return pl.pallas_call(
        functools.partial(_ssd_prep_kernel, bounds=bounds),
        out_shape=[jax.ShapeDtypeStruct((r, width), BF16), jax.ShapeDtypeStruct((W_GROUP, r), F32)],
        grid=(r // ROW_TILE,),
        in_specs=[pl.BlockSpec((ROW_TILE, width), row), prev, nxt,
                  pl.BlockSpec((8, width), lambda i: (0, 0)),
                  pl.BlockSpec((1, width), lambda i: (0, 0))],
        out_specs=[pl.BlockSpec((ROW_TILE, width), row), pl.BlockSpec((W_GROUP, ROW_TILE), lambda i: (0, i))],
        compiler_params=_cparams(("arbitrary",)),
        name="ssd_conv",
    )(xbc, xbc, xbc, conv_w, conv_b)


def _ssd_kernel(af_ref, xf_ref, mf_ref, ab_ref, xb_ref, mb_ref, pc_ref, yf_ref, yb_ref, st_ref):
    n = af_ref.shape[0]

    @pl.when(pl.program_id(1) == 0)
    def _():
        st_ref[...] = jnp.zeros_like(st_ref)

    tril, triu = _tri_masks(n)
    dirs = ((af_ref, xf_ref, mf_ref, yf_ref, triu, n - 1),
            (ab_ref, xb_ref, mb_ref, yb_ref, tril, 0))
    for d, (act_ref, xt_ref, misc_ref, out_ref, mask, last) in enumerate(dirs):
        dt_r = _softplus(misc_ref[...].T + pc_ref[:, 0:1])
        cs_r = _dot_hi((dt_r * pc_ref[:, 1:2])[0:32, :], mask.astype(F32))
        ncs_c = _col_forms(-cs_r, n)
        for g in range(SSD_GROUPS):
            bm = act_ref[:, W_GROUP + g * SSD_STATE:W_GROUP + (g + 1) * SSD_STATE]
            cm = act_ref[:, W_GROUP + SSD_GROUPS * SSD_STATE + g * SSD_STATE:
                         W_GROUP + SSD_GROUPS * SSD_STATE + (g + 1) * SSD_STATE]
            gmat = _dot_nt(bm, cm)
            for hh in range(SSD_HEADS // SSD_GROUPS):
                h = g * (SSD_HEADS // SSD_GROUPS) + hh
                hd = d * SSD_HEADS + h
                idx = MISC_DT + d * SSD_HEADS + h
                cs_row = cs_r[idx:idx + 1, :]
                ncs_col = ncs_c[:, idx:idx + 1]
                cs_last = cs_row[:, last:last + 1]
                dt_row = dt_r[idx:idx + 1, :]
                xt = xt_ref[h * SSD_HEADDIM:(h + 1) * SSD_HEADDIM, :]
                state = st_ref[hd]
                decay = jnp.exp(jnp.where(mask, cs_row + ncs_col, -jnp.inf))
                scores = (gmat * decay).astype(BF16)
                y = (_dot((xt * dt_row).astype(BF16), scores)
                     + jnp.exp(cs_row) * _dot_nt(state.astype(BF16), cm))
                out_ref[h * SSD_HEADDIM:(h + 1) * SSD_HEADDIM, :] = y
                xw = (xt * (jnp.exp(cs_last - cs_row) * dt_row)).astype(BF16)
                st_ref[hd] = jnp.exp(cs_last) * state + _dot(xw, bm)


def _ssd_scan(act, xst, misc, p_col, n_batch, n_chunks):
    r, width = act.shape
    n = SCAN_CHUNK
    fwd, bwd = _scan_index_maps(n_batch, n_chunks)
    const = lambda b, s: (0, 0)
    fwd_c = lambda b, s: fwd(b, s)[::-1]
    bwd_c = lambda b, s: bwd(b, s)[::-1]
    specs = []
    for imap, cmap in ((fwd, fwd_c), (bwd, bwd_c)):
        specs += [pl.BlockSpec((n, width), imap), pl.BlockSpec((W_GROUP, n), cmap), pl.BlockSpec((n, LANES), imap)]
    specs += [pl.BlockSpec((LANES, 8), const)]
    return pl.pallas_call(
        _ssd_kernel,
        out_shape=[jax.ShapeDtypeStruct((W_GROUP, r), F32)] * 2,
        grid=(n_batch, n_chunks + 1),
        in_specs=specs,
        out_specs=[pl.BlockSpec((W_GROUP, n), fwd_c), pl.BlockSpec((W_GROUP, n), bwd_c)],
        scratch_shapes=[pltpu.VMEM((2 * SSD_HEADS, SSD_HEADDIM, SSD_STATE), F32)],
        compiler_params=_cparams(("arbitrary", "arbitrary")),
        name="ssd_scan",
    )(act, xst, misc, act, xst, misc, p_col)


def _mla_prep_kernel(c_ref, misc_ref, cos_ref, sin_ref, qn_ref, kvn_ref, wqa_ref, wqb_ref,
                     wka_ref, wvt_ref, ea_ref, eb_ref, q_ref, k_ref, vt_ref):
    scale = (MLA_NOPE + MLA_ROPE) ** -0.5 * math.log2(math.e)
    cq = _rms(c_ref[:, 0:MLA_Q_LORA], qn_ref[...]).astype(BF16)
    ckv = _rms(c_ref[:, MLA_Q_LORA:MLA_Q_LORA + MLA_KV_LORA], kvn_ref[...]).astype(BF16)
    cos = cos_ref[...]
    sin = sin_ref[...]
    qa = _dot(cq, wqa_ref[...])
    qb = _dot(cq, wqb_ref[...])
    ka = _dot(ckv, wka_ref[...])
    misc = misc_ref[...]
    kr = _dot_hi(misc, ea_ref[...]) * cos + _dot_hi(misc, eb_ref[...]) * sin
    for h in range(MLA_HEADS):
        sl = slice(h * LANES, (h + 1) * LANES)
        q_ref[:, sl] = ((qa[:, sl] * cos + qb[:, sl] * sin) * scale).astype(BF16)
        k_ref[:, sl] = (ka[:, sl] + kr).astype(BF16)
    vt = _dot_nt(wvt_ref[...], ckv)
    row = lax.broadcasted_iota(jnp.int32, (vt.shape[0], 1), 0)
    vt_ref[...] = (vt + jnp.where((row & (LANES - 1)) == MLA_V, 1.0, 0.0)).astype(BF16)


def _mla_prep(mlac, misc, cos, sin, qn, kvn, wqa, wqb, wka, wvt, ea, eb):
    r = mlac.shape[0]
    row = lambda i: (i, 0)
    full = lambda a: pl.BlockSpec(a.shape, lambda i: (0, 0))
    hw = MLA_HEADS * LANES
    return pl.pallas_call(
        _mla_prep_kernel,
        out_shape=[jax.ShapeDtypeStruct((r, hw), BF16)] * 2 + [jax.ShapeDtypeStruct((hw, r), BF16)],
        grid=(r // ROW_TILE,),
        in_specs=[pl.BlockSpec((ROW_TILE, mlac.shape[1]), row), pl.BlockSpec((ROW_TILE, LANES), row),
                  pl.BlockSpec((ROW_TILE, LANES), row), pl.BlockSpec((ROW_TILE, LANES), row),
                  full(qn), full(kvn), full(wqa), full(wqb), full(wka), full(wvt), full(ea), full(eb)],
        out_specs=[pl.BlockSpec((ROW_TILE, hw), row)] * 2 + [pl.BlockSpec((hw, ROW_TILE), lambda i: (0, i))],
        compiler_params=_cparams(("arbitrary",)),
        name="mla_prep",
    )(mlac, misc, cos, sin, qn, kvn, wqa, wqb, wka, wvt, ea, eb)


_HEAD_LANES = tuple(slice(h * LANES, (h + 1) * LANES) for h in range(MLA_HEADS))


def _attn_scores(q_ref, k_ref, rows, n_keys, s_ref):
    for h, sl in enumerate(_HEAD_LANES):
        s_ref[h, 0:n_keys, :] = _dot_nt(k_ref[rows, sl], q_ref[:, sl])


def _attn_update(s_ref, n_keys, vt_ref, cols, m_ref, acc_ref):
    new = []
    for h, sl in enumerate(_HEAD_LANES):
        st = s_ref[h, 0:n_keys, :]
        m_old = m_ref[h]
        m_new = jnp.maximum(m_old, jnp.max(st, axis=0, keepdims=True))
        p = jnp.exp2(st - m_new).astype(BF16)
        new.append((m_new, jnp.exp2(m_old - m_new) * acc_ref[h] + _dot(vt_ref[sl, cols], p)))
    for h, (m_new, acc) in enumerate(new):
        m_ref[h] = m_new
        acc_ref[h] = acc


def _attn_init(m_ref, acc_ref):
    m_ref[...] = jnp.full_like(m_ref, -jnp.inf)
    acc_ref[...] = jnp.zeros_like(acc_ref)


def _attn_finish(o_ref, acc_ref):
    for h in range(MLA_HEADS):
        acc = acc_ref[h]
        o_ref[h * MLA_V:(h + 1) * MLA_V, :] = (acc[0:MLA_V] / acc[MLA_V:MLA_V + 1]).astype(o_ref.dtype)


def _attn_lat_kernel(q_ref, kl_ref, vlt_ref, kc_ref, vct_ref, o_ref, m_ref, acc_ref, sa_ref, sb_ref):
    tk = ATTN_TK
    n_pairs = kl_ref.shape[0] // (2 * tk)
    n_ctx = kc_ref.shape[0]
    chunk = lambda c: pl.ds(pl.multiple_of(c * tk, tk), tk)
    _attn_init(m_ref, acc_ref)
    _attn_scores(q_ref, kl_ref, chunk(0), tk, sa_ref)

    def body(i, carry):
        c = 2 * i
        _attn_scores(q_ref, kl_ref, chunk(c + 1), tk, sb_ref)
        _attn_update(sa_ref, tk, vlt_ref, chunk(c), m_ref, acc_ref)
        _attn_scores(q_ref, kl_ref, chunk(c + 2), tk, sa_ref)
        _attn_update(sb_ref, tk, vlt_ref, chunk(c + 1), m_ref, acc_ref)
        return carry

    lax.fori_loop(0, n_pairs - 1, body, 0)
    c = 2 * (n_pairs - 1)
    _attn_scores(q_ref, kl_ref, chunk(c + 1), tk, sb_ref)
    _attn_update(sa_ref, tk, vlt_ref, chunk(c), m_ref, acc_ref)
    _attn_scores(q_ref, kc_ref, slice(None), n_ctx, sa_ref)
    _attn_update(sb_ref, tk, vlt_ref, chunk(c + 1), m_ref, acc_ref)
    _attn_update(sa_ref, n_ctx, vct_ref, slice(None), m_ref, acc_ref)
    _attn_finish(o_ref, acc_ref)


def _attn_ctx_kernel(q_ref, kc_ref, vct_ref, prev_ref, o_ref, m_ref, acc_ref, s_ref):
    del prev_ref
    _attn_init(m_ref, acc_ref)
    _attn_scores(q_ref, kc_ref, slice(None), kc_ref.shape[0], s_ref)
    _attn_update(s_ref, kc_ref.shape[0], vct_ref, slice(None), m_ref, acc_ref)
    _attn_finish(o_ref, acc_ref)


def _attention(q, k, vt, n_batch, t_lat, t_ctx):
    r, hw = q.shape
    nq = t_lat // ATTN_TQ
    ctx_blk = n_batch * t_lat // t_ctx
    assert t_lat % (2 * ATTN_TK) == 0 and t_ctx <= ATTN_TK
    scratch = lambda tq: [pltpu.VMEM((MLA_HEADS, 1, tq), F32), pltpu.VMEM((MLA_HEADS, LANES, tq), F32)]
    scores = lambda tk, tq: pltpu.VMEM((MLA_HEADS, tk, tq), F32)
    lat = pl.pallas_call(
        _attn_lat_kernel,
        out_shape=jax.ShapeDtypeStruct((W_GROUP, r), BF16),
        grid=(n_batch, nq),
        in_specs=[pl.BlockSpec((ATTN_TQ, hw), lambda b, i: (b * nq + i, 0)),
                  pl.BlockSpec((t_lat, hw), lambda b, i: (b, 0)),
                  pl.BlockSpec((hw, t_lat), lambda b, i: (0, b)),
                  pl.BlockSpec((t_ctx, hw), lambda b, i: (ctx_blk + b, 0)),
                  pl.BlockSpec((hw, t_ctx), lambda b, i: (0, ctx_blk + b))],
        out_specs=pl.BlockSpec((W_GROUP, ATTN_TQ), lambda b, i: (0, b * nq + i)),
        scratch_shapes=scratch(ATTN_TQ) + [scores(ATTN_TK, ATTN_TQ)] * 2,
        compiler_params=_cparams(("arbitrary", "arbitrary")),
        name="mla_attention_latent",
    )(q, k, vt, k, vt)
    return pl.pallas_call(
        _attn_ctx_kernel,
        out_shape=jax.ShapeDtypeStruct((W_GROUP, r), BF16),
        grid=(n_batch,),
        in_specs=[pl.BlockSpec((t_ctx, hw), lambda b: (ctx_blk + b, 0)),
                  pl.BlockSpec((t_ctx, hw), lambda b: (ctx_blk + b, 0)),
                  pl.BlockSpec((hw, t_ctx), lambda b: (0, ctx_blk + b)),
                  pl.BlockSpec(memory_space=pl.ANY)],
        out_specs=pl.BlockSpec((W_GROUP, t_ctx), lambda b: (0, ctx_blk + b)),
        scratch_shapes=scratch(t_ctx) + [scores(t_ctx, t_ctx)],
        input_output_aliases={3: 0},
        compiler_params=_cparams(("arbitrary",)),
        name="mla_attention_context",
    )(q, k, vt, lat)


def _cis_pow(lr, li, dt, k):
    mag = jnp.exp(lr * dt * k)
    ang = li * dt * k
    return mag * jnp.cos(ang), mag * jnp.sin(ang)


def _s5_tables_kernel(pr_ref, pc_ref, btre_ref, btim_ref, ctre_ref, ctim_ref,
                      toep_ref, wz_ref, wy_ref, decay_ref):
    lc, ng, ns = S5_CHUNK, S5_GROUP, S5_STATE
    lr_r = jnp.minimum(pr_ref[0:1, :], -1e-4)
    li_r = pr_ref[1:2, :]
    dt_r = jnp.exp(pr_ref[2:3, :])
    lr_c = jnp.minimum(pc_ref[:, 0:1], -1e-4)
    li_c = pc_ref[:, 1:2]
    dt_c = jnp.exp(pc_ref[:, 2:3])
    lane4 = lax.broadcasted_iota(jnp.int32, (1, S5_SW), 1)
    odd_lane = (lane4 & ns) != 0
    row4 = lax.broadcasted_iota(jnp.int32, (S5_SW, 1), 0)
    odd_row = (row4 & ns) != 0

    ab_re, ab_im = _cis_pow(lr_r, li_r, dt_r, 1.0)
    den = lr_r * lr_r + li_r * li_r
    f_re = ((ab_re - 1.0) * lr_r + ab_im * li_r) / den
    f_im = (ab_im * lr_r - (ab_re - 1.0) * li_r) / den
    bb_re = f_re * btre_ref[...] - f_im * btim_ref[...]
    bb_im = f_re * btim_ref[...] + f_im * btre_ref[...]

    s_col = lax.broadcasted_iota(jnp.int32, (lc, 1), 0)
    expo = jnp.where(lane4 < 2 * ns, lc - 1 - s_col, s_col).astype(F32)
    pz_re, pz_im = _cis_pow(lr_r, li_r, dt_r, expo)
    xz = jnp.where(odd_lane, bb_im, bb_re)
    yz = jnp.where(odd_lane, bb_re, -bb_im)
    for j in range(ng):
        wz_ref[j] = (pz_re * xz[j:j + 1, :] + pz_im * yz[j:j + 1, :]).astype(BF16)

    al_re, al_im = _cis_pow(lr_r, li_r, dt_r, float(lc))
    al_sw = jnp.where(odd_lane, al_im, -al_im)
    decay_ref[...] = jnp.zeros_like(decay_ref)
    for d in range(2):
        decay_ref[d:d + 1, :] = al_re[:, d * LANES:(d + 1) * LANES]
        decay_ref[2 + d:3 + d, :] = al_sw[:, d * LANES:(d + 1) * LANES]

    t_lane = lax.broadcasted_iota(jnp.int32, (1, LANES), 1) & (lc - 1)
    expo = jnp.where(row4 < 2 * ns, t_lane + 1, lc - t_lane).astype(F32)
    py_re, py_im = _cis_pow(lr_c, li_c, dt_c, expo)
    reps = lc * ng // LANES
    py_re = jnp.tile(py_re, (1, reps))
    py_im = jnp.tile(py_im, (1, reps))
    low_half = lax.broadcasted_iota(jnp.int32, (1, 2 * lc), 1) < lc

    def expand_half(c):
        bc = lambda j: jnp.broadcast_to(c[:, j:j + 1], (c.shape[0], 2 * lc))
        return jnp.concatenate([jnp.where(low_half, bc(2 * p), bc(2 * p + 1)) for p in range(ng // 2)], axis=1)

    def expand_full(c):
        return jnp.concatenate([jnp.broadcast_to(c[:, j:j + 1], (c.shape[0], 2 * lc)) for j in range(ng)], axis=1)

    ce_re = expand_half(ctre_ref[...])
    ce_im = expand_half(ctim_ref[...])
    wy_ref[...] = jnp.where(odd_row, -(ce_re * py_im + ce_im * py_re),
                            ce_re * py_re - ce_im * py_im).astype(BF16)

    m_lane = lax.broadcasted_iota(jnp.int32, (1, 2 * lc), 1)
    xmat = None
    for d in range(2):
        rows = slice(d * 2 * ns, d * 2 * ns + ns)
        lag = m_lane - (lc - 1) if d == 0 else (lc - 1) - m_lane
        valid = (lag >= 0) & (m_lane < 2 * lc - 1)
        pm_re, pm_im = _cis_pow(lr_c[rows], li_c[rows], dt_c[rows], jnp.where(valid, lag, 0).astype(F32))
        pm_re = jnp.tile(jnp.where(valid, pm_re, 0.0), (1, ng))
        pm_im = jnp.tile(jnp.where(valid, pm_im, 0.0), (1, ng))
        c2_re = expand_full(ctre_ref[rows, :])
        c2_im = expand_full(ctim_ref[rows, :])
        p_re = c2_re * pm_re - c2_im * pm_im
        p_im = c2_re * pm_im + c2_im * pm_re
        lanes = slice(d * 2 * ns, d * 2 * ns + ns)
        term = _dot_hi(bb_re[:, lanes], p_re) - _dot_hi(bb_im[:, lanes], p_im)
        xmat = term if xmat is None else xmat + term

    for j in range(ng):
        xb = jnp.broadcast_to(xmat[j:j + 1, :], (lc, 2 * lc * ng))
        cols = []
        for p in range(ng // 2):
            even = xb[:, (2 * p) * 2 * lc:(2 * p + 1) * 2 * lc]
            odd = xb[:, (2 * p + 1) * 2 * lc:(2 * p + 2) * 2 * lc]
            cols.append(jnp.where(low_half,
                                  pltpu.roll(even, lc + 1, 1, stride=1, stride_axis=0),
                                  pltpu.roll(odd, 1, 1, stride=1, stride_axis=0)))
        toep_ref[j] = jnp.concatenate(cols, axis=1).astype(BF16)


def _s5_tables(a_re, a_im, log_dt, b_re, b_im, c_re, c_im):
    depth = a_re.shape[0]
    g, n, j, lc = S5_NGROUPS, S5_STATE, S5_GROUP, S5_CHUNK
    n_all = depth * g

    def parts(v):
        v = jnp.transpose(v, (0, 2, 1, 3))
        return jnp.concatenate([v[:, :, 0], v[:, :, 0], v[:, :, 1], v[:, :, 1]], axis=-1).reshape(n_all, S5_SW)

    rows = jnp.stack([parts(a_re), parts(a_im), parts(jnp.broadcast_to(log_dt[..., None], a_re.shape))], axis=1)
    p_row = jnp.pad(rows, ((0, 0), (0, 5), (0, 0)))
    p_col = jnp.swapaxes(p_row, 1, 2)
    bt = lambda b: jnp.tile(jnp.swapaxes(b.reshape(n_all, n, j), 1, 2), (1, 1, 4))
    ct = lambda c: jnp.tile(jnp.swapaxes(c.reshape(n_all, j, n), 1, 2), (1, 4, 1))
    own = lambda i: (i, 0, 0)
    width = lc * j
    return pl.pallas_call(
        _s5_tables_kernel,
        out_shape=[jax.ShapeDtypeStruct((n_all * j, lc, width), BF16),
                   jax.ShapeDtypeStruct((n_all * j, lc, S5_SW), BF16),
                   jax.ShapeDtypeStruct((n_all, S5_SW, width), BF16),
                   jax.ShapeDtypeStruct((n_all, 8, LANES), F32)],
        grid=(n_all,),
        in_specs=[pl.BlockSpec((None, 8, S5_SW), own), pl.BlockSpec((None, S5_SW, 8), own),
                  pl.BlockSpec((None, j, S5_SW), own), pl.BlockSpec((None, j, S5_SW), own),
                  pl.BlockSpec((None, S5_SW, j), own), pl.BlockSpec((None, S5_SW, j), own)],
        out_specs=[pl.BlockSpec((j, lc, width), own), pl.BlockSpec((j, lc, S5_SW), own),
                   pl.BlockSpec((None, S5_SW, width), own), pl.BlockSpec((None, 8, LANES), own)],
        compiler_params=_cparams(("arbitrary",)),
        name="s5_tables",
    )(p_row, p_col, bt(b_re), bt(b_im), ct(c_re), ct(c_im))


def _s5_local_kernel(u_ref, wz_ref, z_ref):
    acc = None
    for j in range(S5_GROUP):
        term = _dot(u_ref[j].astype(BF16), wz_ref[j])
        acc = term if acc is None else acc + term
    z_ref[...] = acc


def _s5_scan_kernel(z_ref, ca_ref, cb_ref, x_ref, xf_ref, xb_ref, *, n_batch, n_lat, n_ctx):
    ca = ca_ref[...]
    cb = cb_ref[...]
    zero = jnp.zeros(ca.shape, F32)
    ctx0 = n_batch * n_lat

    def step(x, z):
        return x * ca + pltpu.roll(x, LANES // 2, 1) * cb + z

    def body(i, carry):
        new = []
        for b in range(n_batch):
            rf = jnp.where(i < n_ctx, ctx0 + b * n_ctx + i, b * n_lat + i - n_ctx)
            rb = jnp.where(i < n_ctx, ctx0 + b * n_ctx + n_ctx - 1 - i, b * n_lat + n_lat - 1 - (i - n_ctx))
            xf, xb = carry[2 * b], carry[2 * b + 1]
            xf_ref[rf] = xf
            xb_ref[rb] = xb
            new += [step(xf, z_ref[rf]), step(xb, z_ref[rb])]
        return tuple(new)

    lax.fori_loop(0, n_ctx + n_lat, body, (zero,) * (2 * n_batch))
    row = lax.broadcasted_iota(jnp.int32, x_ref.shape, 1)
    x_ref[...] = jnp.where((row & 1) == 0, xf_ref[...], xb_ref[...])


def _s5_out_kernel(u_ref, t_ref, x_ref, wy_ref, y_ref):
    acc = _dot(x_ref[...].astype(BF16), wy_ref[...])
    for j in range(S5_GROUP):
        acc = acc + _dot(u_ref[j].astype(BF16), t_ref[j])
    for j in range(S5_GROUP):
        y_ref[j] = acc[:, j * S5_CHUNK:(j + 1) * S5_CHUNK]


def _s5_mix(ut, toep, wz, wy, ca, cb, layer, n_batch, n_lat, n_ctx):
    g, j, lc = S5_NGROUPS, S5_GROUP, S5_CHUNK
    nr = ut.shape[1] // lc
    u3 = ut.reshape(W_GROUP, nr, lc)
    grp = lambda i: (i, 0, 0)
    tab = lambda i: (layer * g + i, 0, 0)
    z = pl.pallas_call(
        _s5_local_kernel,
        out_shape=jax.ShapeDtypeStruct((nr, g * S5_SW), F32),
        grid=(g,),
        in_specs=[pl.BlockSpec((j, nr, lc), grp), pl.BlockSpec((j, lc, S5_SW), tab)],
        out_specs=pl.BlockSpec((nr, S5_SW), lambda i: (0, i)),
        compiler_params=_cparams(("arbitrary",)),
        name="s5_local_state",
    )(u3, wz)
    z3 = z.reshape(nr, 2 * g, LANES)
    whole = lambda shp: pl.BlockSpec(shp, lambda: tuple(0 for _ in shp))
    x3 = pl.pallas_call(
        functools.partial(_s5_scan_kernel, n_batch=n_batch, n_lat=n_lat, n_ctx=n_ctx),
        out_shape=jax.ShapeDtypeStruct(z3.shape, F32),
        in_specs=[whole(z3.shape), whole(ca.shape), whole(cb.shape)],
        out_specs=whole(z3.shape),
        scratch_shapes=[pltpu.VMEM(z3.shape, F32), pltpu.VMEM(z3.shape, F32)],
        compiler_params=pltpu.CompilerParams(vmem_limit_bytes=VMEM_LIMIT),
        name="s5_chunk_scan",
    )(z3, ca, cb)
    x = x3.reshape(nr, g * S5_SW)
    y3 = pl.pallas_call(
        _s5_out_kernel,
        out_shape=jax.ShapeDtypeStruct((W_GROUP, nr, lc), F32),
        grid=(g,),
        in_specs=[pl.BlockSpec((j, nr, lc), grp), pl.BlockSpec((j, lc, lc * j), tab),
                  pl.BlockSpec((nr, S5_SW), lambda i: (0, i)),
                  pl.BlockSpec((None, S5_SW, lc * j), tab)],
        out_specs=pl.BlockSpec((j, nr, lc), grp),
        compiler_params=_cparams(("arbitrary",)),
        name="s5_output",
    )(u3, toep, x, wy)
    return y3.reshape(W_GROUP, nr * lc)


def _outproj_kernel(x_ref, mod_ref, hf_ref, hb_ref, og_ref, yf_ref, yb_ref, xs_ref, z_ref,
                    att_ref, s5_ref, ut_ref, pcol_ref, glu_ref, wo_ref, n2_ref, xo_ref, h2_ref):
    h = hf_ref[...] + hb_ref[...]
    parts = []
    for i in range(ML_HEADS):
        hh = h[i * ML_DH:(i + 1) * ML_DH, :]
        parts.append(hh * lax.rsqrt(jnp.mean(hh * hh, axis=0, keepdims=True) + EPS))
    ya = jnp.concatenate(parts, axis=0) * pcol_ref[:, 0:1] * _sigmoid(og_ref[...])
    yc = (pcol_ref[:, 1:2] * xs_ref[...] + yf_ref[...] + yb_ref[...]) * _silu(z_ref[...])
    yc = yc * lax.rsqrt(jnp.mean(yc * yc, axis=0, keepdims=True) + EPS) * pcol_ref[:, 2:3]
    yd = _gelu_tanh(pcol_ref[:, 3:4] * ut_ref[...] + s5_ref[...])
    yd = yd * _sigmoid(_dot(glu_ref[...], yd.astype(BF16)))
    mixed = jnp.concatenate([ya.astype(BF16), att_ref[...], yc.astype(BF16), yd.astype(BF16)], axis=0)
    y = _dot_tn(mixed, wo_ref[...])
    x = x_ref[...] + mod_ref[2:3, :] * y
    xo_ref[...] = x
    h2 = _rms(x, n2_ref[...]) * (1.0 + mod_ref[4:5, :]) + mod_ref[3:4, :]
    h2_ref[...] = h2.astype(BF16)


def _out_projection(x, mod, seg_of_tile, acts, params):
    r, d = x.shape
    row = lambda i: (i, 0)
    full = lambda a: pl.BlockSpec(a.shape, lambda i: (0, 0))
    return pl.pallas_call(
        _outproj_kernel,
        out_shape=[jax.ShapeDtypeStruct((r, d), F32), jax.ShapeDtypeStruct((r, d), BF16)],
        grid=(r // ROW_TILE,),
        in_specs=[pl.BlockSpec((ROW_TILE, d), row),
                  pl.BlockSpec((None, 8, d), lambda i: (seg_of_tile(i), 0, 0))]
        + [pl.BlockSpec((W_GROUP, ROW_TILE), lambda i: (0, i)) for _ in acts]
        + [full(p) for p in params],
        out_specs=[pl.BlockSpec((ROW_TILE, d), row)] * 2,
        compiler_params=_cparams(("arbitrary",)),
        name="out_projection",
    )(x, mod, *acts, *params)


def _ffn_kernel(x_ref, mod_ref, h_ref, hp_ref, hn_ref, wup_ref, cw_ref, wdn_ref, o_ref, acc_ref, *, bounds):
    tm = h_ref.shape[0]
    first, last = _segment_masks(pl.program_id(0) * tm, tm, bounds)
    h = h_ref[...]
    h_ext = jnp.concatenate([hp_ref[...], h, hn_ref[...]], axis=0)
    n_ext = tm + 2 * HALO
    n_chunks = D_FF // FF_CHUNK

    def up(j):
        c0 = j * FF_CHUNK
        return (_dot(h, wup_ref[:, c0:c0 + FF_CHUNK]),
                _dot(h_ext, wup_ref[:, D_FF + c0:D_FF + c0 + FF_CHUNK]))

    def down(j, u, g_ext):
        c0 = j * FF_CHUNK
        gp = jnp.where(first, 0.0, pltpu.roll(g_ext, 1, 0)[HALO:HALO + tm])
        gn = jnp.where(last, 0.0, pltpu.roll(g_ext, n_ext - 1, 0)[HALO:HALO + tm])
        gc = g_ext[HALO:HALO + tm]
        cw = cw_ref[:, c0:c0 + FF_CHUNK]
        conv = cw[0:1, :] * gp + cw[1:2, :] * gc + cw[2:3, :] * gn
        act = (_silu(conv) * u).astype(BF16)
        return _dot(act, wdn_ref[c0:c0 + FF_CHUNK, :])

    nxt = up(0)
    for j in range(n_chunks):
        cur = nxt
        if j + 1 < n_chunks:
            nxt = up(j + 1)
        y = down(j, *cur)
        if j == 0:
            acc_ref[...] = y
        else:
            acc_ref[...] += y
    o_ref[...] = x_ref[...] + mod_ref[5:6, :] * acc_ref[...]


def _conv_ffn(x, mod, seg_of_tile, h2, w_up, conv_w, w_down, bounds):
    r, d = x.shape
    row = lambda i: (i, 0)
    prev, nxt = _halo_specs(d, r)
    full = lambda a: pl.BlockSpec(a.shape, lambda i: (0, 0))
    return pl.pallas_call(
        functools.partial(_ffn_kernel, bounds=bounds),
        out_shape=jax.ShapeDtypeStruct((r, d), F32),
        grid=(r // ROW_TILE,),
        in_specs=[pl.BlockSpec((ROW_TILE, d), row),
                  pl.BlockSpec((None, 8, d), lambda i: (seg_of_tile(i), 0, 0)),
                  pl.BlockSpec((ROW_TILE, d), row), prev, nxt,
                  full(w_up), full(conv_w), full(w_down)],
        out_specs=pl.BlockSpec((ROW_TILE, d), row),
        scratch_shapes=[pltpu.VMEM((ROW_TILE, d), F32)],
        compiler_params=_cparams(("arbitrary",)),
        name="conv_ffn",
    )(x, mod, h2, h2, h2, w_up, conv_w, w_down)


def _final_norm_kernel(x_ref, g_ref, o_ref):
    o_ref[...] = _rms(x_ref[...], g_ref[...])


def _final_norm(x, g, n_rows):
    d = x.shape[1]
    return pl.pallas_call(
        _final_norm_kernel,
        out_shape=jax.ShapeDtypeStruct((n_rows, d), F32),
        grid=(n_rows // ROW_TILE,),
        in_specs=[pl.BlockSpec((ROW_TILE, d), lambda i: (i, 0)), pl.BlockSpec((1, d), lambda i: (0, 0))],
        out_specs=pl.BlockSpec((ROW_TILE, d), lambda i: (i, 0)),
        compiler_params=_cparams(("arbitrary",)),
        name="final_norm",
    )(x, g)


def _rope_tables(t_lat, n_ctx_rows):
    rows = t_lat // GRID_W
    row = jnp.broadcast_to(jnp.arange(rows)[:, None], (rows, GRID_W)).reshape(-1).astype(F32)
    col = jnp.broadcast_to(jnp.arange(GRID_W)[None, :], (rows, GRID_W)).reshape(-1).astype(F32)
    n_freq = MLA_ROPE // 4
    inv = ROPE_BASE ** (-jnp.arange(n_freq, dtype=F32) / n_freq)
    ang = jnp.concatenate([row[:, None] * inv, col[:, None] * inv], axis=-1)
    half = MLA_ROPE // 2
    pad = LANES - MLA_NOPE - MLA_ROPE

    def table(t, lead):
        return jnp.concatenate([jnp.full((t_lat, MLA_NOPE), lead, F32), t, t, jnp.zeros((t_lat, pad), F32)], axis=-1)

    cos = table(jnp.cos(ang), 1.0)
    sin = table(jnp.sin(ang), 0.0)
    lane = np.arange(LANES)
    cos_ctx = jnp.broadcast_to(jnp.asarray((lane < MLA_NOPE + 2 * half).astype(np.float32)), (n_ctx_rows, LANES))
    sin_ctx = jnp.zeros((n_ctx_rows, LANES), F32)
    return cos, sin, cos_ctx, sin_ctx


def _mla_weights(w_uq, w_ukv):
    half = MLA_ROPE // 2
    qd = MLA_NOPE + MLA_ROPE
    kd = MLA_NOPE + MLA_V
    hw = MLA_HEADS * LANES
    qa_idx = np.zeros((hw,), np.int32); qa_s = np.zeros((hw,), np.float32)
    qb_idx = np.zeros((hw,), np.int32); qb_s = np.zeros((hw,), np.float32)
    ka_idx = np.zeros((hw,), np.int32); ka_s = np.zeros((hw,), np.float32)
    v_idx = np.zeros((hw,), np.int32); v_s = np.zeros((hw,), np.float32)
    for h in range(MLA_HEADS):
        for i in range(qd):
            qa_idx[h * LANES + i] = h * qd + i
            qa_s[h * LANES + i] = 1.0
        for i in range(half):
            qb_idx[h * LANES + MLA_NOPE + i] = h * qd + MLA_NOPE + half + i
            qb_s[h * LANES + MLA_NOPE + i] = -1.0
            qb_idx[h * LANES + MLA_NOPE + half + i] = h * qd + MLA_NOPE + i
            qb_s[h * LANES + MLA_NOPE + half + i] = 1.0
        for i in range(MLA_NOPE):
            ka_idx[h * LANES + i] = h * kd + i
            ka_s[h * LANES + i] = 1.0
        for i in range(MLA_V):
            v_idx[h * LANES + i] = h * kd + MLA_NOPE + i
            v_s[h * LANES + i] = 1.0
    pick = lambda w, idx, s: (w[..., idx] * s).astype(BF16)
    ea = np.zeros((LANES, LANES), np.float32)
    eb = np.zeros((LANES, LANES), np.float32)
    for i in range(MLA_ROPE):
        ea[MISC_KR + i, MLA_NOPE + i] = 1.0
    for i in range(half):
        eb[MISC_KR + half + i, MLA_NOPE + i] = -1.0
        eb[MISC_KR + i, MLA_NOPE + half + i] = 1.0
    return (pick(w_uq, qa_idx, qa_s), pick(w_uq, qb_idx, qb_s), pick(w_ukv, ka_idx, ka_s),
            pick(w_ukv, v_idx, v_s).T, jnp.asarray(ea), jnp.asarray(eb))


def _lane_rows(vals, offset, n_rows=8):
    k, n = vals.shape
    out = jnp.zeros((n_rows, LANES), F32)
    return out.at[:k, offset:offset + n].set(vals)


def kernel(x, c, ctx, c_ctx, w_mod, b_mod, norm1, norm2, w_in, ml_gate_bias, ml_norm, mla_q_norm, mla_kv_norm, mla_w_uq, mla_w_ukv, ssd_conv_w, ssd_conv_b, ssd_a_log, ssd_dt_bias, ssd_d, ssd_norm, s5_a_re, s5_a_im, s5_log_dt, s5_b_re, s5_b_im, s5_c_re, s5_c_im, s5_d, s5_w_glu, w_out, ffn_w_up, ffn_conv_w, ffn_w_down, final_norm):
    n_batch, t_lat, d = x.shape
    t_ctx = ctx.shape[1]
    depth = w_mod.shape[0]
    assert d == D_MODEL and t_ctx == SCAN_CHUNK and t_lat % ROW_TILE == 0
    assert (n_batch * t_ctx) % ROW_TILE == 0 and t_lat % ATTN_TK == 0 and t_lat % GRID_W == 0
    n_lat = n_batch * t_lat
    n_scan = t_lat // SCAN_CHUNK
    bounds = tuple(b * t_lat for b in range(n_batch)) + tuple(n_lat + b * t_ctx for b in range(n_batch + 1))
    seg_of_tile = lambda i: jnp.minimum(i * ROW_TILE // t_lat, n_batch)

    c8 = jnp.zeros((8, d), F32).at[:n_batch].set(c).at[n_batch].set(c_ctx)
    mod_all = _modulation(c8, w_mod, b_mod)
    mod_all = mod_all.reshape(depth, 8, 6, d)[:, :n_batch + 1]
    mod_all = jnp.pad(mod_all, ((0, 0), (0, 0), (0, 2), (0, 0)))

    w_in_tok, w_in_chan = _inproj_weights(w_in)
    assert w_in_tok.shape[-1] == _IN_TOTAL and w_in_chan.shape[-2] == _IN_T_TOTAL
    gate_row = jax.vmap(lambda v: _lane_rows(v.reshape(1, -1), MISC_GATE, 1))(ml_gate_bias)
    gate_col = jnp.swapaxes(gate_row, 1, 2)
    ssd_a = -jnp.exp(ssd_a_log)
    ssd_row = jax.vmap(lambda bvec, avec: _lane_rows(jnp.stack([bvec.reshape(-1), avec.reshape(-1)]), MISC_DT))(
        ssd_dt_bias, ssd_a)
    ssd_col = jnp.swapaxes(ssd_row, 1, 2)
    ssd_cw = jnp.pad(ssd_conv_w, ((0, 0), (0, 8 - ssd_conv_w.shape[1]), (0, 0)))
    mix_cols = jnp.stack([ml_norm, jnp.repeat(ssd_d, SSD_HEADDIM, axis=-1), ssd_norm, s5_d], axis=-1)
    mix_cols = jnp.pad(mix_cols, ((0, 0), (0, 0), (0, 4)))
    ffn_cw = jnp.pad(ffn_conv_w, ((0, 0), (0, 8 - ffn_conv_w.shape[1]), (0, 0)))
    cos_l, sin_l, cos_c, sin_c = _rope_tables(t_lat, n_batch * t_ctx)
    cos = jnp.concatenate([cos_l] * n_batch + [cos_c], axis=0)
    sin = jnp.concatenate([sin_l] * n_batch + [sin_c], axis=0)
    toep, wz, wy, s5_decay = _s5_tables(s5_a_re, s5_a_im, s5_log_dt, s5_b_re, s5_b_im, s5_c_re, s5_c_im)
    s5_decay = s5_decay.reshape(depth, S5_NGROUPS, 8, LANES)
    s5_ca = s5_decay[:, :, 0:2].reshape(depth, 2 * S5_NGROUPS, LANES)
    s5_cb = s5_decay[:, :, 2:4].reshape(depth, 2 * S5_NGROUPS, LANES)
    n_s5_ctx = t_ctx // S5_CHUNK
    n_s5_lat = t_lat // S5_CHUNK

    xf = jnp.concatenate([x.reshape(n_lat, d), ctx.reshape(n_batch * t_ctx, d)], axis=0)
    for l in range(depth):
        mod = mod_all[l]
        q, k, misc, mlac, xbc, v1t, ogt, zt, ut = _in_projection(xf, mod, norm1[l][None], w_in_tok, w_in_chan, l,
                                                                 seg_of_tile)
        hf, hb = _mlstm(q, k, v1t, misc, gate_col[l], n_batch, n_scan)
        wqa, wqb, wka, wvt, ea, eb = _mla_weights(mla_w_uq[l], mla_w_ukv[l])
        qa, ka, vta = _mla_prep(mlac, misc, cos, sin, mla_q_norm[l][None], mla_kv_norm[l][None],
                                wqa, wqb, wka, wvt, ea, eb)
        att = _attention(qa, ka, vta, n_batch, t_lat, t_ctx)
        act, xst = _ssd_prep(xbc, ssd_cw[l], ssd_conv_b[l][None], bounds)
        yf, yb = _ssd_scan(act, xst, misc, ssd_col[l], n_batch, n_scan)
        y5 = _s5_mix(ut, toep, wz, wy, s5_ca[l], s5_cb[l], l, n_batch, n_s5_lat, n_s5_ctx)
        acts = (hf, hb, ogt, yf, yb, xst, zt, att, y5, ut)
        params = (mix_cols[l], s5_w_glu[l].T.astype(BF16), w_out[l].astype(BF16), norm2[l][None])
        xf, h2 = _out_projection(xf, mod, seg_of_tile, acts, params)
        xf = _conv_ffn(xf, mod, seg_of_tile, h2, ffn_w_up[l].astype(BF16), ffn_cw[l],
                       ffn_w_down[l].astype(BF16), bounds)
    out = _final_norm(xf, final_norm[None], n_lat)
    return out.reshape(n_batch, t_lat, d)
```

```python
import functools
import math

import numpy as np
import jax
import jax.numpy as jnp
from jax import lax
from jax.experimental import pallas as pl
from jax.experimental.pallas import tpu as pltpu

F32 = jnp.float32
BF16 = jnp.bfloat16
HI = lax.Precision.HIGHEST

D_MODEL = 1024
W_GROUP = 256
EPS = 1e-6
GRID_W = 64
ROPE_BASE = 10000.0

ML_HEADS = 4
ML_DH = 64

MLA_HEADS = 4
MLA_NOPE = 64
MLA_ROPE = 32
MLA_V = 64
MLA_Q_LORA = 256
MLA_KV_LORA = 128
MLA_VROWS = 80

SSD_HEADS = 4
SSD_HEADDIM = 64
SSD_GROUPS = 2
SSD_STATE = 128

S5_GROUP = 16
S5_NGROUPS = 16
S5_STATE = 64
S5_CHUNK = 64
S5_SW = 4 * S5_STATE

D_FF = 2816
FF_CHUNK = 256

ROW_TILE = 512
SCAN_CHUNK = 256
ATTN_TQ = 512
ATTN_TK = 512
HALO = 16
LANES = 128
VMEM_LIMIT = 56 * 1024 * 1024

MISC_GATE = 0
MISC_DT = 16
MISC_KR = 24


def _cparams(sem):
    return pltpu.CompilerParams(dimension_semantics=sem, vmem_limit_bytes=VMEM_LIMIT)


def _dot(a, b):
    return jnp.dot(a, b, preferred_element_type=F32)


def _dot_nt(a, b):
    return lax.dot_general(a, b, (((1,), (1,)), ((), ())), preferred_element_type=F32)


def _dot_tn(a, b):
    return lax.dot_general(a, b, (((0,), (0,)), ((), ())), preferred_element_type=F32)


def _dot_hi(a, b):
    return jnp.dot(a, b, preferred_element_type=F32, precision=HI)


def _sigmoid(x):
    return 1.0 / (1.0 + jnp.exp(-x))


def _silu(x):
    return x * _sigmoid(x)


def _log_sigmoid(x):
    return jnp.minimum(x, 0.0) - jnp.log(1.0 + jnp.exp(-jnp.abs(x)))


def _softplus(x):
    return jnp.maximum(x, 0.0) + jnp.log(1.0 + jnp.exp(-jnp.abs(x)))


def _gelu_tanh(x):
    return 0.5 * x * (1.0 + jnp.tanh(math.sqrt(2.0 / math.pi) * (x + 0.044715 * x * x * x)))


def _rms(x, g):
    return x * lax.rsqrt(jnp.mean(x * x, axis=-1, keepdims=True) + EPS) * g


def _segment_masks(row0, n_rows, bounds):
    r = row0 + lax.broadcasted_iota(jnp.int32, (n_rows, 1), 0)
    first = r == bounds[0]
    last = r == bounds[1] - 1
    for s in bounds[1:-1]:
        first = first | (r == s)
    for e in bounds[2:]:
        last = last | (r == e - 1)
    return first, last


def _mod_kernel(c_ref, w_ref, b_ref, o_ref):
    c = c_ref[...]
    s = _silu(c).astype(BF16)
    o_ref[...] = _dot(s, w_ref[...].astype(BF16)) + b_ref[...]


def _modulation(c8, w_mod, b_mod):
    depth, d, d6 = w_mod.shape
    return pl.pallas_call(
        _mod_kernel,
        out_shape=jax.ShapeDtypeStruct((depth, 8, d6), F32),
        grid=(depth, d6 // d),
        in_specs=[
            pl.BlockSpec((8, d), lambda l, j: (0, 0)),
            pl.BlockSpec((None, d, d), lambda l, j: (l, 0, j)),
            pl.BlockSpec((None, 1, d), lambda l, j: (l, 0, j)),
        ],
        out_specs=pl.BlockSpec((None, 8, d), lambda l, j: (l, 0, j)),
        compiler_params=_cparams(("arbitrary", "arbitrary")),
        name="modulation",
    )(c8, w_mod, b_mod.reshape(depth, 1, d6))


_IN_SPLITS = (("q", 256, BF16), ("k", 256, BF16), ("misc", 128, F32), ("mlac", 384, F32), ("xbc", 768, F32))
_IN_TOTAL = sum(s[1] for s in _IN_SPLITS)
_IN_T_SPLITS = (("v1t", ML_HEADS * LANES, BF16), ("ogt", 256, F32), ("zt", 256, F32), ("ut", 256, F32))
_IN_T_TOTAL = sum(s[1] for s in _IN_T_SPLITS)


def _inproj_weights(w_in):
    ml, mla, ssd, s5 = 0, 1040, 1456, 2488
    cols = lambda a, b: w_in[..., a:b]
    zeros = lambda n: jnp.zeros(w_in.shape[:-1] + (n,), w_in.dtype)
    tok = [cols(ml, ml + 256), cols(ml + 256, ml + 512) * ML_DH ** -0.5]
    tok += [cols(ml + 1024, ml + 1040), cols(ssd + 1024, ssd + 1032),
            cols(mla + 384, mla + 416), zeros(LANES - 56)]
    tok += [cols(mla, mla + 384), cols(ssd + 256, ssd + 1024)]
    chan = []
    for h in range(ML_HEADS):
        chan += [cols(ml + 512 + h * 64, ml + 512 + (h + 1) * 64), zeros(64)]
    chan += [cols(ml + 768, ml + 1024), cols(ssd, ssd + 256), cols(s5, s5 + 256)]
    w_tok = jnp.concatenate(tok, axis=-1).astype(BF16)
    w_chan = jnp.swapaxes(jnp.concatenate(chan, axis=-1), -1, -2).astype(BF16)
    return w_tok, w_chan


def _inproj_kernel(x_ref, mod_ref, g_ref, w_ref, wt_ref, q_ref, k_ref, misc_ref, mlac_ref, xbc_ref,
                   v1t_ref, ogt_ref, zt_ref, ut_ref):
    x = x_ref[...]
    h = _rms(x, g_ref[...]) * (1.0 + mod_ref[1:2, :]) + mod_ref[0:1, :]
    hb = h.astype(BF16)
    off = 0
    for (_, width, _), o_ref in zip(_IN_SPLITS, (q_ref, k_ref, misc_ref, mlac_ref, xbc_ref)):
        o_ref[...] = _dot(hb, w_ref[:, off:off + width]).astype(o_ref.dtype)
        off += width
    off = 0
    for (name, width, _), o_ref in zip(_IN_T_SPLITS, (v1t_ref, ogt_ref, zt_ref, ut_ref)):
        y = _dot_nt(wt_ref[off:off + width, :], hb)
        if name == "v1t":
            row = lax.broadcasted_iota(jnp.int32, (width, 1), 0)
            y = y + jnp.where((row & (LANES - 1)) == ML_DH, 1.0, 0.0)
        o_ref[...] = y.astype(o_ref.dtype)
        off += width


def _in_projection(x, mod, g, w, wt, layer, seg_of_tile):
    r, d = x.shape
    row = lambda i: (i, 0)
    return pl.pallas_call(
        _inproj_kernel,
        out_shape=[jax.ShapeDtypeStruct((r, width), dt) for _, width, dt in _IN_SPLITS]
        + [jax.ShapeDtypeStruct((width, r), dt) for _, width, dt in _IN_T_SPLITS],
        grid=(r // ROW_TILE,),
        in_specs=[
            pl.BlockSpec((ROW_TILE, d), row),
            pl.BlockSpec((None, 8, d), lambda i: (seg_of_tile(i), 0, 0)),
            pl.BlockSpec((1, d), lambda i: (0, 0)),
            pl.BlockSpec((None, d, _IN_TOTAL), lambda i: (layer, 0, 0)),
            pl.BlockSpec((None, _IN_T_TOTAL, d), lambda i: (layer, 0, 0)),
        ],
        out_specs=[pl.BlockSpec((ROW_TILE, width), row) for _, width, _ in _IN_SPLITS]
        + [pl.BlockSpec((width, ROW_TILE), lambda i: (0, i)) for _, width, _ in _IN_T_SPLITS],
        compiler_params=_cparams(("arbitrary",)),
        name="in_projection",
    )(x, mod, g, w, wt)


def _tri_masks(n):
    row = lax.broadcasted_iota(jnp.int32, (n, n), 0)
    col = lax.broadcasted_iota(jnp.int32, (n, n), 1)
    return col <= row, col >= row


def _col_forms(rows16, n):
    return jnp.concatenate([rows16, jnp.zeros((LANES - rows16.shape[0], n), F32)], axis=0).T


def _mlstm_kernel(qf_ref, kf_ref, vf_ref, mf_ref, qb_ref, kb_ref, vb_ref, mb_ref,
                  gbc_ref, hf_ref, hb_ref, st_ref, m_ref):
    n = qf_ref.shape[0]

    @pl.when(pl.program_id(1) == 0)
    def _():
        st_ref[...] = jnp.zeros_like(st_ref)
        m_ref[...] = jnp.zeros_like(m_ref)

    tril, triu = _tri_masks(n)
    dirs = ((qf_ref, kf_ref, vf_ref, mf_ref, hf_ref, triu, n - 1),
            (qb_ref, kb_ref, vb_ref, mb_ref, hb_ref, tril, 0))
    pending = []
    for d, (q_ref, k_ref, vt_ref, misc_ref, out_ref, mask, last) in enumerate(dirs):
        gt = misc_ref[...].T + gbc_ref[...]
        brow = _dot_hi(_log_sigmoid(gt[0:16, :]), mask.astype(F32))
        ccol = _col_forms(gt[0:16, :] - pltpu.roll(brow, 16 - ML_HEADS, 0), n)
        for h in range(ML_HEADS):
            hd = d * ML_HEADS + h
            i_idx = MISC_GATE + 8 * d + h
            f_idx = i_idx + ML_HEADS
            ig_row = gt[i_idx:i_idx + 1, :]
            b_row = brow[f_idx:f_idx + 1, :]
            c_col = ccol[:, i_idx:i_idx + 1]
            b_last = b_row[:, last:last + 1]
            m_prev = m_ref[hd:hd + 1, 0:1]
            q = q_ref[:, h * ML_DH:(h + 1) * ML_DH]
            k = k_ref[:, h * ML_DH:(h + 1) * ML_DH]
            v1t = vt_ref[h * LANES:(h + 1) * LANES, :]
            state = st_ref[hd]

            qk = _dot_nt(k, q)
            carried = _dot_nt(state.astype(BF16), q)
            w_log = b_last - b_row + ig_row
            m_new = jnp.maximum(b_last + m_prev, jnp.max(w_log, axis=1, keepdims=True))
            decay = jnp.exp(b_last + m_prev - m_new)
            vw = (v1t.astype(F32) * jnp.exp(w_log - m_new)).astype(BF16)
            st_ref[hd] = decay * state + _dot(vw, k)
            m_ref[hd:hd + 1, :] = jnp.broadcast_to(m_new, (1, LANES))
            pending.append((out_ref, h, mask, b_row, c_col, m_prev, qk, carried, v1t))

    for out_ref, h, mask, b_row, c_col, m_prev, qk, carried, v1t in pending:
        dmat = jnp.where(mask, b_row + c_col, -jnp.inf)
        inter = b_row + m_prev
        m_t = jnp.maximum(inter, jnp.max(dmat, axis=0, keepdims=True))
        s = qk * jnp.exp(dmat - m_t)
        tot = _dot(v1t, s.astype(BF16)) + jnp.exp(inter - m_t) * carried
        den = tot[ML_DH:ML_DH + 1, :]
        out_ref[h * ML_DH:(h + 1) * ML_DH, :] = tot[0:ML_DH, :] / jnp.maximum(jnp.abs(den), jnp.exp(-m_t))


def _scan_index_maps(n_batch, n_chunks):
    ctx0 = n_batch * n_chunks
    fwd = lambda b, s: (jnp.where(s == 0, ctx0 + b, b * n_chunks + s - 1), 0)
    bwd = lambda b, s: (jnp.where(s == 0, ctx0 + b, b * n_chunks + n_chunks - s), 0)
    return fwd, bwd


def _mlstm(q, k, v1t, misc, gb_col, n_batch, n_chunks):
    r = q.shape[0]
    n = SCAN_CHUNK
    fwd, bwd = _scan_index_maps(n_batch, n_chunks)
    const = lambda b, s: (0, 0)
    specs = []
    for imap in (fwd, bwd):
        cols = lambda b, s, imap=imap: imap(b, s)[::-1]
        specs += [pl.BlockSpec((n, W_GROUP), imap), pl.BlockSpec((n, W_GROUP), imap),
                  pl.BlockSpec((ML_HEADS * LANES, n), cols), pl.BlockSpec((n, LANES), imap)]
    specs += [pl.BlockSpec((LANES, 1), const)]
    fwd_c = lambda b, s: fwd(b, s)[::-1]
    bwd_c = lambda b, s: bwd(b, s)[::-1]
    return pl.pallas_call(
        _mlstm_kernel,
        out_shape=[jax.ShapeDtypeStruct((W_GROUP, r), F32)] * 2,
        grid=(n_batch, n_chunks + 1),
        in_specs=specs,
        out_specs=[pl.BlockSpec((W_GROUP, n), fwd_c), pl.BlockSpec((W_GROUP, n), bwd_c)],
        scratch_shapes=[pltpu.VMEM((2 * ML_HEADS, LANES, ML_DH), F32),
                        pltpu.VMEM((2 * ML_HEADS, LANES), F32)],
        compiler_params=_cparams(("arbitrary", "arbitrary")),
        name="mlstm_scan",
    )(q, k, v1t, misc, q, k, v1t, misc, gb_col)


def _ssd_prep_kernel(x_ref, xp_ref, xn_ref, w_ref, b_ref, act_ref, xs_ref, *, bounds):
    tm = x_ref.shape[0]
    first, last = _segment_masks(pl.program_id(0) * tm, tm, bounds)
    x = x_ref[...]
    rid = lax.broadcasted_iota(jnp.int32, (tm, 1), 0)
    xp = jnp.where(rid == 0, xp_ref[HALO - 1:HALO, :], pltpu.roll(x, 1, 0))
    xn = jnp.where(rid == tm - 1, xn_ref[0:1, :], pltpu.roll(x, tm - 1, 0))
    xp = jnp.where(first, 0.0, xp)
    xn = jnp.where(last, 0.0, xn)
    y = _silu(w_ref[0:1, :] * xp + w_ref[1:2, :] * x + w_ref[2:3, :] * xn + b_ref[...])
    act_ref[...] = y.astype(act_ref.dtype)
    xs_ref[...] = y[:, 0:W_GROUP].T


def _halo_specs(width, n_rows):
    per = ROW_TILE // HALO
    n_blocks = n_rows // HALO
    prev = pl.BlockSpec((HALO, width), lambda i: (jnp.maximum(i * per - 1, 0), 0))
    nxt = pl.BlockSpec((HALO, width), lambda i: (jnp.minimum((i + 1) * per, n_blocks - 1), 0))
    return prev, nxt


def _ssd_prep(xbc, conv_w, conv_b, bounds):
    r, width = xbc.shape
    row = lambda i: (i, 0)
    prev, nxt = _halo_specs(width, r)
    return pl.pallas_call(
        functools.partial(_ssd_prep_kernel, bounds=bounds),
        out_shape=[jax.ShapeDtypeStruct((r, width), BF16), jax.ShapeDtypeStruct((W_GROUP, r), F32)],
        grid=(r // ROW_TILE,),
        in_specs=[pl.BlockSpec((ROW_TILE, width), row), prev, nxt,
                  pl.BlockSpec((8, width), lambda i: (0, 0)),
                  pl.BlockSpec((1, width), lambda i: (0, 0))],
        out_specs=[pl.BlockSpec((ROW_TILE, width), row), pl.BlockSpec((W_GROUP, ROW_TILE), lambda i: (0, i))],
        compiler_params=_cparams(("arbitrary",)),
        name="ssd_conv",
    )(xbc, xbc, xbc, conv_w, conv_b)


def _ssd_kernel(af_ref, xf_ref, mf_ref, ab_ref, xb_ref, mb_ref, pc_ref, yf_ref, yb_ref, st_ref):
    n = af_ref.shape[0]

    @pl.when(pl.program_id(1) == 0)
    def _():
        st_ref[...] = jnp.zeros_like(st_ref)

    tril, triu = _tri_masks(n)
    dirs = ((af_ref, xf_ref, mf_ref, yf_ref, triu, n - 1),
            (ab_ref, xb_ref, mb_ref, yb_ref, tril, 0))
    pending = []
    for d, (act_ref, xt_ref, misc_ref, out_ref, mask, last) in enumerate(dirs):
        dt_r = _softplus(misc_ref[...].T + pc_ref[:, 0:1])
        cs_r = _dot_hi((dt_r * pc_ref[:, 1:2])[0:32, :], mask.astype(F32))
        ncs_c = _col_forms(-cs_r, n)
        for g in range(SSD_GROUPS):
            bm = act_ref[:, W_GROUP + g * SSD_STATE:W_GROUP + (g + 1) * SSD_STATE]
            cm = act_ref[:, W_GROUP + SSD_GROUPS * SSD_STATE + g * SSD_STATE:
                         W_GROUP + SSD_GROUPS * SSD_STATE + (g + 1) * SSD_STATE]
            gmat = _dot_nt(bm, cm)
            for hh in range(SSD_HEADS // SSD_GROUPS):
                h = g * (SSD_HEADS // SSD_GROUPS) + hh
                hd = d * SSD_HEADS + h
                idx = MISC_DT + d * SSD_HEADS + h
                cs_row = cs_r[idx:idx + 1, :]
                ncs_col = ncs_c[:, idx:idx + 1]
                cs_last = cs_row[:, last:last + 1]
                dt_row = dt_r[idx:idx + 1, :]
                xt = xt_ref[h * SSD_HEADDIM:(h + 1) * SSD_HEADDIM, :]
                state = st_ref[hd]
                carried = _dot_nt(state.astype(BF16), cm)
                xw = (xt * (jnp.exp(cs_last - cs_row) * dt_row)).astype(BF16)
                st_ref[hd] = jnp.exp(cs_last) * state + _dot(xw, bm)
                pending.append((out_ref, h, mask, cs_row, ncs_col, gmat, carried, (xt * dt_row).astype(BF16)))

    for out_ref, h, mask, cs_row, ncs_col, gmat, carried, xdt in pending:
        decay = jnp.exp(jnp.where(mask, cs_row + ncs_col, -jnp.inf))
        y = _dot(xdt, (gmat * decay).astype(BF16)) + jnp.exp(cs_row) * carried
        out_ref[h * SSD_HEADDIM:(h + 1) * SSD_HEADDIM, :] = y


def _ssd_scan(act, xst, misc, p_col, n_batch, n_chunks):
    r, width = act.shape
    n = SCAN_CHUNK
    fwd, bwd = _scan_index_maps(n_batch, n_chunks)
    const = lambda b, s: (0, 0)
    fwd_c = lambda b, s: fwd(b, s)[::-1]
    bwd_c = lambda b, s: bwd(b, s)[::-1]
    specs = []
    for imap, cmap in ((fwd, fwd_c), (bwd, bwd_c)):
        specs += [pl.BlockSpec((n, width), imap), pl.BlockSpec((W_GROUP, n), cmap), pl.BlockSpec((n, LANES), imap)]
    specs += [pl.BlockSpec((LANES, 8), const)]
    return pl.pallas_call(
        _ssd_kernel,
        out_shape=[jax.ShapeDtypeStruct((W_GROUP, r), F32)] * 2,
        grid=(n_batch, n_chunks + 1),
        in_specs=specs,
        out_specs=[pl.BlockSpec((W_GROUP, n), fwd_c), pl.BlockSpec((W_GROUP, n), bwd_c)],
        scratch_shapes=[pltpu.VMEM((2 * SSD_HEADS, SSD_HEADDIM, SSD_STATE), F32)],
        compiler_params=_cparams(("arbitrary", "arbitrary")),
        name="ssd_scan",
    )(act, xst, misc, act, xst, misc, p_col)


def _mla_prep_kernel(c_ref, misc_ref, cos_ref, sin_ref, qn_ref, kvn_ref, wqa_ref, wqb_ref,
                     wka_ref, wvt_ref, q_ref, k_ref, vt_ref):
    scale = (MLA_NOPE + MLA_ROPE) ** -0.5 * math.log2(math.e)
    half = MLA_ROPE // 2
    cq = _rms(c_ref[:, 0:MLA_Q_LORA], qn_ref[...]).astype(BF16)
    ckv = _rms(c_ref[:, MLA_Q_LORA:MLA_Q_LORA + MLA_KV_LORA], kvn_ref[...]).astype(BF16)
    cos = cos_ref[...]
    sin = sin_ref[...]
    qa = _dot(cq, wqa_ref[...])
    qb = _dot(cq, wqb_ref[...])
    ka = _dot(ckv, wka_ref[...])
    misc = misc_ref[...]
    lane = lax.broadcasted_iota(jnp.int32, (1, LANES), 1)
    in_rope = (lane >= MLA_NOPE) & (lane < MLA_NOPE + MLA_ROPE)
    kr_a = pltpu.roll(misc, MLA_NOPE - MISC_KR, 1)
    kr_b = jnp.where(lane < MLA_NOPE + half,
                     -pltpu.roll(misc, MLA_NOPE - MISC_KR - half, 1),
                     pltpu.roll(misc, MLA_NOPE - MISC_KR + half, 1))
    kr = jnp.where(in_rope, kr_a * cos + kr_b * sin, 0.0)
    for h in range(MLA_HEADS):
        sl = slice(h * LANES, (h + 1) * LANES)
        q_ref[:, sl] = ((qa[:, sl] * cos + qb[:, sl] * sin) * scale).astype(BF16)
        k_ref[:, sl] = (ka[:, sl] + kr).astype(BF16)
    vt = _dot_nt(wvt_ref[...], ckv)
    row = lax.broadcasted_iota(jnp.int32, (vt.shape[0], 1), 0)
    ones_row = row == MLA_V
    for h in range(1, MLA_HEADS):
        ones_row = ones_row | (row == h * MLA_VROWS + MLA_V)
    vt_ref[...] = (vt + jnp.where(ones_row, 1.0, 0.0)).astype(BF16)


def _mla_prep(mlac, misc, cos, sin, qn, kvn, wqa, wqb, wka, wvt):
    r = mlac.shape[0]
    row = lambda i: (i, 0)
    full = lambda a: pl.BlockSpec(a.shape, lambda i: (0, 0))
    hw = MLA_HEADS * LANES
    vr = MLA_HEADS * MLA_VROWS
    return pl.pallas_call(
        _mla_prep_kernel,
        out_shape=[jax.ShapeDtypeStruct((r, hw), BF16)] * 2 + [jax.ShapeDtypeStruct((vr, r), BF16)],
        grid=(r // ROW_TILE,),
        in_specs=[pl.BlockSpec((ROW_TILE, mlac.shape[1]), row), pl.BlockSpec((ROW_TILE, LANES), row),
                  pl.BlockSpec((ROW_TILE, LANES), row), pl.BlockSpec((ROW_TILE, LANES), row),
                  full(qn), full(kvn), full(wqa), full(wqb), full(wka), full(wvt)],
        out_specs=[pl.BlockSpec((ROW_TILE, hw), row)] * 2 + [pl.BlockSpec((vr, ROW_TILE), lambda i: (0, i))],
        compiler_params=_cparams(("arbitrary",)),
        name="mla_prep",
    )(mlac, misc, cos, sin, qn, kvn, wqa, wqb, wka, wvt)


_HEAD_LANES = tuple(slice(h * LANES, (h + 1) * LANES) for h in range(MLA_HEADS))
_HEAD_VROWS = tuple(slice(h * MLA_VROWS, (h + 1) * MLA_VROWS) for h in range(MLA_HEADS))


def _attn_step(q_ref, m_ref, acc_ref, scores=None, update=None):
    new = []
    for h in range(MLA_HEADS):
        if scores is not None:
            k_ref, rows, n_keys, (s_ref, mx_ref) = scores
            st = _dot_nt(k_ref[rows, _HEAD_LANES[h]], q_ref[:, _HEAD_LANES[h]])
            s_ref[h, 0:n_keys, :] = st
            mx_ref[h] = jnp.max(st, axis=0, keepdims=True)
        if update is not None:
            (s_ref, mx_ref), n_keys, vt_ref, cols = update
            m_old = m_ref[h]
            m_new = jnp.maximum(m_old, mx_ref[h])
            p = jnp.exp2(s_ref[h, 0:n_keys, :] - m_new).astype(BF16)
            new.append((m_new, jnp.exp2(m_old - m_new) * acc_ref[h] + _dot(vt_ref[_HEAD_VROWS[h], cols], p)))
    for h, (m_new, acc) in enumerate(new):
        m_ref[h] = m_new
        acc_ref[h] = acc


def _attn_init(m_ref, acc_ref):
    m_ref[...] = jnp.full_like(m_ref, -jnp.inf)
    acc_ref[...] = jnp.zeros_like(acc_ref)


def _attn_finish(o_ref, acc_ref):
    for h in range(MLA_HEADS):
        acc = acc_ref[h]
        o_ref[h * MLA_V:(h + 1) * MLA_V, :] = (acc[0:MLA_V] / acc[MLA_V:MLA_V + 1]).astype(o_ref.dtype)


def _attn_lat_kernel(q_ref, kl_ref, vlt_ref, kc_ref, vct_ref, o_ref, m_ref, acc_ref,
                     sa_s_ref, sa_m_ref, sb_s_ref, sb_m_ref):
    sa_ref = (sa_s_ref, sa_m_ref)
    sb_ref = (sb_s_ref, sb_m_ref)
    tk = ATTN_TK
    n_pairs = kl_ref.shape[0] // (2 * tk)
    n_ctx = kc_ref.shape[0]
    chunk = lambda c: pl.ds(pl.multiple_of(c * tk, tk), tk)
    step = functools.partial(_attn_step, q_ref, m_ref, acc_ref)
    _attn_init(m_ref, acc_ref)
    step(scores=(kl_ref, chunk(0), tk, sa_ref))

    def body(i, carry):
        c = 2 * i
        step(scores=(kl_ref, chunk(c + 1), tk, sb_ref), update=(sa_ref, tk, vlt_ref, chunk(c)))
        step(scores=(kl_ref, chunk(c + 2), tk, sa_ref), update=(sb_ref, tk, vlt_ref, chunk(c + 1)))
        return carry

    lax.fori_loop(0, n_pairs - 1, body, 0)
    c = 2 * (n_pairs - 1)
    step(scores=(kl_ref, chunk(c + 1), tk, sb_ref), update=(sa_ref, tk, vlt_ref, chunk(c)))
    step(scores=(kc_ref, slice(None), n_ctx, sa_ref), update=(sb_ref, tk, vlt_ref, chunk(c + 1)))
    step(update=(sa_ref, n_ctx, vct_ref, slice(None)))
    _attn_finish(o_ref, acc_ref)


def _attn_ctx_kernel(q_ref, kc_ref, vct_ref, o_ref, m_ref, acc_ref, s_ref, mx_ref):
    n_ctx = kc_ref.shape[0]
    _attn_init(m_ref, acc_ref)
    _attn_step(q_ref, m_ref, acc_ref, scores=(kc_ref, slice(None), n_ctx, (s_ref, mx_ref)))
    _attn_step(q_ref, m_ref, acc_ref, update=((s_ref, mx_ref), n_ctx, vct_ref, slice(None)))
    _attn_finish(o_ref, acc_ref)


def _attention(q, k, vt, n_batch, t_lat, t_ctx):
    r, hw = q.shape
    vr = vt.shape[0]
    nq = t_lat // ATTN_TQ
    ctx_blk = n_batch * t_lat // t_ctx
    assert t_lat % (2 * ATTN_TK) == 0 and t_ctx <= ATTN_TK
    scratch = lambda tq: [pltpu.VMEM((MLA_HEADS, 1, tq), F32), pltpu.VMEM((MLA_HEADS, MLA_VROWS, tq), F32)]
    scores = lambda tk, tq: [pltpu.VMEM((MLA_HEADS, tk, tq), F32), pltpu.VMEM((MLA_HEADS, 1, tq), F32)]
    lat = pl.pallas_call(
        _attn_lat_kernel,
        out_shape=jax.ShapeDtypeStruct((W_GROUP, n_batch * t_lat), BF16),
        grid=(n_batch, nq),
        in_specs=[pl.BlockSpec((ATTN_TQ, hw), lambda b, i: (b * nq + i, 0)),
                  pl.BlockSpec((t_lat, hw), lambda b, i: (b, 0)),
                  pl.BlockSpec((vr, t_lat), lambda b, i: (0, b)),
                  pl.BlockSpec((t_ctx, hw), lambda b, i: (ctx_blk + b, 0)),
                  pl.BlockSpec((vr, t_ctx), lambda b, i: (0, ctx_blk + b))],
        out_specs=pl.BlockSpec((W_GROUP, ATTN_TQ), lambda b, i: (0, b * nq + i)),
        scratch_shapes=scratch(ATTN_TQ) + scores(ATTN_TK, ATTN_TQ) * 2,
        compiler_params=_cparams(("arbitrary", "arbitrary")),
        name="mla_attention_latent",
    )(q, k, vt, k, vt)
    ctx = pl.pallas_call(
        _attn_ctx_kernel,
        out_shape=jax.ShapeDtypeStruct((W_GROUP, n_batch * t_ctx), BF16),
        grid=(n_batch,),
        in_specs=[pl.BlockSpec((t_ctx, hw), lambda b: (ctx_blk + b, 0)),
                  pl.BlockSpec((t_ctx, hw), lambda b: (ctx_blk + b, 0)),
                  pl.BlockSpec((vr, t_ctx), lambda b: (0, ctx_blk + b))],
        out_specs=pl.BlockSpec((W_GROUP, t_ctx), lambda b: (0, b)),
        scratch_shapes=scratch(t_ctx) + scores(t_ctx, t_ctx),
        compiler_params=_cparams(("arbitrary",)),
        name="mla_attention_context",
    )(q, k, vt)
    return lat, ctx


def _cis_pow(lr, li, dt, k):
    mag = jnp.exp(lr * dt * k)
    ang = li * dt * k
    return mag * jnp.cos(ang), mag * jnp.sin(ang)


def _s5_tables_kernel(pr_ref, pc_ref, btre_ref, btim_ref, ctre_ref, ctim_ref,
                      toep_ref, wz_ref, wy_ref, decay_ref):
    lc, ng, ns = S5_CHUNK, S5_GROUP, S5_STATE
    lr_r = jnp.minimum(pr_ref[0:1, :], -1e-4)
    li_r = pr_ref[1:2, :]
    dt_r = jnp.exp(pr_ref[2:3, :])
    lr_c = jnp.minimum(pc_ref[:, 0:1], -1e-4)
    li_c = pc_ref[:, 1:2]
    dt_c = jnp.exp(pc_ref[:, 2:3])
    lane4 = lax.broadcasted_iota(jnp.int32, (1, S5_SW), 1)
    odd_lane = (lane4 & ns) != 0
    row4 = lax.broadcasted_iota(jnp.int32, (S5_SW, 1), 0)
    odd_row = (row4 & ns) != 0

    ab_re, ab_im = _cis_pow(lr_r, li_r, dt_r, 1.0)
    den = lr_r * lr_r + li_r * li_r
    f_re = ((ab_re - 1.0) * lr_r + ab_im * li_r) / den
    f_im = (ab_im * lr_r - (ab_re - 1.0) * li_r) / den
    bb_re = f_re * btre_ref[...] - f_im * btim_ref[...]
    bb_im = f_re * btim_ref[...] + f_im * btre_ref[...]

    s_col = lax.broadcasted_iota(jnp.int32, (lc, 1), 0)
    expo = jnp.where(lane4 < 2 * ns, lc - 1 - s_col, s_col).astype(F32)
    pz_re, pz_im = _cis_pow(lr_r, li_r, dt_r, expo)
    xz = jnp.where(odd_lane, bb_im, bb_re)
    yz = jnp.where(odd_lane, bb_re, -bb_im)
    for j in range(ng):
        wz_ref[j] = (pz_re * xz[j:j + 1, :] + pz_im * yz[j:j + 1, :]).astype(BF16)

    al_re, al_im = _cis_pow(lr_r, li_r, dt_r, float(lc))
    al_sw = jnp.where(odd_lane, al_im, -al_im)
    decay_ref[...] = jnp.zeros_like(decay_ref)
    for d in range(2):
        decay_ref[d:d + 1, :] = al_re[:, d * LANES:(d + 1) * LANES]
        decay_ref[2 + d:3 + d, :] = al_sw[:, d * LANES:(d + 1) * LANES]

    t_lane = lax.broadcasted_iota(jnp.int32, (1, LANES), 1) & (lc - 1)
    expo = jnp.where(row4 < 2 * ns, t_lane + 1, lc - t_lane).astype(F32)
    py_re, py_im = _cis_pow(lr_c, li_c, dt_c, expo)
    reps = lc * ng // LANES
    py_re = jnp.tile(py_re, (1, reps))
    py_im = jnp.tile(py_im, (1, reps))
    low_half = lax.broadcasted_iota(jnp.int32, (1, 2 * lc), 1) < lc

    def expand_half(c):
        bc = lambda j: jnp.broadcast_to(c[:, j:j + 1], (c.shape[0], 2 * lc))
        return jnp.concatenate([jnp.where(low_half, bc(2 * p), bc(2 * p + 1)) for p in range(ng // 2)], axis=1)

    def expand_full(c):
        return jnp.concatenate([jnp.broadcast_to(c[:, j:j + 1], (c.shape[0], 2 * lc)) for j in range(ng)], axis=1)

    ce_re = expand_half(ctre_ref[...])
    ce_im = expand_half(ctim_ref[...])
    wy_ref[...] = jnp.where(odd_row, -(ce_re * py_im + ce_im * py_re),
                            ce_re * py_re - ce_im * py_im).astype(BF16)

    m_lane = lax.broadcasted_iota(jnp.int32, (1, 2 * lc), 1)
    xmat = None
    for d in range(2):
        rows = slice(d * 2 * ns, d * 2 * ns + ns)
        lag = m_lane - (lc - 1) if d == 0 else (lc - 1) - m_lane
        valid = (lag >= 0) & (m_lane < 2 * lc - 1)
        pm_re, pm_im = _cis_pow(lr_c[rows], li_c[rows], dt_c[rows], jnp.where(valid, lag, 0).astype(F32))
        pm_re = jnp.tile(jnp.where(valid, pm_re, 0.0), (1, ng))
        pm_im = jnp.tile(jnp.where(valid, pm_im, 0.0), (1, ng))
        c2_re = expand_full(ctre_ref[rows, :])
        c2_im = expand_full(ctim_ref[rows, :])
        p_re = c2_re * pm_re - c2_im * pm_im
        p_im = c2_re * pm_im + c2_im * pm_re
        lanes = slice(d * 2 * ns, d * 2 * ns + ns)
        term = _dot_hi(bb_re[:, lanes], p_re) - _dot_hi(bb_im[:, lanes], p_im)
        xmat = term if xmat is None else xmat + term

    for j in range(ng):
        xb = jnp.broadcast_to(xmat[j:j + 1, :], (lc, 2 * lc * ng))
        cols = []
        for p in range(ng // 2):
            even = xb[:, (2 * p) * 2 * lc:(2 * p + 1) * 2 * lc]
            odd = xb[:, (2 * p + 1) * 2 * lc:(2 * p + 2) * 2 * lc]
            cols.append(jnp.where(low_half,
                                  pltpu.roll(even, lc + 1, 1, stride=1, stride_axis=0),
                                  pltpu.roll(odd, 1, 1, stride=1, stride_axis=0)))
        toep_ref[j] = jnp.concatenate(cols, axis=1).astype(BF16)


def _s5_tables(a_re, a_im, log_dt, b_re, b_im, c_re, c_im):
    depth = a_re.shape[0]
    g, n, j, lc = S5_NGROUPS, S5_STATE, S5_GROUP, S5_CHUNK
    n_all = depth * g

    def parts(v):
        v = jnp.transpose(v, (0, 2, 1, 3))
        return jnp.concatenate([v[:, :, 0], v[:, :, 0], v[:, :, 1], v[:, :, 1]], axis=-1).reshape(n_all, S5_SW)

    rows = jnp.stack([parts(a_re), parts(a_im), parts(jnp.broadcast_to(log_dt[..., None], a_re.shape))], axis=1)
    p_row = jnp.pad(rows, ((0, 0), (0, 5), (0, 0)))
    p_col = jnp.swapaxes(p_row, 1, 2)
    bt = lambda b: jnp.tile(jnp.swapaxes(b.reshape(n_all, n, j), 1, 2), (1, 1, 4))
    ct = lambda c: jnp.tile(jnp.swapaxes(c.reshape(n_all, j, n), 1, 2), (1, 4, 1))
    own = lambda i: (i, 0, 0)
    width = lc * j
    return pl.pallas_call(
        _s5_tables_kernel,
        out_shape=[jax.ShapeDtypeStruct((n_all * j, lc, width), BF16),
                   jax.ShapeDtypeStruct((n_all * j, lc, S5_SW), BF16),
                   jax.ShapeDtypeStruct((n_all, S5_SW, width), BF16),
                   jax.ShapeDtypeStruct((n_all, 8, LANES), F32)],
        grid=(n_all,),
        in_specs=[pl.BlockSpec((None, 8, S5_SW), own), pl.BlockSpec((None, S5_SW, 8), own),
                  pl.BlockSpec((None, j, S5_SW), own), pl.BlockSpec((None, j, S5_SW), own),
                  pl.BlockSpec((None, S5_SW, j), own), pl.BlockSpec((None, S5_SW, j), own)],
        out_specs=[pl.BlockSpec((j, lc, width), own), pl.BlockSpec((j, lc, S5_SW), own),
                   pl.BlockSpec((None, S5_SW, width), own), pl.BlockSpec((None, 8, LANES), own)],
        compiler_params=_cparams(("arbitrary",)),
        name="s5_tables",
    )(p_row, p_col, bt(b_re), bt(b_im), ct(c_re), ct(c_im))


def _s5_local_kernel(u_ref, wz_ref, z_ref):
    acc = None
    for j in range(S5_GROUP):
        term = _dot(u_ref[j].astype(BF16), wz_ref[j])
        acc = term if acc is None else acc + term
    z_ref[...] = acc


def _s5_scan_kernel(z_ref, ca_ref, cb_ref, x_ref, xf_ref, xb_ref, *, n_batch, n_lat, n_ctx):
    ca = ca_ref[...]
    cb = cb_ref[...]
    zero = jnp.zeros(ca.shape, F32)
    ctx0 = n_batch * n_lat

    def step(x, z):
        return x * ca + pltpu.roll(x, LANES // 2, 1) * cb + z

    def body(i, carry):
        new = []
        for b in range(n_batch):
            rf = jnp.where(i < n_ctx, ctx0 + b * n_ctx + i, b * n_lat + i - n_ctx)
            rb = jnp.where(i < n_ctx, ctx0 + b * n_ctx + n_ctx - 1 - i, b * n_lat + n_lat - 1 - (i - n_ctx))
            xf, xb = carry[2 * b], carry[2 * b + 1]
            xf_ref[rf] = xf
            xb_ref[rb] = xb
            new += [step(xf, z_ref[rf]), step(xb, z_ref[rb])]
        return tuple(new)

    lax.fori_loop(0, n_ctx + n_lat, body, (zero,) * (2 * n_batch))
    row = lax.broadcasted_iota(jnp.int32, x_ref.shape, 1)
    x_ref[...] = jnp.where((row & 1) == 0, xf_ref[...], xb_ref[...])


def _s5_out_kernel(u_ref, t_ref, x_ref, wy_ref, y_ref):
    acc = _dot(x_ref[...].astype(BF16), wy_ref[...])
    for j in range(S5_GROUP):
        acc = acc + _dot(u_ref[j].astype(BF16), t_ref[j])
    for j in range(S5_GROUP):
        y_ref[j] = acc[:, j * S5_CHUNK:(j + 1) * S5_CHUNK]


def _s5_mix(ut, toep, wz, wy, ca, cb, layer, n_batch, n_lat, n_ctx):
    g, j, lc = S5_NGROUPS, S5_GROUP, S5_CHUNK
    nr = ut.shape[1] // lc
    u3 = ut.reshape(W_GROUP, nr, lc)
    grp = lambda i: (i, 0, 0)
    tab = lambda i: (layer * g + i, 0, 0)
    z = pl.pallas_call(
        _s5_local_kernel,
        out_shape=jax.ShapeDtypeStruct((nr, g * S5_SW), F32),
        grid=(g,),
        in_specs=[pl.BlockSpec((j, nr, lc), grp), pl.BlockSpec((j, lc, S5_SW), tab)],
        out_specs=pl.BlockSpec((nr, S5_SW), lambda i: (0, i)),
        compiler_params=_cparams(("arbitrary",)),
        name="s5_local_state",
    )(u3, wz)
    z3 = z.reshape(nr, 2 * g, LANES)
    whole = lambda shp: pl.BlockSpec(shp, lambda: tuple(0 for _ in shp))
    x3 = pl.pallas_call(
        functools.partial(_s5_scan_kernel, n_batch=n_batch, n_lat=n_lat, n_ctx=n_ctx),
        out_shape=jax.ShapeDtypeStruct(z3.shape, F32),
        in_specs=[whole(z3.shape), whole(ca.shape), whole(cb.shape)],
        out_specs=whole(z3.shape),
        scratch_shapes=[pltpu.VMEM(z3.shape, F32), pltpu.VMEM(z3.shape, F32)],
        compiler_params=pltpu.CompilerParams(vmem_limit_bytes=VMEM_LIMIT),
        name="s5_chunk_scan",
    )(z3, ca, cb)
    x = x3.reshape(nr, g * S5_SW)
    y3 = pl.pallas_call(
        _s5_out_kernel,
        out_shape=jax.ShapeDtypeStruct((W_GROUP, nr, lc), F32),
        grid=(g,),
        in_specs=[pl.BlockSpec((j, nr, lc), grp), pl.BlockSpec((j, lc, lc * j), tab),
                  pl.BlockSpec((nr, S5_SW), lambda i: (0, i)),
                  pl.BlockSpec((None, S5_SW, lc * j), tab)],
        out_specs=pl.BlockSpec((j, nr, lc), grp),
        compiler_params=_cparams(("arbitrary",)),
        name="s5_output",
    )(u3, toep, x, wy)
    return y3.reshape(W_GROUP, nr * lc)


def _outproj_kernel(x_ref, mod_ref, hf_ref, hb_ref, og_ref, yf_ref, yb_ref, xs_ref, z_ref,
                    s5_ref, ut_ref, attl_ref, attc_ref, pcol_ref, glu_ref, wo_ref, n2_ref, xo_ref, h2_ref,
                    *, n_lat_tiles):
    h = hf_ref[...] + hb_ref[...]
    parts = []
    for i in range(ML_HEADS):
        hh = h[i * ML_DH:(i + 1) * ML_DH, :]
        parts.append(hh * lax.rsqrt(jnp.mean(hh * hh, axis=0, keepdims=True) + EPS))
    ya = jnp.concatenate(parts, axis=0) * pcol_ref[:, 0:1] * _sigmoid(og_ref[...])
    yc = (pcol_ref[:, 1:2] * xs_ref[...] + yf_ref[...] + yb_ref[...]) * _silu(z_ref[...])
    yc = yc * lax.rsqrt(jnp.mean(yc * yc, axis=0, keepdims=True) + EPS) * pcol_ref[:, 2:3]
    yd = _gelu_tanh(pcol_ref[:, 3:4] * ut_ref[...] + s5_ref[...])
    yd = yd * _sigmoid(_dot(glu_ref[...], yd.astype(BF16)))
    att = jnp.where(pl.program_id(0) < n_lat_tiles, attl_ref[...], attc_ref[...])
    mixed = jnp.concatenate([ya.astype(BF16), att, yc.astype(BF16), yd.astype(BF16)], axis=0)
    y = _dot_tn(mixed, wo_ref[...])
    x = x_ref[...] + mod_ref[2:3, :] * y
    xo_ref[...] = x
    h2 = _rms(x, n2_ref[...]) * (1.0 + mod_ref[4:5, :]) + mod_ref[3:4, :]
    h2_ref[...] = h2.astype(BF16)


def _out_projection(x, mod, seg_of_tile, acts, att_lat, att_ctx, params):
    r, d = x.shape
    n_lat_tiles = att_lat.shape[1] // ROW_TILE
    assert att_ctx.shape[1] == ROW_TILE and r == (n_lat_tiles + 1) * ROW_TILE
    row = lambda i: (i, 0)
    full = lambda a: pl.BlockSpec(a.shape, lambda i: (0, 0))
    return pl.pallas_call(
        functools.partial(_outproj_kernel, n_lat_tiles=n_lat_tiles),
        out_shape=[jax.ShapeDtypeStruct((r, d), F32), jax.ShapeDtypeStruct((r, d), BF16)],
        grid=(r // ROW_TILE,),
        in_specs=[pl.BlockSpec((ROW_TILE, d), row),
                  pl.BlockSpec((None, 8, d), lambda i: (seg_of_tile(i), 0, 0))]
        + [pl.BlockSpec((W_GROUP, ROW_TILE), lambda i: (0, i)) for _ in acts]
        + [pl.BlockSpec((W_GROUP, ROW_TILE), lambda i: (0, jnp.minimum(i, n_lat_tiles - 1))),
           pl.BlockSpec((W_GROUP, ROW_TILE), lambda i: (0, 0))]
        + [full(p) for p in params],
        out_specs=[pl.BlockSpec((ROW_TILE, d), row)] * 2,
        compiler_params=_cparams(("arbitrary",)),
        name="out_projection",
    )(x, mod, *acts, att_lat, att_ctx, *params)


def _ffn_kernel(x_ref, mod_ref, h_ref, hp_ref, hn_ref, wup_ref, cw_ref, wdn_ref, fg_ref, o_ref, acc_ref,
                *, bounds, final_norm):
    tm = h_ref.shape[0]
    first, last = _segment_masks(pl.program_id(0) * tm, tm, bounds)
    h = h_ref[...]
    h_ext = jnp.concatenate([hp_ref[...], h, hn_ref[...]], axis=0)
    n_ext = tm + 2 * HALO
    n_chunks = D_FF // FF_CHUNK

    def up(j):
        c0 = j * FF_CHUNK
        return (_dot(h, wup_ref[:, c0:c0 + FF_CHUNK]),
                _dot(h_ext, wup_ref[:, D_FF + c0:D_FF + c0 + FF_CHUNK]))

    def down(j, u, g_ext):
        c0 = j * FF_CHUNK
        gp = jnp.where(first, 0.0, pltpu.roll(g_ext, 1, 0)[HALO:HALO + tm])
        gn = jnp.where(last, 0.0, pltpu.roll(g_ext, n_ext - 1, 0)[HALO:HALO + tm])
        gc = g_ext[HALO:HALO + tm]
        cw = cw_ref[:, c0:c0 + FF_CHUNK]
        conv = cw[0:1, :] * gp + cw[1:2, :] * gc + cw[2:3, :] * gn
        act = (_silu(conv) * u).astype(BF16)
        return _dot(act, wdn_ref[c0:c0 + FF_CHUNK, :])

    nxt = up(0)
    for j in range(n_chunks):
        cur = nxt
        if j + 1 < n_chunks:
            nxt = up(j + 1)
        y = down(j, *cur)
        if j == 0:
            acc_ref[...] = y
        else:
            acc_ref[...] += y
    y = x_ref[...] + mod_ref[5:6, :] * acc_ref[...]
    o_ref[...] = _rms(y, fg_ref[...]) if final_norm else y


def _conv_ffn(x, mod, seg_of_tile, h2, w_up, conv_w, w_down, bounds, n_out_rows, final_gain=None):
    r, d = x.shape
    row = lambda i: (i, 0)
    prev, nxt = _halo_specs(d, r)
    full = lambda a: pl.BlockSpec(a.shape, lambda i: (0, 0))
    gain = jnp.ones((1, d), F32) if final_gain is None else final_gain
    return pl.pallas_call(
        functools.partial(_ffn_kernel, bounds=bounds, final_norm=final_gain is not None),
        out_shape=jax.ShapeDtypeStruct((n_out_rows, d), F32),
        grid=(n_out_rows // ROW_TILE,),
        in_specs=[pl.BlockSpec((ROW_TILE, d), row),
                  pl.BlockSpec((None, 8, d), lambda i: (seg_of_tile(i), 0, 0)),
                  pl.BlockSpec((ROW_TILE, d), row), prev, nxt,
                  full(w_up), full(conv_w), full(w_down), full(gain)],
        out_specs=pl.BlockSpec((ROW_TILE, d), row),
        scratch_shapes=[pltpu.VMEM((ROW_TILE, d), F32)],
        compiler_params=_cparams(("arbitrary",)),
        name="conv_ffn",
    )(x, mod, h2, h2, h2, w_up, conv_w, w_down, gain)


def _rope_tables(t_lat, n_ctx_rows):
    rows = t_lat // GRID_W
    row = jnp.broadcast_to(jnp.arange(rows)[:, None], (rows, GRID_W)).reshape(-1).astype(F32)
    col = jnp.broadcast_to(jnp.arange(GRID_W)[None, :], (rows, GRID_W)).reshape(-1).astype(F32)
    n_freq = MLA_ROPE // 4
    inv = ROPE_BASE ** (-jnp.arange(n_freq, dtype=F32) / n_freq)
    ang = jnp.concatenate([row[:, None] * inv, col[:, None] * inv], axis=-1)
    half = MLA_ROPE // 2
    pad = LANES - MLA_NOPE - MLA_ROPE

    def table(t, lead):
        return jnp.concatenate([jnp.full((t_lat, MLA_NOPE), lead, F32), t, t, jnp.zeros((t_lat, pad), F32)], axis=-1)

    cos = table(jnp.cos(ang), 1.0)
    sin = table(jnp.sin(ang), 0.0)
    lane = np.arange(LANES)
    cos_ctx = jnp.broadcast_to(jnp.asarray((lane < MLA_NOPE + 2 * half).astype(np.float32)), (n_ctx_rows, LANES))
    sin_ctx = jnp.zeros((n_ctx_rows, LANES), F32)
    return cos, sin, cos_ctx, sin_ctx


def _mla_weights(w_uq, w_ukv):
    half = MLA_ROPE // 2
    qd = MLA_NOPE + MLA_ROPE
    kd = MLA_NOPE + MLA_V
    hw = MLA_HEADS * LANES
    qa_idx = np.zeros((hw,), np.int32); qa_s = np.zeros((hw,), np.float32)
    qb_idx = np.zeros((hw,), np.int32); qb_s = np.zeros((hw,), np.float32)
    ka_idx = np.zeros((hw,), np.int32); ka_s = np.zeros((hw,), np.float32)
    v_idx = np.zeros((MLA_HEADS * MLA_VROWS,), np.int32); v_s = np.zeros((MLA_HEADS * MLA_VROWS,), np.float32)
    for h in range(MLA_HEADS):
        for i in range(qd):
            qa_idx[h * LANES + i] = h * qd + i
            qa_s[h * LANES + i] = 1.0
        for i in range(half):
            qb_idx[h * LANES + MLA_NOPE + i] = h * qd + MLA_NOPE + half + i
            qb_s[h * LANES + MLA_NOPE + i] = -1.0
            qb_idx[h * LANES + MLA_NOPE + half + i] = h * qd + MLA_NOPE + i
            qb_s[h * LANES + MLA_NOPE + half + i] = 1.0
        for i in range(MLA_NOPE):
            ka_idx[h * LANES + i] = h * kd + i
            ka_s[h * LANES + i] = 1.0
        for i in range(MLA_V):
            v_idx[h * MLA_VROWS + i] = h * kd + MLA_NOPE + i
            v_s[h * MLA_VROWS + i] = 1.0
    pick = lambda w, idx, s: (w[..., idx] * s).astype(BF16)
    return (pick(w_uq, qa_idx, qa_s), pick(w_uq, qb_idx, qb_s), pick(w_ukv, ka_idx, ka_s),
            pick(w_ukv, v_idx, v_s).T)


def _lane_rows(vals, offset, n_rows=8):
    k, n = vals.shape
    out = jnp.zeros((n_rows, LANES), F32)
    return out.at[:k, offset:offset + n].set(vals)


def kernel(x, c, ctx, c_ctx, w_mod, b_mod, norm1, norm2, w_in, ml_gate_bias, ml_norm, mla_q_norm, mla_kv_norm, mla_w_uq, mla_w_ukv, ssd_conv_w, ssd_conv_b, ssd_a_log, ssd_dt_bias, ssd_d, ssd_norm, s5_a_re, s5_a_im, s5_log_dt, s5_b_re, s5_b_im, s5_c_re, s5_c_im, s5_d, s5_w_glu, w_out, ffn_w_up, ffn_conv_w, ffn_w_down, final_norm):
    n_batch, t_lat, d = x.shape
    t_ctx = ctx.shape[1]
    depth = w_mod.shape[0]
    assert d == D_MODEL and t_ctx == SCAN_CHUNK and t_lat % ROW_TILE == 0
    assert (n_batch * t_ctx) % ROW_TILE == 0 and t_lat % ATTN_TK == 0 and t_lat % GRID_W == 0
    n_lat = n_batch * t_lat
    n_rows = n_lat + n_batch * t_ctx
    n_scan = t_lat // SCAN_CHUNK
    bounds = tuple(b * t_lat for b in range(n_batch)) + tuple(n_lat + b * t_ctx for b in range(n_batch + 1))
    seg_of_tile = lambda i: jnp.minimum(i * ROW_TILE // t_lat, n_batch)

    c8 = jnp.zeros((8, d), F32).at[:n_batch].set(c).at[n_batch].set(c_ctx)
    mod_all = _modulation(c8, w_mod, b_mod)
    mod_all = mod_all.reshape(depth, 8, 6, d)[:, :n_batch + 1]
    mod_all = jnp.pad(mod_all, ((0, 0), (0, 0), (0, 2), (0, 0)))

    w_in_tok, w_in_chan = _inproj_weights(w_in)
    assert w_in_tok.shape[-1] == _IN_TOTAL and w_in_chan.shape[-2] == _IN_T_TOTAL
    gate_row = jax.vmap(lambda v: _lane_rows(v.reshape(1, -1), MISC_GATE, 1))(ml_gate_bias)
    gate_col = jnp.swapaxes(gate_row, 1, 2)
    ssd_a = -jnp.exp(ssd_a_log)
    ssd_row = jax.vmap(lambda bvec, avec: _lane_rows(jnp.stack([bvec.reshape(-1), avec.reshape(-1)]), MISC_DT))(
        ssd_dt_bias, ssd_a)
    ssd_col = jnp.swapaxes(ssd_row, 1, 2)
    ssd_cw = jnp.pad(ssd_conv_w, ((0, 0), (0, 8 - ssd_conv_w.shape[1]), (0, 0)))
    mix_cols = jnp.stack([ml_norm, jnp.repeat(ssd_d, SSD_HEADDIM, axis=-1), ssd_norm, s5_d], axis=-1)
    mix_cols = jnp.pad(mix_cols, ((0, 0), (0, 0), (0, 4)))
    ffn_cw = jnp.pad(ffn_conv_w, ((0, 0), (0, 8 - ffn_conv_w.shape[1]), (0, 0)))
    cos_l, sin_l, cos_c, sin_c = _rope_tables(t_lat, n_batch * t_ctx)
    cos = jnp.concatenate([cos_l] * n_batch + [cos_c], axis=0)
    sin = jnp.concatenate([sin_l] * n_batch + [sin_c], axis=0)
    toep, wz, wy, s5_decay = _s5_tables(s5_a_re, s5_a_im, s5_log_dt, s5_b_re, s5_b_im, s5_c_re, s5_c_im)
    s5_decay = s5_decay.reshape(depth, S5_NGROUPS, 8, LANES)
    s5_ca = s5_decay[:, :, 0:2].reshape(depth, 2 * S5_NGROUPS, LANES)
    s5_cb = s5_decay[:, :, 2:4].reshape(depth, 2 * S5_NGROUPS, LANES)
    n_s5_ctx = t_ctx // S5_CHUNK
    n_s5_lat = t_lat // S5_CHUNK

    xf = jnp.concatenate([x.reshape(n_lat, d), ctx.reshape(n_batch * t_ctx, d)], axis=0)
    for l in range(depth):
        mod = mod_all[l]
        q, k, misc, mlac, xbc, v1t, ogt, zt, ut = _in_projection(xf, mod, norm1[l][None], w_in_tok, w_in_chan, l,
                                                                 seg_of_tile)
        hf, hb = _mlstm(q, k, v1t, misc, gate_col[l], n_batch, n_scan)
        wqa, wqb, wka, wvt = _mla_weights(mla_w_uq[l], mla_w_ukv[l])
        qa, ka, vta = _mla_prep(mlac, misc, cos, sin, mla_q_norm[l][None], mla_kv_norm[l][None],
                                wqa, wqb, wka, wvt)
        att_lat, att_ctx = _attention(qa, ka, vta, n_batch, t_lat, t_ctx)
        act, xst = _ssd_prep(xbc, ssd_cw[l], ssd_conv_b[l][None], bounds)
        yf, yb = _ssd_scan(act, xst, misc, ssd_col[l], n_batch, n_scan)
        y5 = _s5_mix(ut, toep, wz, wy, s5_ca[l], s5_cb[l], l, n_batch, n_s5_lat, n_s5_ctx)
        acts = (hf, hb, ogt, yf, yb, xst, zt, y5, ut)
        params = (mix_cols[l], s5_w_glu[l].T.astype(BF16), w_out[l].astype(BF16), norm2[l][None])
        xf, h2 = _out_projection(xf, mod, seg_of_tile, acts, att_lat, att_ctx, params)
        last = l == depth - 1
        xf = _conv_ffn(xf, mod, seg_of_tile, h2, ffn_w_up[l].astype(BF16), ffn_cw[l], ffn_w_down[l].astype(BF16),
                       bounds, n_lat if last else n_rows, final_norm[None] if last else None)
    return xf.reshape(n_batch, t_lat, d)
```

```python
import functools
import math

import numpy as np
import jax
import jax.numpy as jnp
from jax import lax
from jax.experimental import pallas as pl
from jax.experimental.pallas import tpu as pltpu

F32 = jnp.float32
BF16 = jnp.bfloat16
HI = lax.Precision.HIGHEST

D_MODEL = 1024
W_GROUP = 256
EPS = 1e-6
GRID_W = 64
ROPE_BASE = 10000.0

ML_HEADS = 4
ML_DH = 64

MLA_HEADS = 4
MLA_NOPE = 64
MLA_ROPE = 32
MLA_V = 64
MLA_Q_LORA = 256
MLA_KV_LORA = 128
MLA_VROWS = 80

SSD_HEADS = 4
SSD_HEADDIM = 64
SSD_GROUPS = 2
SSD_STATE = 128

S5_GROUP = 16
S5_NGROUPS = 16
S5_STATE = 64
S5_CHUNK = 64
S5_SW = 4 * S5_STATE

D_FF = 2816
FF_CHUNK = 256

ROW_TILE = 512
SCAN_CHUNK = 256
ATTN_TQ = 512
ATTN_TK = 512
HALO = 16
LANES = 128
VMEM_LIMIT = 56 * 1024 * 1024

MISC_GATE = 0
MISC_DT = 16
MISC_KR = 24


def _cparams(sem):
    return pltpu.CompilerParams(dimension_semantics=sem, vmem_limit_bytes=VMEM_LIMIT)


def _dot(a, b):
    return jnp.dot(a, b, preferred_element_type=F32)


def _dot_nt(a, b):
    return lax.dot_general(a, b, (((1,), (1,)), ((), ())), preferred_element_type=F32)


def _dot_tn(a, b):
    return lax.dot_general(a, b, (((0,), (0,)), ((), ())), preferred_element_type=F32)


def _dot_hi(a, b):
    return jnp.dot(a, b, preferred_element_type=F32, precision=HI)


def _sigmoid(x):
    return 1.0 / (1.0 + jnp.exp(-x))


def _silu(x):
    return x * _sigmoid(x)


def _log_sigmoid(x):
    return jnp.minimum(x, 0.0) - jnp.log(1.0 + jnp.exp(-jnp.abs(x)))


def _softplus(x):
    return jnp.maximum(x, 0.0) + jnp.log(1.0 + jnp.exp(-jnp.abs(x)))


def _gelu_tanh(x):
    return 0.5 * x * (1.0 + jnp.tanh(math.sqrt(2.0 / math.pi) * (x + 0.044715 * x * x * x)))


def _rms(x, g):
    return x * lax.rsqrt(jnp.mean(x * x, axis=-1, keepdims=True) + EPS) * g


def _segment_masks(row0, n_rows, bounds):
    r = row0 + lax.broadcasted_iota(jnp.int32, (n_rows, 1), 0)
    first = r == bounds[0]
    last = r == bounds[1] - 1
    for s in bounds[1:-1]:
        first = first | (r == s)
    for e in bounds[2:]:
        last = last | (r == e - 1)
    return first, last


def _mod_kernel(c_ref, w_ref, b_ref, o_ref):
    c = c_ref[...]
    s = _silu(c).astype(BF16)
    o_ref[...] = _dot(s, w_ref[...].astype(BF16)) + b_ref[...]


def _modulation(c8, w_mod, b_mod):
    depth, d, d6 = w_mod.shape
    return pl.pallas_call(
        _mod_kernel,
        out_shape=jax.ShapeDtypeStruct((depth, 8, d6), F32),
        grid=(depth, d6 // d),
        in_specs=[
            pl.BlockSpec((8, d), lambda l, j: (0, 0)),
            pl.BlockSpec((None, d, d), lambda l, j: (l, 0, j)),
            pl.BlockSpec((None, 1, d), lambda l, j: (l, 0, j)),
        ],
        out_specs=pl.BlockSpec((None, 8, d), lambda l, j: (l, 0, j)),
        compiler_params=_cparams(("arbitrary", "arbitrary")),
        name="modulation",
    )(c8, w_mod, b_mod.reshape(depth, 1, d6))


_IN_SPLITS = (("q", 256, BF16), ("k", 256, BF16), ("misc", 128, F32), ("mlac", 384, F32), ("xbc", 768, F32))
_IN_TOTAL = sum(s[1] for s in _IN_SPLITS)
_IN_T_SPLITS = (("v1t", ML_HEADS * LANES, BF16), ("ogt", 256, F32), ("zt", 256, F32), ("ut", 256, F32))
_IN_T_TOTAL = sum(s[1] for s in _IN_T_SPLITS)


def _inproj_weights(w_in):
    ml, mla, ssd, s5 = 0, 1040, 1456, 2488
    cols = lambda a, b: w_in[..., a:b]
    zeros = lambda n: jnp.zeros(w_in.shape[:-1] + (n,), w_in.dtype)
    tok = [cols(ml, ml + 256), cols(ml + 256, ml + 512) * ML_DH ** -0.5]
    tok += [cols(ml + 1024, ml + 1040), cols(ssd + 1024, ssd + 1032),
            cols(mla + 384, mla + 416), zeros(LANES - 56)]
    tok += [cols(mla, mla + 384), cols(ssd + 256, ssd + 1024)]
    chan = []
    for h in range(ML_HEADS):
        chan += [cols(ml + 512 + h * 64, ml + 512 + (h + 1) * 64), zeros(64)]
    chan += [cols(ml + 768, ml + 1024), cols(ssd, ssd + 256), cols(s5, s5 + 256)]
    w_tok = jnp.concatenate(tok, axis=-1).astype(BF16)
    w_chan = jnp.swapaxes(jnp.concatenate(chan, axis=-1), -1, -2).astype(BF16)
    return w_tok, w_chan


def _inproj_kernel(x_ref, mod_ref, g_ref, w_ref, wt_ref, cos_ref, sin_ref, qn_ref, kvn_ref,
                   wqa_ref, wqb_ref, wka_ref, wvt_ref,
                   q_ref, k_ref, misc_ref, xbc_ref, v1t_ref, ogt_ref, zt_ref, ut_ref, qa_ref, ka_ref, vta_ref):
    x = x_ref[...]
    h = _rms(x, g_ref[...]) * (1.0 + mod_ref[1:2, :]) + mod_ref[0:1, :]
    hb = h.astype(BF16)
    off = 0
    tok = {}
    for (name, width, _), o_ref in zip(_IN_SPLITS, (q_ref, k_ref, misc_ref, None, xbc_ref)):
        tok[name] = _dot(hb, w_ref[:, off:off + width])
        if o_ref is not None:
            o_ref[...] = tok[name].astype(o_ref.dtype)
        off += width
    _mla_project(tok["mlac"], tok["misc"], cos_ref[...], sin_ref[...], qn_ref[...], kvn_ref[...],
                 wqa_ref, wqb_ref, wka_ref, wvt_ref, qa_ref, ka_ref, vta_ref)
    off = 0
    for (name, width, _), o_ref in zip(_IN_T_SPLITS, (v1t_ref, ogt_ref, zt_ref, ut_ref)):
        y = _dot_nt(wt_ref[off:off + width, :], hb)
        if name == "v1t":
            row = lax.broadcasted_iota(jnp.int32, (width, 1), 0)
            y = y + jnp.where((row & (LANES - 1)) == ML_DH, 1.0, 0.0)
        o_ref[...] = y.astype(o_ref.dtype)
        off += width


def _in_projection(x, mod, g, w, wt, layer, seg_of_tile, mla):
    r, d = x.shape
    row = lambda i: (i, 0)
    col = lambda i: (0, i)
    full = lambda a: pl.BlockSpec(a.shape, lambda i: (0, 0))
    tok_out = [s for s in _IN_SPLITS if s[0] != "mlac"]
    hw = MLA_HEADS * LANES
    vr = MLA_HEADS * MLA_VROWS
    cos, sin = mla[0], mla[1]
    return pl.pallas_call(
        _inproj_kernel,
        out_shape=[jax.ShapeDtypeStruct((r, width), dt) for _, width, dt in tok_out]
        + [jax.ShapeDtypeStruct((width, r), dt) for _, width, dt in _IN_T_SPLITS]
        + [jax.ShapeDtypeStruct((r, hw), BF16)] * 2 + [jax.ShapeDtypeStruct((vr, r), BF16)],
        grid=(r // ROW_TILE,),
        in_specs=[
            pl.BlockSpec((ROW_TILE, d), row),
            pl.BlockSpec((None, 8, d), lambda i: (seg_of_tile(i), 0, 0)),
            pl.BlockSpec((1, d), lambda i: (0, 0)),
            pl.BlockSpec((None, d, _IN_TOTAL), lambda i: (layer, 0, 0)),
            pl.BlockSpec((None, _IN_T_TOTAL, d), lambda i: (layer, 0, 0)),
            pl.BlockSpec((ROW_TILE, LANES), row), pl.BlockSpec((ROW_TILE, LANES), row),
        ] + [full(a) for a in mla[2:]],
        out_specs=[pl.BlockSpec((ROW_TILE, width), row) for _, width, _ in tok_out]
        + [pl.BlockSpec((width, ROW_TILE), col) for _, width, _ in _IN_T_SPLITS]
        + [pl.BlockSpec((ROW_TILE, hw), row)] * 2 + [pl.BlockSpec((vr, ROW_TILE), col)],
        compiler_params=_cparams(("arbitrary",)),
        name="in_projection",
    )(x, mod, g, w, wt, cos, sin, *mla[2:])


def _tri_masks(n):
    row = lax.broadcasted_iota(jnp.int32, (n, n), 0)
    col = lax.broadcasted_iota(jnp.int32, (n, n), 1)
    return col <= row, col >= row


def _col_forms(rows16, n):
    return jnp.concatenate([rows16, jnp.zeros((LANES - rows16.shape[0], n), F32)], axis=0).T


def _mlstm_kernel(qf_ref, kf_ref, vf_ref, mf_ref, qb_ref, kb_ref, vb_ref, mb_ref,
                  gbc_ref, hf_ref, hb_ref, st_ref, m_ref):
    n = qf_ref.shape[0]

    @pl.when(pl.program_id(1) == 0)
    def _():
        st_ref[...] = jnp.zeros_like(st_ref)
        m_ref[...] = jnp.zeros_like(m_ref)

    tril, triu = _tri_masks(n)
    dirs = ((qf_ref, kf_ref, vf_ref, mf_ref, hf_ref, triu, n - 1),
            (qb_ref, kb_ref, vb_ref, mb_ref, hb_ref, tril, 0))
    pending = []
    for d, (q_ref, k_ref, vt_ref, misc_ref, out_ref, mask, last) in enumerate(dirs):
        gt = misc_ref[...].T + gbc_ref[...]
        brow = _dot_hi(_log_sigmoid(gt[0:16, :]), mask.astype(F32))
        ccol = _col_forms(gt[0:16, :] - pltpu.roll(brow, 16 - ML_HEADS, 0), n)
        for h in range(ML_HEADS):
            hd = d * ML_HEADS + h
            i_idx = MISC_GATE + 8 * d + h
            f_idx = i_idx + ML_HEADS
            ig_row = gt[i_idx:i_idx + 1, :]
            b_row = brow[f_idx:f_idx + 1, :]
            c_col = ccol[:, i_idx:i_idx + 1]
            b_last = b_row[:, last:last + 1]
            m_prev = m_ref[hd:hd + 1, 0:1]
            q = q_ref[:, h * ML_DH:(h + 1) * ML_DH]
            k = k_ref[:, h * ML_DH:(h + 1) * ML_DH]
            v1t = vt_ref[h * LANES:(h + 1) * LANES, :]
            state = st_ref[hd]

            qk = _dot_nt(k, q)
            carried = _dot_nt(state.astype(BF16), q)
            w_log = b_last - b_row + ig_row
            m_new = jnp.maximum(b_last + m_prev, jnp.max(w_log, axis=1, keepdims=True))
            decay = jnp.exp(b_last + m_prev - m_new)
            vw = (v1t.astype(F32) * jnp.exp(w_log - m_new)).astype(BF16)
            st_ref[hd] = decay * state + _dot(vw, k)
            m_ref[hd:hd + 1, :] = jnp.broadcast_to(m_new, (1, LANES))
            pending.append((out_ref, h, mask, b_row, c_col, m_prev, qk, carried, v1t))

    for out_ref, h, mask, b_row, c_col, m_prev, qk, carried, v1t in pending:
        dmat = jnp.where(mask, b_row + c_col, -jnp.inf)
        inter = b_row + m_prev
        m_t = jnp.maximum(inter, jnp.max(dmat, axis=0, keepdims=True))
        s = qk * jnp.exp(dmat - m_t)
        tot = _dot(v1t, s.astype(BF16)) + jnp.exp(inter - m_t) * carried
        den = tot[ML_DH:ML_DH + 1, :]
        out_ref[h * ML_DH:(h + 1) * ML_DH, :] = tot[0:ML_DH, :] / jnp.maximum(jnp.abs(den), jnp.exp(-m_t))


def _scan_index_maps(n_batch, n_chunks):
    ctx0 = n_batch * n_chunks
    fwd = lambda b, s: (jnp.where(s == 0, ctx0 + b, b * n_chunks + s - 1), 0)
    bwd = lambda b, s: (jnp.where(s == 0, ctx0 + b, b * n_chunks + n_chunks - s), 0)
    return fwd, bwd


def _mlstm(q, k, v1t, misc, gb_col, n_batch, n_chunks):
    r = q.shape[0]
    n = SCAN_CHUNK
    fwd, bwd = _scan_index_maps(n_batch, n_chunks)
    const = lambda b, s: (0, 0)
    specs = []
    for imap in (fwd, bwd):
        cols = lambda b, s, imap=imap: imap(b, s)[::-1]
        specs += [pl.BlockSpec((n, W_GROUP), imap), pl.BlockSpec((n, W_GROUP), imap),
                  pl.BlockSpec((ML_HEADS * LANES, n), cols), pl.BlockSpec((n, LANES), imap)]
    specs += [pl.BlockSpec((LANES, 1), const)]
    fwd_c = lambda b, s: fwd(b, s)[::-1]
    bwd_c = lambda b, s: bwd(b, s)[::-1]
    return pl.pallas_call(
        _mlstm_kernel,
        out_shape=[jax.ShapeDtypeStruct((W_GROUP, r), F32)] * 2,
        grid=(n_batch, n_chunks + 1),
        in_specs=specs,
        out_specs=[pl.BlockSpec((W_GROUP, n), fwd_c), pl.BlockSpec((W_GROUP, n), bwd_c)],
        scratch_shapes=[pltpu.VMEM((2 * ML_HEADS, LANES, ML_DH), F32),
                        pltpu.VMEM((2 * ML_HEADS, LANES), F32)],
        compiler_params=_cparams(("arbitrary", "arbitrary")),
        name="mlstm_scan",
    )(q, k, v1t, misc, q, k, v1t, misc, gb_col)


def _ssd_prep_kernel(x_ref, xp_ref, xn_ref, w_ref, b_ref, act_ref, xs_ref, *, bounds):
    tm = x_ref.shape[0]
    first, last = _segment_masks(pl.program_id(0) * tm, tm, bounds)
    x = x_ref[...]
    rid = lax.broadcasted_iota(jnp.int32, (tm, 1), 0)
    xp = jnp.where(rid == 0, xp_ref[HALO - 1:HALO, :], pltpu.roll(x, 1, 0))
    xn = jnp.where(rid == tm - 1, xn_ref[0:1, :], pltpu.roll(x, tm - 1, 0))
    xp = jnp.where(first, 0.0, xp)
    xn = jnp.where(last, 0.0, xn)
    y = _silu(w_ref[0:1, :] * xp + w_ref[1:2, :] * x + w_ref[2:3, :] * xn + b_ref[...])
    act_ref[...] = y.astype(act_ref.dtype)
    xs_ref[...] = y[:, 0:W_GROUP].T


def _halo_specs(width, n_rows):
    per = ROW_TILE // HALO
    n_blocks = n_rows // HALO
    prev = pl.BlockSpec((HALO, width), lambda i: (jnp.maximum(i * per - 1, 0), 0))
    nxt = pl.BlockSpec((HALO, width), lambda i: (jnp.minimum((i + 1) * per, n_blocks - 1), 0))
    return prev, nxt


def _ssd_prep(xbc, conv_w, conv_b, bounds):
    r, width = xbc.shape
    row = lambda i: (i, 0)
    prev, nxt = _halo_specs(width, r)
    return pl.pallas_call(
        functools.partial(_ssd_prep_kernel, bounds=bounds),
        out_shape=[jax.ShapeDtypeStruct((r, width), BF16), jax.ShapeDtypeStruct((W_GROUP, r), F32)],
        grid=(r // ROW_TILE,),
        in_specs=[pl.BlockSpec((ROW_TILE, width), row), prev, nxt,
                  pl.BlockSpec((8, width), lambda i: (0, 0)),
                  pl.BlockSpec((1, width), lambda i: (0, 0))],
        out_specs=[pl.BlockSpec((ROW_TILE, width), row), pl.BlockSpec((W_GROUP, ROW_TILE), lambda i: (0, i))],
        compiler_params=_cparams(("arbitrary",)),
        name="ssd_conv",
    )(xbc, xbc, xbc, conv_w, conv_b)


def _ssd_kernel(af_ref, xf_ref, mf_ref, ab_ref, xb_ref, mb_ref, pc_ref, yf_ref, yb_ref, st_ref):
    n = af_ref.shape[0]

    @pl.when(pl.program_id(1) == 0)
    def _():
        st_ref[...] = jnp.zeros_like(st_ref)

    tril, triu = _tri_masks(n)
    dirs = ((af_ref, xf_ref, mf_ref, yf_ref, triu, n - 1),
            (ab_ref, xb_ref, mb_ref, yb_ref, tril, 0))
    pending = []
    for d, (act_ref, xt_ref, misc_ref, out_ref, mask, last) in enumerate(dirs):
        dt_r = _softplus(misc_ref[...].T + pc_ref[:, 0:1])
        cs_r = _dot_hi((dt_r * pc_ref[:, 1:2])[0:32, :], mask.astype(F32))
        ncs_c = _col_forms(-cs_r, n)
        for g in range(SSD_GROUPS):
            bm = act_ref[:, W_GROUP + g * SSD_STATE:W_GROUP + (g + 1) * SSD_STATE]
            cm = act_ref[:, W_GROUP + SSD_GROUPS * SSD_STATE + g * SSD_STATE:
                         W_GROUP + SSD_GROUPS * SSD_STATE + (g + 1) * SSD_STATE]
            gmat = _dot_nt(bm, cm)
            for hh in range(SSD_HEADS // SSD_GROUPS):
                h = g * (SSD_HEADS // SSD_GROUPS) + hh
                hd = d * SSD_HEADS + h
                idx = MISC_DT + d * SSD_HEADS + h
                cs_row = cs_r[idx:idx + 1, :]
                ncs_col = ncs_c[:, idx:idx + 1]
                cs_last = cs_row[:, last:last + 1]
                dt_row = dt_r[idx:idx + 1, :]
                xt = xt_ref[h * SSD_HEADDIM:(h + 1) * SSD_HEADDIM, :]
                state = st_ref[hd]
                carried = _dot_nt(state.astype(BF16), cm)
                xw = (xt * (jnp.exp(cs_last - cs_row) * dt_row)).astype(BF16)
                st_ref[hd] = jnp.exp(cs_last) * state + _dot(xw, bm)
                pending.append((out_ref, h, mask, cs_row, ncs_col, gmat, carried, (xt * dt_row).astype(BF16)))

    for out_ref, h, mask, cs_row, ncs_col, gmat, carried, xdt in pending:
        decay = jnp.exp(jnp.where(mask, cs_row + ncs_col, -jnp.inf))
        y = _dot(xdt, (gmat * decay).astype(BF16)) + jnp.exp(cs_row) * carried
        out_ref[h * SSD_HEADDIM:(h + 1) * SSD_HEADDIM, :] = y


def _ssd_scan(act, xst, misc, p_col, n_batch, n_chunks):
    r, width = act.shape
    n = SCAN_CHUNK
    fwd, bwd = _scan_index_maps(n_batch, n_chunks)
    const = lambda b, s: (0, 0)
    fwd_c = lambda b, s: fwd(b, s)[::-1]
    bwd_c = lambda b, s: bwd(b, s)[::-1]
    specs = []
    for imap, cmap in ((fwd, fwd_c), (bwd, bwd_c)):
        specs += [pl.BlockSpec((n, width), imap), pl.BlockSpec((W_GROUP, n), cmap), pl.BlockSpec((n, LANES), imap)]
    specs += [pl.BlockSpec((LANES, 8), const)]
    return pl.pallas_call(
        _ssd_kernel,
        out_shape=[jax.ShapeDtypeStruct((W_GROUP, r), F32)] * 2,
        grid=(n_batch, n_chunks + 1),
        in_specs=specs,
        out_specs=[pl.BlockSpec((W_GROUP, n), fwd_c), pl.BlockSpec((W_GROUP, n), bwd_c)],
        scratch_shapes=[pltpu.VMEM((2 * SSD_HEADS, SSD_HEADDIM, SSD_STATE), F32)],
        compiler_params=_cparams(("arbitrary", "arbitrary")),
        name="ssd_scan",
    )(act, xst, misc, act, xst, misc, p_col)


def _mla_project(c, misc, cos, sin, qn, kvn, wqa_ref, wqb_ref, wka_ref, wvt_ref, q_ref, k_ref, vt_ref):
    scale = (MLA_NOPE + MLA_ROPE) ** -0.5 * math.log2(math.e)
    half = MLA_ROPE // 2
    cq = _rms(c[:, 0:MLA_Q_LORA], qn).astype(BF16)
    ckv = _rms(c[:, MLA_Q_LORA:MLA_Q_LORA + MLA_KV_LORA], kvn).astype(BF16)
    qa = _dot(cq, wqa_ref[...])
    qb = _dot(cq, wqb_ref[...])
    ka = _dot(ckv, wka_ref[...])
    lane = lax.broadcasted_iota(jnp.int32, (1, LANES), 1)
    in_rope = (lane >= MLA_NOPE) & (lane < MLA_NOPE + MLA_ROPE)
    kr_a = pltpu.roll(misc, MLA_NOPE - MISC_KR, 1)
    kr_b = jnp.where(lane < MLA_NOPE + half,
                     -pltpu.roll(misc, MLA_NOPE - MISC_KR - half, 1),
                     pltpu.roll(misc, MLA_NOPE - MISC_KR + half, 1))
    kr = jnp.where(in_rope, kr_a * cos + kr_b * sin, 0.0)
    for h in range(MLA_HEADS):
        sl = slice(h * LANES, (h + 1) * LANES)
        q_ref[:, sl] = ((qa[:, sl] * cos + qb[:, sl] * sin) * scale).astype(BF16)
        k_ref[:, sl] = (ka[:, sl] + kr).astype(BF16)
    vt = _dot_nt(wvt_ref[...], ckv)
    row = lax.broadcasted_iota(jnp.int32, (vt.shape[0], 1), 0)
    ones_row = row == MLA_V
    for h in range(1, MLA_HEADS):
        ones_row = ones_row | (row == h * MLA_VROWS + MLA_V)
    vt_ref[...] = (vt + jnp.where(ones_row, 1.0, 0.0)).astype(BF16)


_HEAD_LANES = tuple(slice(h * LANES, (h + 1) * LANES) for h in range(MLA_HEADS))
_HEAD_VROWS = tuple(slice(h * MLA_VROWS, (h + 1) * MLA_VROWS) for h in range(MLA_HEADS))


def _attn_step(q_ref, m_ref, acc_ref, scores=None, update=None):
    new = []
    for h in range(MLA_HEADS):
        if scores is not None:
            k_ref, rows, n_keys, (s_ref, mx_ref) = scores
            st = _dot_nt(k_ref[rows, _HEAD_LANES[h]], q_ref[:, _HEAD_LANES[h]])
            s_ref[h, 0:n_keys, :] = st
            mx_ref[h] = jnp.max(st, axis=0, keepdims=True)
        if update is not None:
            (s_ref, mx_ref), n_keys, vt_ref, cols = update
            m_old = m_ref[h]
            m_new = jnp.maximum(m_old, mx_ref[h])
            p = jnp.exp2(s_ref[h, 0:n_keys, :] - m_new).astype(BF16)
            new.append((m_new, jnp.exp2(m_old - m_new) * acc_ref[h] + _dot(vt_ref[_HEAD_VROWS[h], cols], p)))
    for h, (m_new, acc) in enumerate(new):
        m_ref[h] = m_new
        acc_ref[h] = acc


def _attn_init(m_ref, acc_ref):
    m_ref[...] = jnp.full_like(m_ref, -jnp.inf)
    acc_ref[...] = jnp.zeros_like(acc_ref)


def _attn_finish(o_ref, acc_ref):
    for h in range(MLA_HEADS):
        acc = acc_ref[h]
        o_ref[h * MLA_V:(h + 1) * MLA_V, :] = (acc[0:MLA_V] / acc[MLA_V:MLA_V + 1]).astype(o_ref.dtype)


def _attn_lat_kernel(q_ref, kl_ref, vlt_ref, kc_ref, vct_ref, o_ref, m_ref, acc_ref,
                     sa_s_ref, sa_m_ref, sb_s_ref, sb_m_ref):
    sa_ref = (sa_s_ref, sa_m_ref)
    sb_ref = (sb_s_ref, sb_m_ref)
    tk = ATTN_TK
    n_pairs = kl_ref.shape[0] // (2 * tk)
    n_ctx = kc_ref.shape[0]
    chunk = lambda c: pl.ds(pl.multiple_of(c * tk, tk), tk)
    step = functools.partial(_attn_step, q_ref, m_ref, acc_ref)
    _attn_init(m_ref, acc_ref)
    step(scores=(kl_ref, chunk(0), tk, sa_ref))

    def body(i, carry):
        c = 2 * i
        step(scores=(kl_ref, chunk(c + 1), tk, sb_ref), update=(sa_ref, tk, vlt_ref, chunk(c)))
        step(scores=(kl_ref, chunk(c + 2), tk, sa_ref), update=(sb_ref, tk, vlt_ref, chunk(c + 1)))
        return carry

    lax.fori_loop(0, n_pairs - 1, body, 0)
    c = 2 * (n_pairs - 1)
    step(scores=(kl_ref, chunk(c + 1), tk, sb_ref), update=(sa_ref, tk, vlt_ref, chunk(c)))
    step(scores=(kc_ref, slice(None), n_ctx, sa_ref), update=(sb_ref, tk, vlt_ref, chunk(c + 1)))
    step(update=(sa_ref, n_ctx, vct_ref, slice(None)))
    _attn_finish(o_ref, acc_ref)


def _attn_ctx_kernel(q_ref, kc_ref, vct_ref, o_ref, m_ref, acc_ref, s_ref, mx_ref):
    n_ctx = kc_ref.shape[0]
    _attn_init(m_ref, acc_ref)
    _attn_step(q_ref, m_ref, acc_ref, scores=(kc_ref, slice(None), n_ctx, (s_ref, mx_ref)))
    _attn_step(q_ref, m_ref, acc_ref, update=((s_ref, mx_ref), n_ctx, vct_ref, slice(None)))
    _attn_finish(o_ref, acc_ref)


def _attention(q, k, vt, n_batch, t_lat, t_ctx):
    r, hw = q.shape
    vr = vt.shape[0]
    nq = t_lat // ATTN_TQ
    ctx_blk = n_batch * t_lat // t_ctx
    assert t_lat % (2 * ATTN_TK) == 0 and t_ctx <= ATTN_TK
    scratch = lambda tq: [pltpu.VMEM((MLA_HEADS, 1, tq), F32), pltpu.VMEM((MLA_HEADS, MLA_VROWS, tq), F32)]
    scores = lambda tk, tq: [pltpu.VMEM((MLA_HEADS, tk, tq), F32), pltpu.VMEM((MLA_HEADS, 1, tq), F32)]
    lat = pl.pallas_call(
        _attn_lat_kernel,
        out_shape=jax.ShapeDtypeStruct((W_GROUP, n_batch * t_lat), BF16),
        grid=(n_batch, nq),
        in_specs=[pl.BlockSpec((ATTN_TQ, hw), lambda b, i: (b * nq + i, 0)),
                  pl.BlockSpec((t_lat, hw), lambda b, i: (b, 0)),
                  pl.BlockSpec((vr, t_lat), lambda b, i: (0, b)),
                  pl.BlockSpec((t_ctx, hw), lambda b, i: (ctx_blk + b, 0)),
                  pl.BlockSpec((vr, t_ctx), lambda b, i: (0, ctx_blk + b))],
        out_specs=pl.BlockSpec((W_GROUP, ATTN_TQ), lambda b, i: (0, b * nq + i)),
        scratch_shapes=scratch(ATTN_TQ) + scores(ATTN_TK, ATTN_TQ) * 2,
        compiler_params=_cparams(("arbitrary", "arbitrary")),
        name="mla_attention_latent",
    )(q, k, vt, k, vt)
    ctx = pl.pallas_call(
        _attn_ctx_kernel,
        out_shape=jax.ShapeDtypeStruct((W_GROUP, n_batch * t_ctx), BF16),
        grid=(n_batch,),
        in_specs=[pl.BlockSpec((t_ctx, hw), lambda b: (ctx_blk + b, 0)),
                  pl.BlockSpec((t_ctx, hw), lambda b: (ctx_blk + b, 0)),
                  pl.BlockSpec((vr, t_ctx), lambda b: (0, ctx_blk + b))],
        out_specs=pl.BlockSpec((W_GROUP, t_ctx), lambda b: (0, b)),
        scratch_shapes=scratch(t_ctx) + scores(t_ctx, t_ctx),
        compiler_params=_cparams(("arbitrary",)),
        name="mla_attention_context",
    )(q, k, vt)
    return lat, ctx


def _cis_pow(lr, li, dt, k):
    mag = jnp.exp(lr * dt * k)
    ang = li * dt * k
    return mag * jnp.cos(ang), mag * jnp.sin(ang)


def _s5_tables_kernel(pr_ref, pc_ref, btre_ref, btim_ref, ctre_ref, ctim_ref,
                      toep_ref, wz_ref, wy_ref, decay_ref):
    lc, ng, ns = S5_CHUNK, S5_GROUP, S5_STATE
    lr_r = jnp.minimum(pr_ref[0:1, :], -1e-4)
    li_r = pr_ref[1:2, :]
    dt_r = jnp.exp(pr_ref[2:3, :])
    lr_c = jnp.minimum(pc_ref[:, 0:1], -1e-4)
    li_c = pc_ref[:, 1:2]
    dt_c = jnp.exp(pc_ref[:, 2:3])
    lane4 = lax.broadcasted_iota(jnp.int32, (1, S5_SW), 1)
    odd_lane = (lane4 & ns) != 0
    row4 = lax.broadcasted_iota(jnp.int32, (S5_SW, 1), 0)
    odd_row = (row4 & ns) != 0

    ab_re, ab_im = _cis_pow(lr_r, li_r, dt_r, 1.0)
    den = lr_r * lr_r + li_r * li_r
    f_re = ((ab_re - 1.0) * lr_r + ab_im * li_r) / den
    f_im = (ab_im * lr_r - (ab_re - 1.0) * li_r) / den
    bb_re = f_re * btre_ref[...] - f_im * btim_ref[...]
    bb_im = f_re * btim_ref[...] + f_im * btre_ref[...]

    s_col = lax.broadcasted_iota(jnp.int32, (lc, 1), 0)
    expo = jnp.where(lane4 < 2 * ns, lc - 1 - s_col, s_col).astype(F32)
    pz_re, pz_im = _cis_pow(lr_r, li_r, dt_r, expo)
    xz = jnp.where(odd_lane, bb_im, bb_re)
    yz = jnp.where(odd_lane, bb_re, -bb_im)
    for j in range(ng):
        wz_ref[j] = (pz_re * xz[j:j + 1, :] + pz_im * yz[j:j + 1, :]).astype(BF16)

    al_re, al_im = _cis_pow(lr_r, li_r, dt_r, float(lc))
    al_sw = jnp.where(odd_lane, al_im, -al_im)
    decay_ref[...] = jnp.zeros_like(decay_ref)
    for d in range(2):
        decay_ref[d:d + 1, :] = al_re[:, d * LANES:(d + 1) * LANES]
        decay_ref[2 + d:3 + d, :] = al_sw[:, d * LANES:(d + 1) * LANES]

    t_lane = lax.broadcasted_iota(jnp.int32, (1, LANES), 1) & (lc - 1)
    expo = jnp.where(row4 < 2 * ns, t_lane + 1, lc - t_lane).astype(F32)
    py_re, py_im = _cis_pow(lr_c, li_c, dt_c, expo)
    reps = lc * ng // LANES
    py_re = jnp.tile(py_re, (1, reps))
    py_im = jnp.tile(py_im, (1, reps))
    low_half = lax.broadcasted_iota(jnp.int32, (1, 2 * lc), 1) < lc

    def expand_half(c):
        bc = lambda j: jnp.broadcast_to(c[:, j:j + 1], (c.shape[0], 2 * lc))
        return jnp.concatenate([jnp.where(low_half, bc(2 * p), bc(2 * p + 1)) for p in range(ng // 2)], axis=1)

    def expand_full(c):
        return jnp.concatenate([jnp.broadcast_to(c[:, j:j + 1], (c.shape[0], 2 * lc)) for j in range(ng)], axis=1)

    ce_re = jnp.tile(expand_half(ctre_ref[0:ns, :]), (S5_SW // ns, 1))
    ce_im = jnp.tile(expand_half(ctim_ref[0:ns, :]), (S5_SW // ns, 1))
    c2_re = expand_full(ctre_ref[0:ns, :])
    c2_im = expand_full(ctim_ref[0:ns, :])
    wy_ref[...] = jnp.where(odd_row, -(ce_re * py_im + ce_im * py_re),
                            ce_re * py_re - ce_im * py_im).astype(BF16)

    m_lane = lax.broadcasted_iota(jnp.int32, (1, 2 * lc), 1)
    xmat = None
    for d in range(2):
        rows = slice(d * 2 * ns, d * 2 * ns + ns)
        lag = m_lane - (lc - 1) if d == 0 else (lc - 1) - m_lane
        valid = (lag >= 0) & (m_lane < 2 * lc - 1)
        pm_re, pm_im = _cis_pow(lr_c[rows], li_c[rows], dt_c[rows], jnp.where(valid, lag, 0).astype(F32))
        pm_re = jnp.tile(jnp.where(valid, pm_re, 0.0), (1, ng))
        pm_im = jnp.tile(jnp.where(valid, pm_im, 0.0), (1, ng))
        p_re = c2_re * pm_re - c2_im * pm_im
        p_im = c2_re * pm_im + c2_im * pm_re
        lanes = slice(d * 2 * ns, d * 2 * ns + ns)
        term = _dot_hi(bb_re[:, lanes], p_re) - _dot_hi(bb_im[:, lanes], p_im)
        xmat = term if xmat is None else xmat + term

    for j in range(ng):
        xb = jnp.broadcast_to(xmat[j:j + 1, :], (lc, 2 * lc * ng))
        cols = []
        for p in range(ng // 2):
            even = xb[:, (2 * p) * 2 * lc:(2 * p + 1) * 2 * lc]
            odd = xb[:, (2 * p + 1) * 2 * lc:(2 * p + 2) * 2 * lc]
            cols.append(jnp.where(low_half,
                                  pltpu.roll(even, lc + 1, 1, stride=1, stride_axis=0),
                                  pltpu.roll(odd, 1, 1, stride=1, stride_axis=0)))
        toep_ref[j] = jnp.concatenate(cols, axis=1).astype(BF16)


def _s5_tables(a_re, a_im, log_dt, b_re, b_im, c_re, c_im):
    depth = a_re.shape[0]
    g, n, j, lc = S5_NGROUPS, S5_STATE, S5_GROUP, S5_CHUNK
    n_all = depth * g

    def parts(v):
        v = jnp.transpose(v, (0, 2, 1, 3))
        return jnp.concatenate([v[:, :, 0], v[:, :, 0], v[:, :, 1], v[:, :, 1]], axis=-1).reshape(n_all, S5_SW)

    rows = jnp.stack([parts(a_re), parts(a_im), parts(jnp.broadcast_to(log_dt[..., None], a_re.shape))], axis=1)
    p_row = jnp.pad(rows, ((0, 0), (0, 5), (0, 0)))
    p_col = jnp.swapaxes(p_row, 1, 2)
    bt = lambda b: jnp.tile(jnp.swapaxes(b.reshape(n_all, n, j), 1, 2), (1, 1, 4))
    ct = lambda c: jnp.tile(jnp.swapaxes(c.reshape(n_all, j, n), 1, 2), (1, 4, 1))
    own = lambda i: (i, 0, 0)
    width = lc * j
    return pl.pallas_call(
        _s5_tables_kernel,
        out_shape=[jax.ShapeDtypeStruct((n_all * j, lc, width), BF16),
                   jax.ShapeDtypeStruct((n_all * j, lc, S5_SW), BF16),
                   jax.ShapeDtypeStruct((n_all, S5_SW, width), BF16),
                   jax.ShapeDtypeStruct((n_all, 8, LANES), F32)],
        grid=(n_all,),
        in_specs=[pl.BlockSpec((None, 8, S5_SW), own), pl.BlockSpec((None, S5_SW, 8), own),
                  pl.BlockSpec((None, j, S5_SW), own), pl.BlockSpec((None, j, S5_SW), own),
                  pl.BlockSpec((None, S5_SW, j), own), pl.BlockSpec((None, S5_SW, j), own)],
        out_specs=[pl.BlockSpec((j, lc, width), own), pl.BlockSpec((j, lc, S5_SW), own),
                   pl.BlockSpec((None, S5_SW, width), own), pl.BlockSpec((None, 8, LANES), own)],
        compiler_params=_cparams(("arbitrary",)),
        name="s5_tables",
    )(p_row, p_col, bt(b_re), bt(b_im), ct(c_re), ct(c_im))


def _s5_local_kernel(u_ref, wz_ref, z_ref):
    acc = None
    for j in range(S5_GROUP):
        term = _dot(u_ref[j].astype(BF16), wz_ref[j])
        acc = term if acc is None else acc + term
    z_ref[...] = acc


def _s5_scan_kernel(z_ref, ca_ref, cb_ref, x_ref, xf_ref, xb_ref, *, n_batch, n_lat, n_ctx):
    ca = ca_ref[...]
    cb = cb_ref[...]
    zero = jnp.zeros(ca.shape, F32)
    ctx0 = n_batch * n_lat

    def step(x, z):
        return x * ca + pltpu.roll(x, LANES // 2, 1) * cb + z

    def body(i, carry):
        new = []
        for b in range(n_batch):
            rf = jnp.where(i < n_ctx, ctx0 + b * n_ctx + i, b * n_lat + i - n_ctx)
            rb = jnp.where(i < n_ctx, ctx0 + b * n_ctx + n_ctx - 1 - i, b * n_lat + n_lat - 1 - (i - n_ctx))
            xf, xb = carry[2 * b], carry[2 * b + 1]
            xf_ref[rf] = xf
            xb_ref[rb] = xb
            new += [step(xf, z_ref[rf]), step(xb, z_ref[rb])]
        return tuple(new)

    lax.fori_loop(0, n_ctx + n_lat, body, (zero,) * (2 * n_batch))
    row = lax.broadcasted_iota(jnp.int32, x_ref.shape, 1)
    x_ref[...] = jnp.where((row & 1) == 0, xf_ref[...], xb_ref[...])


def _s5_out_kernel(u_ref, t_ref, x_ref, wy_ref, y_ref):
    acc = _dot(x_ref[...].astype(BF16), wy_ref[...])
    for j in range(S5_GROUP):
        acc = acc + _dot(u_ref[j].astype(BF16), t_ref[j])
    for j in range(S5_GROUP):
        y_ref[j] = acc[:, j * S5_CHUNK:(j + 1) * S5_CHUNK]


def _s5_mix(ut, toep, wz, wy, ca, cb, layer, n_batch, n_lat, n_ctx):
    g, j, lc = S5_NGROUPS, S5_GROUP, S5_CHUNK
    nr = ut.shape[1] // lc
    u3 = ut.reshape(W_GROUP, nr, lc)
    grp = lambda i: (i, 0, 0)
    tab = lambda i: (layer * g + i, 0, 0)
    z = pl.pallas_call(
        _s5_local_kernel,
        out_shape=jax.ShapeDtypeStruct((nr, g * S5_SW), F32),
        grid=(g,),
        in_specs=[pl.BlockSpec((j, nr, lc), grp), pl.BlockSpec((j, lc, S5_SW), tab)],
        out_specs=pl.BlockSpec((nr, S5_SW), lambda i: (0, i)),
        compiler_params=_cparams(("arbitrary",)),
        name="s5_local_state",
    )(u3, wz)
    z3 = z.reshape(nr, 2 * g, LANES)
    whole = lambda shp: pl.BlockSpec(shp, lambda: tuple(0 for _ in shp))
    x3 = pl.pallas_call(
        functools.partial(_s5_scan_kernel, n_batch=n_batch, n_lat=n_lat, n_ctx=n_ctx),
        out_shape=jax.ShapeDtypeStruct(z3.shape, F32),
        in_specs=[whole(z3.shape), whole(ca.shape), whole(cb.shape)],
        out_specs=whole(z3.shape),
        scratch_shapes=[pltpu.VMEM(z3.shape, F32), pltpu.VMEM(z3.shape, F32)],
        compiler_params=pltpu.CompilerParams(vmem_limit_bytes=VMEM_LIMIT),
        name="s5_chunk_scan",
    )(z3, ca, cb)
    x = x3.reshape(nr, g * S5_SW)
    y3 = pl.pallas_call(
        _s5_out_kernel,
        out_shape=jax.ShapeDtypeStruct((W_GROUP, nr, lc), F32),
        grid=(g,),
        in_specs=[pl.BlockSpec((j, nr, lc), grp), pl.BlockSpec((j, lc, lc * j), tab),
                  pl.BlockSpec((nr, S5_SW), lambda i: (0, i)),
                  pl.BlockSpec((None, S5_SW, lc * j), tab)],
        out_specs=pl.BlockSpec((j, nr, lc), grp),
        compiler_params=_cparams(("arbitrary",)),
        name="s5_output",
    )(u3, toep, x, wy)
    return y3.reshape(W_GROUP, nr * lc)


def _outproj_kernel(x_ref, mod_ref, hf_ref, hb_ref, og_ref, yf_ref, yb_ref, xs_ref, z_ref,
                    ut_ref, s5_ref, attl_ref, attc_ref, pcol_ref, glu_ref, wo_ref, n2_ref, xo_ref, h2_ref,
                    *, n_lat_tiles):
    h = hf_ref[...] + hb_ref[...]
    parts = []
    for i in range(ML_HEADS):
        hh = h[i * ML_DH:(i + 1) * ML_DH, :]
        parts.append(hh * lax.rsqrt(jnp.mean(hh * hh, axis=0, keepdims=True) + EPS))
    ya = jnp.concatenate(parts, axis=0) * pcol_ref[:, 0:1] * _sigmoid(og_ref[...])
    yc = (pcol_ref[:, 1:2] * xs_ref[...] + yf_ref[...] + yb_ref[...]) * _silu(z_ref[...])
    yc = yc * lax.rsqrt(jnp.mean(yc * yc, axis=0, keepdims=True) + EPS) * pcol_ref[:, 2:3]
    yd = _gelu_tanh(pcol_ref[:, 3:4] * ut_ref[...] + s5_ref[...])
    yd = yd * _sigmoid(_dot(glu_ref[...], yd.astype(BF16)))
    att = jnp.where(pl.program_id(0) < n_lat_tiles, attl_ref[...], attc_ref[...])
    mixed = jnp.concatenate([ya.astype(BF16), att, yc.astype(BF16), yd.astype(BF16)], axis=0)
    y = _dot_tn(mixed, wo_ref[...])
    x = x_ref[...] + mod_ref[2:3, :] * y
    xo_ref[...] = x
    h2 = _rms(x, n2_ref[...]) * (1.0 + mod_ref[4:5, :]) + mod_ref[3:4, :]
    h2_ref[...] = h2.astype(BF16)


def _out_projection(x, mod, seg_of_tile, acts, y5, att_lat, att_ctx, params):
    r, d = x.shape
    n_lat_tiles = att_lat.shape[1] // ROW_TILE
    assert att_ctx.shape[1] == ROW_TILE and r == (n_lat_tiles + 1) * ROW_TILE
    row = lambda i: (i, 0)
    full = lambda a: pl.BlockSpec(a.shape, lambda i: (0, 0))
    return pl.pallas_call(
        functools.partial(_outproj_kernel, n_lat_tiles=n_lat_tiles),
        out_shape=[jax.ShapeDtypeStruct((r, d), F32), jax.ShapeDtypeStruct((r, d), BF16)],
        grid=(r // ROW_TILE,),
        in_specs=[pl.BlockSpec((ROW_TILE, d), row),
                  pl.BlockSpec((None, 8, d), lambda i: (seg_of_tile(i), 0, 0))]
        + [pl.BlockSpec((W_GROUP, ROW_TILE), lambda i: (0, i)) for _ in acts]
        + [pl.BlockSpec((W_GROUP, ROW_TILE), lambda i: (0, i)),
           pl.BlockSpec((W_GROUP, ROW_TILE), lambda i: (0, jnp.minimum(i, n_lat_tiles - 1))),
           pl.BlockSpec((W_GROUP, ROW_TILE), lambda i: (0, 0))]
        + [full(p) for p in params],
        out_specs=[pl.BlockSpec((ROW_TILE, d), row)] * 2,
        compiler_params=_cparams(("arbitrary",)),
        name="out_projection",
    )(x, mod, *acts, y5, att_lat, att_ctx, *params)


def _ffn_kernel(x_ref, mod_ref, h_ref, hp_ref, hn_ref, wup_ref, cw_ref, wdn_ref, fg_ref, o_ref, acc_ref,
                *, bounds, final_norm):
    tm = h_ref.shape[0]
    first, last = _segment_masks(pl.program_id(0) * tm, tm, bounds)
    h = h_ref[...]
    h_ext = jnp.concatenate([hp_ref[...], h, hn_ref[...]], axis=0)
    n_ext = tm + 2 * HALO
    n_chunks = D_FF // FF_CHUNK

    def up(j):
        c0 = j * FF_CHUNK
        return (_dot(h, wup_ref[:, c0:c0 + FF_CHUNK]),
                _dot(h_ext, wup_ref[:, D_FF + c0:D_FF + c0 + FF_CHUNK]))

    def down(j, u, g_ext):
        c0 = j * FF_CHUNK
        gp = jnp.where(first, 0.0, pltpu.roll(g_ext, 1, 0)[HALO:HALO + tm])
        gn = jnp.where(last, 0.0, pltpu.roll(g_ext, n_ext - 1, 0)[HALO:HALO + tm])
        gc = g_ext[HALO:HALO + tm]
        cw = cw_ref[:, c0:c0 + FF_CHUNK]
        conv = cw[0:1, :] * gp + cw[1:2, :] * gc + cw[2:3, :] * gn
        act = (_silu(conv) * u).astype(BF16)
        return _dot(act, wdn_ref[c0:c0 + FF_CHUNK, :])

    nxt = up(0)
    for j in range(n_chunks):
        cur = nxt
        if j + 1 < n_chunks:
            nxt = up(j + 1)
        y = down(j, *cur)
        if j == 0:
            acc_ref[...] = y
        else:
            acc_ref[...] += y
    y = x_ref[...] + mod_ref[5:6, :] * acc_ref[...]
    o_ref[...] = _rms(y, fg_ref[...]) if final_norm else y


def _conv_ffn(x, mod, seg_of_tile, h2, w_up, conv_w, w_down, bounds, n_out_rows, final_gain=None):
    r, d = x.shape
    row = lambda i: (i, 0)
    prev, nxt = _halo_specs(d, r)
    full = lambda a: pl.BlockSpec(a.shape, lambda i: (0, 0))
    gain = jnp.ones((1, d), F32) if final_gain is None else final_gain
    return pl.pallas_call(
        functools.partial(_ffn_kernel, bounds=bounds, final_norm=final_gain is not None),
        out_shape=jax.ShapeDtypeStruct((n_out_rows, d), F32),
        grid=(n_out_rows // ROW_TILE,),
        in_specs=[pl.BlockSpec((ROW_TILE, d), row),
                  pl.BlockSpec((None, 8, d), lambda i: (seg_of_tile(i), 0, 0)),
                  pl.BlockSpec((ROW_TILE, d), row), prev, nxt,
                  full(w_up), full(conv_w), full(w_down), full(gain)],
        out_specs=pl.BlockSpec((ROW_TILE, d), row),
        scratch_shapes=[pltpu.VMEM((ROW_TILE, d), F32)],
        compiler_params=_cparams(("arbitrary",)),
        name="conv_ffn",
    )(x, mod, h2, h2, h2, w_up, conv_w, w_down, gain)


def _rope_tables(t_lat, n_ctx_rows):
    rows = t_lat // GRID_W
    row = jnp.broadcast_to(jnp.arange(rows)[:, None], (rows, GRID_W)).reshape(-1).astype(F32)
    col = jnp.broadcast_to(jnp.arange(GRID_W)[None, :], (rows, GRID_W)).reshape(-1).astype(F32)
    n_freq = MLA_ROPE // 4
    inv = ROPE_BASE ** (-jnp.arange(n_freq, dtype=F32) / n_freq)
    ang = jnp.concatenate([row[:, None] * inv, col[:, None] * inv], axis=-1)
    half = MLA_ROPE // 2
    pad = LANES - MLA_NOPE - MLA_ROPE

    def table(t, lead):
        return jnp.concatenate([jnp.full((t_lat, MLA_NOPE), lead, F32), t, t, jnp.zeros((t_lat, pad), F32)], axis=-1)

    cos = table(jnp.cos(ang), 1.0)
    sin = table(jnp.sin(ang), 0.0)
    lane = np.arange(LANES)
    cos_ctx = jnp.broadcast_to(jnp.asarray((lane < MLA_NOPE + 2 * half).astype(np.float32)), (n_ctx_rows, LANES))
    sin_ctx = jnp.zeros((n_ctx_rows, LANES), F32)
    return cos, sin, cos_ctx, sin_ctx


def _mla_weights(w_uq, w_ukv):
    half = MLA_ROPE // 2
    qd = MLA_NOPE + MLA_ROPE
    kd = MLA_NOPE + MLA_V
    hw = MLA_HEADS * LANES
    qa_idx = np.zeros((hw,), np.int32); qa_s = np.zeros((hw,), np.float32)
    qb_idx = np.zeros((hw,), np.int32); qb_s = np.zeros((hw,), np.float32)
    ka_idx = np.zeros((hw,), np.int32); ka_s = np.zeros((hw,), np.float32)
    v_idx = np.zeros((MLA_HEADS * MLA_VROWS,), np.int32); v_s = np.zeros((MLA_HEADS * MLA_VROWS,), np.float32)
    for h in range(MLA_HEADS):
        for i in range(qd):
            qa_idx[h * LANES + i] = h * qd + i
            qa_s[h * LANES + i] = 1.0
        for i in range(half):
            qb_idx[h * LANES + MLA_NOPE + i] = h * qd + MLA_NOPE + half + i
            qb_s[h * LANES + MLA_NOPE + i] = -1.0
            qb_idx[h * LANES + MLA_NOPE + half + i] = h * qd + MLA_NOPE + i
            qb_s[h * LANES + MLA_NOPE + half + i] = 1.0
        for i in range(MLA_NOPE):
            ka_idx[h * LANES + i] = h * kd + i
            ka_s[h * LANES + i] = 1.0
        for i in range(MLA_V):
            v_idx[h * MLA_VROWS + i] = h * kd + MLA_NOPE + i
            v_s[h * MLA_VROWS + i] = 1.0
    pick = lambda w, idx, s: (w[..., idx] * s).astype(BF16)
    return (pick(w_uq, qa_idx, qa_s), pick(w_uq, qb_idx, qb_s), pick(w_ukv, ka_idx, ka_s),
            pick(w_ukv, v_idx, v_s).T)


def _lane_rows(vals, offset, n_rows=8):
    k, n = vals.shape
    out = jnp.zeros((n_rows, LANES), F32)
    return out.at[:k, offset:offset + n].set(vals)


def kernel(x, c, ctx, c_ctx, w_mod, b_mod, norm1, norm2, w_in, ml_gate_bias, ml_norm, mla_q_norm, mla_kv_norm, mla_w_uq, mla_w_ukv, ssd_conv_w, ssd_conv_b, ssd_a_log, ssd_dt_bias, ssd_d, ssd_norm, s5_a_re, s5_a_im, s5_log_dt, s5_b_re, s5_b_im, s5_c_re, s5_c_im, s5_d, s5_w_glu, w_out, ffn_w_up, ffn_conv_w, ffn_w_down, final_norm):
    n_batch, t_lat, d = x.shape
    t_ctx = ctx.shape[1]
    depth = w_mod.shape[0]
    assert d == D_MODEL and t_ctx == SCAN_CHUNK and t_lat % ROW_TILE == 0
    assert (n_batch * t_ctx) % ROW_TILE == 0 and t_lat % ATTN_TK == 0 and t_lat % GRID_W == 0
    n_lat = n_batch * t_lat
    n_rows = n_lat + n_batch * t_ctx
    n_scan = t_lat // SCAN_CHUNK
    bounds = tuple(b * t_lat for b in range(n_batch)) + tuple(n_lat + b * t_ctx for b in range(n_batch + 1))
    seg_of_tile = lambda i: jnp.minimum(i * ROW_TILE // t_lat, n_batch)

    c8 = jnp.zeros((8, d), F32).at[:n_batch].set(c).at[n_batch].set(c_ctx)
    mod_all = _modulation(c8, w_mod, b_mod)
    mod_all = mod_all.reshape(depth, 8, 6, d)[:, :n_batch + 1]
    mod_all = jnp.pad(mod_all, ((0, 0), (0, 0), (0, 2), (0, 0)))

    w_in_tok, w_in_chan = _inproj_weights(w_in)
    assert w_in_tok.shape[-1] == _IN_TOTAL and w_in_chan.shape[-2] == _IN_T_TOTAL
    gate_row = jax.vmap(lambda v: _lane_rows(v.reshape(1, -1), MISC_GATE, 1))(ml_gate_bias)
    gate_col = jnp.swapaxes(gate_row, 1, 2)
    ssd_a = -jnp.exp(ssd_a_log)
    ssd_row = jax.vmap(lambda bvec, avec: _lane_rows(jnp.stack([bvec.reshape(-1), avec.reshape(-1)]), MISC_DT))(
        ssd_dt_bias, ssd_a)
    ssd_col = jnp.swapaxes(ssd_row, 1, 2)
    ssd_cw = jnp.pad(ssd_conv_w, ((0, 0), (0, 8 - ssd_conv_w.shape[1]), (0, 0)))
    mix_cols = jnp.stack([ml_norm, jnp.repeat(ssd_d, SSD_HEADDIM, axis=-1), ssd_norm, s5_d], axis=-1)
    mix_cols = jnp.pad(mix_cols, ((0, 0), (0, 0), (0, 4)))
    ffn_cw = jnp.pad(ffn_conv_w, ((0, 0), (0, 8 - ffn_conv_w.shape[1]), (0, 0)))
    cos_l, sin_l, cos_c, sin_c = _rope_tables(t_lat, n_batch * t_ctx)
    cos = jnp.concatenate([cos_l] * n_batch + [cos_c], axis=0)
    sin = jnp.concatenate([sin_l] * n_batch + [sin_c], axis=0)
    toep, wz, wy, s5_decay = _s5_tables(s5_a_re, s5_a_im, s5_log_dt, s5_b_re, s5_b_im, s5_c_re, s5_c_im)
    s5_decay = s5_decay.reshape(depth, S5_NGROUPS, 8, LANES)
    s5_ca = s5_decay[:, :, 0:2].reshape(depth, 2 * S5_NGROUPS, LANES)
    s5_cb = s5_decay[:, :, 2:4].reshape(depth, 2 * S5_NGROUPS, LANES)
    n_s5_ctx = t_ctx // S5_CHUNK
    n_s5_lat = t_lat // S5_CHUNK

    xf = jnp.concatenate([x.reshape(n_lat, d), ctx.reshape(n_batch * t_ctx, d)], axis=0)
    for l in range(depth):
        mod = mod_all[l]
        mla = (cos, sin, mla_q_norm[l][None], mla_kv_norm[l][None]) + _mla_weights(mla_w_uq[l], mla_w_ukv[l])
        q, k, misc, xbc, v1t, ogt, zt, ut, qa, ka, vta = _in_projection(
            xf, mod, norm1[l][None], w_in_tok, w_in_chan, l, seg_of_tile, mla)
        hf, hb = _mlstm(q, k, v1t, misc, gate_col[l], n_batch, n_scan)
        att_lat, att_ctx = _attention(qa, ka, vta, n_batch, t_lat, t_ctx)
        act, xst = _ssd_prep(xbc, ssd_cw[l], ssd_conv_b[l][None], bounds)
        yf, yb = _ssd_scan(act, xst, misc, ssd_col[l], n_batch, n_scan)
        y5 = _s5_mix(ut, toep, wz, wy, s5_ca[l], s5_cb[l], l, n_batch, n_s5_lat, n_s5_ctx)
        acts = (hf, hb, ogt, yf, yb, xst, zt, ut)
        params = (mix_cols[l], s5_w_glu[l].T.astype(BF16), w_out[l].astype(BF16), norm2[l][None])
        xf, h2 = _out_projection(xf, mod, seg_of_tile, acts, y5, att_lat, att_ctx, params)
        last = l == depth - 1
        xf = _conv_ffn(xf, mod, seg_of_tile, h2, ffn_w_up[l].astype(BF16), ffn_cw[l], ffn_w_down[l].astype(BF16),
                       bounds, n_lat if last else n_rows, final_norm[None] if last else None)
    return xf.reshape(n_batch, t_lat, d)
```

```python
import functools
import math

import numpy as np
import jax
import jax.numpy as jnp
from jax import lax
from jax.experimental import pallas as pl
from jax.experimental.pallas import tpu as pltpu

F32 = jnp.float32
BF16 = jnp.bfloat16
HI = lax.Precision.HIGHEST

D_MODEL = 1024
W_GROUP = 256
EPS = 1e-6
GRID_W = 64
ROPE_BASE = 10000.0

ML_HEADS = 4
ML_DH = 64

MLA_HEADS = 4
MLA_NOPE = 64
MLA_ROPE = 32
MLA_V = 64
MLA_Q_LORA = 256
MLA_KV_LORA = 128
MLA_VROWS = 80

SSD_HEADS = 4
SSD_HEADDIM = 64
SSD_GROUPS = 2
SSD_STATE = 128

S5_GROUP = 16
S5_NGROUPS = 16
S5_STATE = 64
S5_CHUNK = 64
S5_SW = 4 * S5_STATE

D_FF = 2816
FF_CHUNK = 256

ROW_TILE = 512
SCAN_CHUNK = 256
ATTN_TQ = 512
ATTN_TK = 512
HALO = 16
LANES = 128
VMEM_LIMIT = 56 * 1024 * 1024

MISC_GATE = 0
MISC_DT = 16
MISC_KR = 24


def _cparams(sem):
    return pltpu.CompilerParams(dimension_semantics=sem, vmem_limit_bytes=VMEM_LIMIT)


def _dot(a, b):
    return jnp.dot(a, b, preferred_element_type=F32)


def _dot_nt(a, b):
    return lax.dot_general(a, b, (((1,), (1,)), ((), ())), preferred_element_type=F32)


def _dot_tn(a, b):
    return lax.dot_general(a, b, (((0,), (0,)), ((), ())), preferred_element_type=F32)


def _dot_hi(a, b):
    return jnp.dot(a, b, preferred_element_type=F32, precision=HI)


def _sigmoid(x):
    return 1.0 / (1.0 + jnp.exp(-x))


def _silu(x):
    return x * _sigmoid(x)


def _log_sigmoid(x):
    return jnp.minimum(x, 0.0) - jnp.log(1.0 + jnp.exp(-jnp.abs(x)))


def _softplus(x):
    return jnp.maximum(x, 0.0) + jnp.log(1.0 + jnp.exp(-jnp.abs(x)))


def _gelu_tanh(x):
    return 0.5 * x * (1.0 + jnp.tanh(math.sqrt(2.0 / math.pi) * (x + 0.044715 * x * x * x)))


def _rms(x, g):
    return x * lax.rsqrt(jnp.mean(x * x, axis=-1, keepdims=True) + EPS) * g


def _segment_masks(row0, n_rows, bounds):
    r = row0 + lax.broadcasted_iota(jnp.int32, (n_rows, 1), 0)
    first = r == bounds[0]
    last = r == bounds[1] - 1
    for s in bounds[1:-1]:
        first = first | (r == s)
    for e in bounds[2:]:
        last = last | (r == e - 1)
    return first, last


def _mod_kernel(c_ref, w_ref, b_ref, o_ref):
    c = c_ref[...]
    s = _silu(c).astype(BF16)
    o_ref[...] = _dot(s, w_ref[...].astype(BF16)) + b_ref[...]


def _modulation(c8, w_mod, b_mod):
    depth, d, d6 = w_mod.shape
    return pl.pallas_call(
        _mod_kernel,
        out_shape=jax.ShapeDtypeStruct((depth, 8, d6), F32),
        grid=(depth, d6 // d),
        in_specs=[
            pl.BlockSpec((8, d), lambda l, j: (0, 0)),
            pl.BlockSpec((None, d, d), lambda l, j: (l, 0, j)),
            pl.BlockSpec((None, 1, d), lambda l, j: (l, 0, j)),
        ],
        out_specs=pl.BlockSpec((None, 8, d), lambda l, j: (l, 0, j)),
        compiler_params=_cparams(("arbitrary", "arbitrary")),
        name="modulation",
    )(c8, w_mod, b_mod.reshape(depth, 1, d6))


_IN_SPLITS = (("q", 256, BF16), ("k", 256, BF16), ("misc", 128, F32), ("mlac", 384, F32), ("xbc", 768, F32))
_IN_TOTAL = sum(s[1] for s in _IN_SPLITS)
_IN_T_SPLITS = (("v1t", ML_HEADS * LANES, BF16), ("ogt", 256, F32), ("zt", 256, F32), ("ut", 256, F32))
_IN_T_TOTAL = sum(s[1] for s in _IN_T_SPLITS)


def _inproj_weights(w_in):
    ml, mla, ssd, s5 = 0, 1040, 1456, 2488
    cols = lambda a, b: w_in[..., a:b]
    zeros = lambda n: jnp.zeros(w_in.shape[:-1] + (n,), w_in.dtype)
    tok = [cols(ml, ml + 256), cols(ml + 256, ml + 512) * ML_DH ** -0.5]
    tok += [cols(ml + 1024, ml + 1040), cols(ssd + 1024, ssd + 1032),
            cols(mla + 384, mla + 416), zeros(LANES - 56)]
    tok += [cols(mla, mla + 384), cols(ssd + 256, ssd + 1024)]
    chan = []
    for h in range(ML_HEADS):
        chan += [cols(ml + 512 + h * 64, ml + 512 + (h + 1) * 64), zeros(64)]
    chan += [cols(ml + 768, ml + 1024), cols(ssd, ssd + 256), cols(s5, s5 + 256)]
    w_tok = jnp.concatenate(tok, axis=-1).astype(BF16)
    w_chan = jnp.swapaxes(jnp.concatenate(chan, axis=-1), -1, -2).astype(BF16)
    return w_tok, w_chan


def _inproj_kernel(x_ref, mod_ref, g_ref, w_ref, wt_ref, cos_ref, sin_ref, qn_ref, kvn_ref,
                   wqa_ref, wqb_ref, wka_ref, wvt_ref,
                   q_ref, k_ref, misc_ref, xbc_ref, v1t_ref, ogt_ref, zt_ref, ut_ref, qa_ref, ka_ref, vta_ref):
    x = x_ref[...]
    h = _rms(x, g_ref[...]) * (1.0 + mod_ref[1:2, :]) + mod_ref[0:1, :]
    hb = h.astype(BF16)
    off = 0
    tok = {}
    for (name, width, _), o_ref in zip(_IN_SPLITS, (q_ref, k_ref, misc_ref, None, xbc_ref)):
        tok[name] = _dot(hb, w_ref[:, off:off + width])
        if o_ref is not None:
            o_ref[...] = tok[name].astype(o_ref.dtype)
        off += width
    _mla_project(tok["mlac"], tok["misc"], cos_ref[...], sin_ref[...], qn_ref[...], kvn_ref[...],
                 wqa_ref, wqb_ref, wka_ref, wvt_ref, qa_ref, ka_ref, vta_ref)
    off = 0
    for (name, width, _), o_ref in zip(_IN_T_SPLITS, (v1t_ref, ogt_ref, zt_ref, ut_ref)):
        y = _dot_nt(wt_ref[off:off + width, :], hb)
        if name == "v1t":
            row = lax.broadcasted_iota(jnp.int32, (width, 1), 0)
            y = y + jnp.where((row & (LANES - 1)) == ML_DH, 1.0, 0.0)
        o_ref[...] = y.astype(o_ref.dtype)
        off += width


def _in_projection(x, mod, g, w, wt, layer, seg_of_tile, mla):
    r, d = x.shape
    row = lambda i: (i, 0)
    col = lambda i: (0, i)
    full = lambda a: pl.BlockSpec(a.shape, lambda i: (0, 0))
    tok_out = [s for s in _IN_SPLITS if s[0] != "mlac"]
    hw = MLA_HEADS * LANES
    vr = MLA_HEADS * MLA_VROWS
    cos, sin = mla[0], mla[1]
    return pl.pallas_call(
        _inproj_kernel,
        out_shape=[jax.ShapeDtypeStruct((r, width), dt) for _, width, dt in tok_out]
        + [jax.ShapeDtypeStruct((width, r), dt) for _, width, dt in _IN_T_SPLITS]
        + [jax.ShapeDtypeStruct((r, hw), BF16)] * 2 + [jax.ShapeDtypeStruct((vr, r), BF16)],
        grid=(r // ROW_TILE,),
        in_specs=[
            pl.BlockSpec((ROW_TILE, d), row),
            pl.BlockSpec((None, 8, d), lambda i: (seg_of_tile(i), 0, 0)),
            pl.BlockSpec((1, d), lambda i: (0, 0)),
            pl.BlockSpec((None, d, _IN_TOTAL), lambda i: (layer, 0, 0)),
            pl.BlockSpec((None, _IN_T_TOTAL, d), lambda i: (layer, 0, 0)),
            pl.BlockSpec((ROW_TILE, LANES), row), pl.BlockSpec((ROW_TILE, LANES), row),
        ] + [full(a) for a in mla[2:]],
        out_specs=[pl.BlockSpec((ROW_TILE, width), row) for _, width, _ in tok_out]
        + [pl.BlockSpec((width, ROW_TILE), col) for _, width, _ in _IN_T_SPLITS]
        + [pl.BlockSpec((ROW_TILE, hw), row)] * 2 + [pl.BlockSpec((vr, ROW_TILE), col)],
        compiler_params=_cparams(("arbitrary",)),
        name="in_projection",
    )(x, mod, g, w, wt, cos, sin, *mla[2:])


def _tri_masks(n):
    row = lax.broadcasted_iota(jnp.int32, (n, n), 0)
    col = lax.broadcasted_iota(jnp.int32, (n, n), 1)
    return col <= row, col >= row


def _col_forms(rows16, n):
    return jnp.concatenate([rows16, jnp.zeros((LANES - rows16.shape[0], n), F32)], axis=0).T


def _mlstm_pass1(qf_ref, kf_ref, vf_ref, mf_ref, qb_ref, kb_ref, vb_ref, mb_ref,
                 gbc_ref, hf_ref, hb_ref, st_ref, m_ref):
    n = qf_ref.shape[0]

    @pl.when(pl.program_id(1) == 0)
    def _():
        st_ref[...] = jnp.zeros_like(st_ref)
        m_ref[...] = jnp.zeros_like(m_ref)

    tril, triu = _tri_masks(n)
    dirs = ((qf_ref, kf_ref, vf_ref, mf_ref, hf_ref, triu, n - 1),
            (qb_ref, kb_ref, vb_ref, mb_ref, hb_ref, tril, 0))
    pending = []
    for d, (q_ref, k_ref, vt_ref, misc_ref, out_ref, mask, last) in enumerate(dirs):
        gt = misc_ref[...].T + gbc_ref[...]
        brow = _dot_hi(_log_sigmoid(gt[0:16, :]), mask.astype(F32))
        ccol = _col_forms(gt[0:16, :] - pltpu.roll(brow, 16 - ML_HEADS, 0), n)
        for h in range(ML_HEADS):
            hd = d * ML_HEADS + h
            i_idx = MISC_GATE + 8 * d + h
            f_idx = i_idx + ML_HEADS
            ig_row = gt[i_idx:i_idx + 1, :]
            b_row = brow[f_idx:f_idx + 1, :]
            c_col = ccol[:, i_idx:i_idx + 1]
            b_last = b_row[:, last:last + 1]
            m_prev = m_ref[hd:hd + 1, 0:1]
            q = q_ref[:, h * ML_DH:(h + 1) * ML_DH]
            k = k_ref[:, h * ML_DH:(h + 1) * ML_DH]
            v1t = vt_ref[h * LANES:(h + 1) * LANES, :]
            state = st_ref[hd]

            qk = _dot_nt(k, q)
            carried = _dot_nt(state.astype(BF16), q)
            w_log = b_last - b_row + ig_row
            m_new = jnp.maximum(b_last + m_prev, jnp.max(w_log, axis=1, keepdims=True))
            decay = jnp.exp(b_last + m_prev - m_new)
            vw = (v1t.astype(F32) * jnp.exp(w_log - m_new)).astype(BF16)
            st_ref[hd] = decay * state + _dot(vw, k)
            m_ref[hd:hd + 1, :] = jnp.broadcast_to(m_new, (1, LANES))
            pending.append((out_ref, h, mask, b_row, c_col, m_prev, qk, carried, v1t))
    return pending


def _mlstm_pass2(pending):
    for out_ref, h, mask, b_row, c_col, m_prev, qk, carried, v1t in pending:
        dmat = jnp.where(mask, b_row + c_col, -jnp.inf)
        inter = b_row + m_prev
        m_t = jnp.maximum(inter, jnp.max(dmat, axis=0, keepdims=True))
        s = qk * jnp.exp(dmat - m_t)
        tot = _dot(v1t, s.astype(BF16)) + jnp.exp(inter - m_t) * carried
        den = tot[ML_DH:ML_DH + 1, :]
        out_ref[h * ML_DH:(h + 1) * ML_DH, :] = tot[0:ML_DH, :] / jnp.maximum(jnp.abs(den), jnp.exp(-m_t))


def _scan_index_maps(n_batch, n_chunks):
    ctx0 = n_batch * n_chunks
    fwd = lambda b, s: (jnp.where(s == 0, ctx0 + b, b * n_chunks + s - 1), 0)
    bwd = lambda b, s: (jnp.where(s == 0, ctx0 + b, b * n_chunks + n_chunks - s), 0)
    return fwd, bwd


def _mlstm_plan(q, k, v1t, misc, gb_col, n_batch, n_chunks):
    n = SCAN_CHUNK
    fwd, bwd = _scan_index_maps(n_batch, n_chunks)
    const = lambda b, s: (0, 0)
    specs = []
    for imap in (fwd, bwd):
        cols = lambda b, s, imap=imap: imap(b, s)[::-1]
        specs += [pl.BlockSpec((n, W_GROUP), imap), pl.BlockSpec((n, W_GROUP), imap),
                  pl.BlockSpec((ML_HEADS * LANES, n), cols), pl.BlockSpec((n, LANES), imap)]
    specs += [pl.BlockSpec((LANES, 1), const)]
    scratch = [pltpu.VMEM((2 * ML_HEADS, LANES, ML_DH), F32), pltpu.VMEM((2 * ML_HEADS, LANES), F32)]
    return (q, k, v1t, misc, q, k, v1t, misc, gb_col), specs, scratch


def _ssd_prep_kernel(x_ref, xp_ref, xn_ref, w_ref, b_ref, act_ref, xs_ref, *, bounds):
    tm = x_ref.shape[0]
    first, last = _segment_masks(pl.program_id(0) * tm, tm, bounds)
    x = x_ref[...]
    rid = lax.broadcasted_iota(jnp.int32, (tm, 1), 0)
    xp = jnp.where(rid == 0, xp_ref[HALO - 1:HALO, :], pltpu.roll(x, 1, 0))
    xn = jnp.where(rid == tm - 1, xn_ref[0:1, :], pltpu.roll(x, tm - 1, 0))
    xp = jnp.where(first, 0.0, xp)
    xn = jnp.where(last, 0.0, xn)
    y = _silu(w_ref[0:1, :] * xp + w_ref[1:2, :] * x + w_ref[2:3, :] * xn + b_ref[...])
    act_ref[...] = y.astype(act_ref.dtype)
    xs_ref[...] = y[:, 0:W_GROUP].T


def _halo_specs(width, n_rows):
    per = ROW_TILE // HALO
    n_blocks = n_rows // HALO
    prev = pl.BlockSpec((HALO, width), lambda i: (jnp.maximum(i * per - 1, 0), 0))
    nxt = pl.BlockSpec((HALO, width), lambda i: (jnp.minimum((i + 1) * per, n_blocks - 1), 0))
    return prev, nxt


def _ssd_prep(xbc, conv_w, conv_b, bounds):
    r, width = xbc.shape
    row = lambda i: (i, 0)
    prev, nxt = _halo_specs(width, r)
    return pl.pallas_call(
        functools.partial(_ssd_prep_kernel, bounds=bounds),
        out_shape=[jax.ShapeDtypeStruct((r, width), BF16), jax.ShapeDtypeStruct((W_GROUP, r), F32)],
        grid=(r // ROW_TILE,),
        in_specs=[pl.BlockSpec((ROW_TILE, width), row), prev, nxt,
                  pl.BlockSpec((8, width), lambda i: (0, 0)),
                  pl.BlockSpec((1, width), lambda i: (0, 0))],
        out_specs=[pl.BlockSpec((ROW_TILE, width), row), pl.BlockSpec((W_GROUP, ROW_TILE), lambda i: (0, i))],
        compiler_params=_cparams(("arbitrary",)),
        name="ssd_conv",
    )(xbc, xbc, xbc, conv_w, conv_b)


def _ssd_pass1(af_ref, xf_ref, mf_ref, ab_ref, xb_ref, mb_ref, pc_ref, yf_ref, yb_ref, st_ref):
    n = af_ref.shape[0]

    @pl.when(pl.program_id(1) == 0)
    def _():
        st_ref[...] = jnp.zeros_like(st_ref)

    tril, triu = _tri_masks(n)
    dirs = ((af_ref, xf_ref, mf_ref, yf_ref, triu, n - 1),
            (ab_ref, xb_ref, mb_ref, yb_ref, tril, 0))
    pending = []
    for d, (act_ref, xt_ref, misc_ref, out_ref, mask, last) in enumerate(dirs):
        dt_r = _softplus(misc_ref[...].T + pc_ref[:, 0:1])
        cs_r = _dot_hi((dt_r * pc_ref[:, 1:2])[0:32, :], mask.astype(F32))
        ncs_c = _col_forms(-cs_r, n)
        for g in range(SSD_GROUPS):
            bm = act_ref[:, W_GROUP + g * SSD_STATE:W_GROUP + (g + 1) * SSD_STATE]
            cm = act_ref[:, W_GROUP + SSD_GROUPS * SSD_STATE + g * SSD_STATE:
                         W_GROUP + SSD_GROUPS * SSD_STATE + (g + 1) * SSD_STATE]
            gmat = _dot_nt(bm, cm)
            for hh in range(SSD_HEADS // SSD_GROUPS):
                h = g * (SSD_HEADS // SSD_GROUPS) + hh
                hd = d * SSD_HEADS + h
                idx = MISC_DT + d * SSD_HEADS + h
                cs_row = cs_r[idx:idx + 1, :]
                ncs_col = ncs_c[:, idx:idx + 1]
                cs_last = cs_row[:, last:last + 1]
                dt_row = dt_r[idx:idx + 1, :]
                xt = xt_ref[h * SSD_HEADDIM:(h + 1) * SSD_HEADDIM, :]
                state = st_ref[hd]
                carried = _dot_nt(state.astype(BF16), cm)
                xw = (xt * (jnp.exp(cs_last - cs_row) * dt_row)).astype(BF16)
                st_ref[hd] = jnp.exp(cs_last) * state + _dot(xw, bm)
                pending.append((out_ref, h, mask, cs_row, ncs_col, gmat, carried, (xt * dt_row).astype(BF16)))
    return pending


def _ssd_pass2(pending):
    for out_ref, h, mask, cs_row, ncs_col, gmat, carried, xdt in pending:
        decay = jnp.exp(jnp.where(mask, cs_row + ncs_col, -jnp.inf))
        y = _dot(xdt, (gmat * decay).astype(BF16)) + jnp.exp(cs_row) * carried
        out_ref[h * SSD_HEADDIM:(h + 1) * SSD_HEADDIM, :] = y


def _ssd_plan(act, xst, misc, p_col, n_batch, n_chunks):
    width = act.shape[1]
    n = SCAN_CHUNK
    fwd, bwd = _scan_index_maps(n_batch, n_chunks)
    const = lambda b, s: (0, 0)
    fwd_c = lambda b, s: fwd(b, s)[::-1]
    bwd_c = lambda b, s: bwd(b, s)[::-1]
    specs = []
    for imap, cmap in ((fwd, fwd_c), (bwd, bwd_c)):
        specs += [pl.BlockSpec((n, width), imap), pl.BlockSpec((W_GROUP, n), cmap), pl.BlockSpec((n, LANES), imap)]
    specs += [pl.BlockSpec((LANES, 8), const)]
    scratch = [pltpu.VMEM((2 * SSD_HEADS, SSD_HEADDIM, SSD_STATE), F32)]
    return (act, xst, misc, act, xst, misc, p_col), specs, scratch


def _scan_pair_kernel(*refs, n_ml_in, n_ss_in):
    ml_in, ss_in = refs[:n_ml_in], refs[n_ml_in:n_ml_in + n_ss_in]
    outs = refs[n_ml_in + n_ss_in:n_ml_in + n_ss_in + 4]
    scr = refs[n_ml_in + n_ss_in + 4:]
    ml_pending = _mlstm_pass1(*ml_in, outs[0], outs[1], scr[0], scr[1])
    ss_pending = _ssd_pass1(*ss_in, outs[2], outs[3], scr[2])
    _mlstm_pass2(ml_pending)
    _ssd_pass2(ss_pending)


def _scans(ml_plan, ss_plan, n_rows, n_batch, n_chunks):
    n = SCAN_CHUNK
    fwd, bwd = _scan_index_maps(n_batch, n_chunks)
    fwd_c = lambda b, s: fwd(b, s)[::-1]
    bwd_c = lambda b, s: bwd(b, s)[::-1]
    (ml_ops, ml_specs, ml_scr), (ss_ops, ss_specs, ss_scr) = ml_plan, ss_plan
    return pl.pallas_call(
        functools.partial(_scan_pair_kernel, n_ml_in=len(ml_ops), n_ss_in=len(ss_ops)),
        out_shape=[jax.ShapeDtypeStruct((W_GROUP, n_rows), F32)] * 4,
        grid=(n_batch, n_chunks + 1),
        in_specs=ml_specs + ss_specs,
        out_specs=[pl.BlockSpec((W_GROUP, n), fwd_c), pl.BlockSpec((W_GROUP, n), bwd_c)] * 2,
        scratch_shapes=ml_scr + ss_scr,
        compiler_params=_cparams(("arbitrary", "arbitrary")),
        name="mlstm_ssd_scan",
    )(*ml_ops, *ss_ops)


def _mla_project(c, misc, cos, sin, qn, kvn, wqa_ref, wqb_ref, wka_ref, wvt_ref, q_ref, k_ref, vt_ref):
    scale = (MLA_NOPE + MLA_ROPE) ** -0.5 * math.log2(math.e)
    half = MLA_ROPE // 2
    cq = _rms(c[:, 0:MLA_Q_LORA], qn).astype(BF16)
    ckv = _rms(c[:, MLA_Q_LORA:MLA_Q_LORA + MLA_KV_LORA], kvn).astype(BF16)
    qa = _dot(cq, wqa_ref[...])
    qb = _dot(cq, wqb_ref[...])
    ka = _dot(ckv, wka_ref[...])
    lane = lax.broadcasted_iota(jnp.int32, (1, LANES), 1)
    in_rope = (lane >= MLA_NOPE) & (lane < MLA_NOPE + MLA_ROPE)
    kr_a = pltpu.roll(misc, MLA_NOPE - MISC_KR, 1)
    kr_b = jnp.where(lane < MLA_NOPE + half,
                     -pltpu.roll(misc, MLA_NOPE - MISC_KR - half, 1),
                     pltpu.roll(misc, MLA_NOPE - MISC_KR + half, 1))
    kr = jnp.where(in_rope, kr_a * cos + kr_b * sin, 0.0)
    for h in range(MLA_HEADS):
        sl = slice(h * LANES, (h + 1) * LANES)
        q_ref[:, sl] = ((qa[:, sl] * cos + qb[:, sl] * sin) * scale).astype(BF16)
        k_ref[:, sl] = (ka[:, sl] + kr).astype(BF16)
    vt = _dot_nt(wvt_ref[...], ckv)
    row = lax.broadcasted_iota(jnp.int32, (vt.shape[0], 1), 0)
    ones_row = row == MLA_V
    for h in range(1, MLA_HEADS):
        ones_row = ones_row | (row == h * MLA_VROWS + MLA_V)
    vt_ref[...] = (vt + jnp.where(ones_row, 1.0, 0.0)).astype(BF16)


_HEAD_LANES = tuple(slice(h * LANES, (h + 1) * LANES) for h in range(MLA_HEADS))
_HEAD_VROWS = tuple(slice(h * MLA_VROWS, (h + 1) * MLA_VROWS) for h in range(MLA_HEADS))


def _attn_step(q_ref, m_ref, acc_ref, scores=None, update=None):
    new = []
    for h in range(MLA_HEADS):
        if scores is not None:
            k_ref, rows, n_keys, (s_ref, mx_ref) = scores
            st = _dot_nt(k_ref[rows, _HEAD_LANES[h]], q_ref[:, _HEAD_LANES[h]])
            s_ref[h, 0:n_keys, :] = st
            mx_ref[h] = jnp.max(st, axis=0, keepdims=True)
        if update is not None:
            (s_ref, mx_ref), n_keys, vt_ref, cols = update
            m_old = m_ref[h]
            m_new = jnp.maximum(m_old, mx_ref[h])
            p = jnp.exp2(s_ref[h, 0:n_keys, :] - m_new).astype(BF16)
            new.append((m_new, jnp.exp2(m_old - m_new) * acc_ref[h] + _dot(vt_ref[_HEAD_VROWS[h], cols], p)))
    for h, (m_new, acc) in enumerate(new):
        m_ref[h] = m_new
        acc_ref[h] = acc


def _attn_init(m_ref, acc_ref):
    m_ref[...] = jnp.full_like(m_ref, -jnp.inf)
    acc_ref[...] = jnp.zeros_like(acc_ref)


def _attn_finish(o_ref, acc_ref):
    for h in range(MLA_HEADS):
        acc = acc_ref[h]
        o_ref[h * MLA_V:(h + 1) * MLA_V, :] = (acc[0:MLA_V] / acc[MLA_V:MLA_V + 1]).astype(o_ref.dtype)


def _attn_lat_kernel(q_ref, kl_ref, vlt_ref, kc_ref, vct_ref, o_ref, m_ref, acc_ref,
                     sa_s_ref, sa_m_ref, sb_s_ref, sb_m_ref):
    sa_ref = (sa_s_ref, sa_m_ref)
    sb_ref = (sb_s_ref, sb_m_ref)
    tk = ATTN_TK
    n_pairs = kl_ref.shape[0] // (2 * tk)
    n_ctx = kc_ref.shape[0]
    chunk = lambda c: pl.ds(pl.multiple_of(c * tk, tk), tk)
    step = functools.partial(_attn_step, q_ref, m_ref, acc_ref)
    _attn_init(m_ref, acc_ref)
    step(scores=(kl_ref, chunk(0), tk, sa_ref))

    def body(i, carry):
        c = 2 * i
        step(scores=(kl_ref, chunk(c + 1), tk, sb_ref), update=(sa_ref, tk, vlt_ref, chunk(c)))
        step(scores=(kl_ref, chunk(c + 2), tk, sa_ref), update=(sb_ref, tk, vlt_ref, chunk(c + 1)))
        return carry

    lax.fori_loop(0, n_pairs - 1, body, 0)
    c = 2 * (n_pairs - 1)
    step(scores=(kl_ref, chunk(c + 1), tk, sb_ref), update=(sa_ref, tk, vlt_ref, chunk(c)))
    step(scores=(kc_ref, slice(None), n_ctx, sa_ref), update=(sb_ref, tk, vlt_ref, chunk(c + 1)))
    step(update=(sa_ref, n_ctx, vct_ref, slice(None)))
    _attn_finish(o_ref, acc_ref)


def _attn_ctx_kernel(q_ref, kc_ref, vct_ref, o_ref, m_ref, acc_ref, s_ref, mx_ref):
    n_ctx = kc_ref.shape[0]
    _attn_init(m_ref, acc_ref)
    _attn_step(q_ref, m_ref, acc_ref, scores=(kc_ref, slice(None), n_ctx, (s_ref, mx_ref)))
    _attn_step(q_ref, m_ref, acc_ref, update=((s_ref, mx_ref), n_ctx, vct_ref, slice(None)))
    _attn_finish(o_ref, acc_ref)


def _attention(q, k, vt, n_batch, t_lat, t_ctx):
    r, hw = q.shape
    vr = vt.shape[0]
    nq = t_lat // ATTN_TQ
    ctx_blk = n_batch * t_lat // t_ctx
    assert t_lat % (2 * ATTN_TK) == 0 and t_ctx <= ATTN_TK
    scratch = lambda tq: [pltpu.VMEM((MLA_HEADS, 1, tq), F32), pltpu.VMEM((MLA_HEADS, MLA_VROWS, tq), F32)]
    scores = lambda tk, tq: [pltpu.VMEM((MLA_HEADS, tk, tq), F32), pltpu.VMEM((MLA_HEADS, 1, tq), F32)]
    lat = pl.pallas_call(
        _attn_lat_kernel,
        out_shape=jax.ShapeDtypeStruct((W_GROUP, n_batch * t_lat), BF16),
        grid=(n_batch, nq),
        in_specs=[pl.BlockSpec((ATTN_TQ, hw), lambda b, i: (b * nq + i, 0)),
                  pl.BlockSpec((t_lat, hw), lambda b, i: (b, 0)),
                  pl.BlockSpec((vr, t_lat), lambda b, i: (0, b)),
                  pl.BlockSpec((t_ctx, hw), lambda b, i: (ctx_blk + b, 0)),
                  pl.BlockSpec((vr, t_ctx), lambda b, i: (0, ctx_blk + b))],
        out_specs=pl.BlockSpec((W_GROUP, ATTN_TQ), lambda b, i: (0, b * nq + i)),
        scratch_shapes=scratch(ATTN_TQ) + scores(ATTN_TK, ATTN_TQ) * 2,
        compiler_params=_cparams(("arbitrary", "arbitrary")),
        name="mla_attention_latent",
    )(q, k, vt, k, vt)
    ctx = pl.pallas_call(
        _attn_ctx_kernel,
        out_shape=jax.ShapeDtypeStruct((W_GROUP, n_batch * t_ctx), BF16),
        grid=(n_batch,),
        in_specs=[pl.BlockSpec((t_ctx, hw), lambda b: (ctx_blk + b, 0)),
                  pl.BlockSpec((t_ctx, hw), lambda b: (ctx_blk + b, 0)),
                  pl.BlockSpec((vr, t_ctx), lambda b: (0, ctx_blk + b))],
        out_specs=pl.BlockSpec((W_GROUP, t_ctx), lambda b: (0, b)),
        scratch_shapes=scratch(t_ctx) + scores(t_ctx, t_ctx),
        compiler_params=_cparams(("arbitrary",)),
        name="mla_attention_context",
    )(q, k, vt)
    return lat, ctx


def _cis_pow(lr, li, dt, k):
    mag = jnp.exp(lr * dt * k)
    ang = li * dt * k
    return mag * jnp.cos(ang), mag * jnp.sin(ang)


def _s5_tables_kernel(pr_ref, pc_ref, btre_ref, btim_ref, ctre_ref, ctim_ref,
                      toep_ref, wz_ref, wy_ref, decay_ref):
    lc, ng, ns = S5_CHUNK, S5_GROUP, S5_STATE
    lr_r = jnp.minimum(pr_ref[0:1, :], -1e-4)
    li_r = pr_ref[1:2, :]
    dt_r = jnp.exp(pr_ref[2:3, :])
    lr_c = jnp.minimum(pc_ref[:, 0:1], -1e-4)
    li_c = pc_ref[:, 1:2]
    dt_c = jnp.exp(pc_ref[:, 2:3])
    lane4 = lax.broadcasted_iota(jnp.int32, (1, S5_SW), 1)
    odd_lane = (lane4 & ns) != 0
    row4 = lax.broadcasted_iota(jnp.int32, (S5_SW, 1), 0)
    odd_row = (row4 & ns) != 0

    ab_re, ab_im = _cis_pow(lr_r, li_r, dt_r, 1.0)
    den = lr_r * lr_r + li_r * li_r
    f_re = ((ab_re - 1.0) * lr_r + ab_im * li_r) / den
    f_im = (ab_im * lr_r - (ab_re - 1.0) * li_r) / den
    bb_re = f_re * btre_ref[...] - f_im * btim_ref[...]
    bb_im = f_re * btim_ref[...] + f_im * btre_ref[...]

    s_col = lax.broadcasted_iota(jnp.int32, (lc, 1), 0)
    expo = jnp.where(lane4 < 2 * ns, lc - 1 - s_col, s_col).astype(F32)
    pz_re, pz_im = _cis_pow(lr_r, li_r, dt_r, expo)
    xz = jnp.where(odd_lane, bb_im, bb_re)
    yz = jnp.where(odd_lane, bb_re, -bb_im)
    for j in range(ng):
        wz_ref[j] = (pz_re * xz[j:j + 1, :] + pz_im * yz[j:j + 1, :]).astype(BF16)

    al_re, al_im = _cis_pow(lr_r, li_r, dt_r, float(lc))
    al_sw = jnp.where(odd_lane, al_im, -al_im)
    decay_ref[...] = jnp.zeros_like(decay_ref)
    for d in range(2):
        decay_ref[d:d + 1, :] = al_re[:, d * LANES:(d + 1) * LANES]
        decay_ref[2 + d:3 + d, :] = al_sw[:, d * LANES:(d + 1) * LANES]

    t_lane = lax.broadcasted_iota(jnp.int32, (1, LANES), 1) & (lc - 1)
    expo = jnp.where(row4 < 2 * ns, t_lane + 1, lc - t_lane).astype(F32)
    py_re, py_im = _cis_pow(lr_c, li_c, dt_c, expo)
    reps = lc * ng // LANES
    py_re = jnp.tile(py_re, (1, reps))
    py_im = jnp.tile(py_im, (1, reps))
    low_half = lax.broadcasted_iota(jnp.int32, (1, 2 * lc), 1) < lc

    def expand_half(c):
        bc = lambda j: jnp.broadcast_to(c[:, j:j + 1], (c.shape[0], 2 * lc))
        return jnp.concatenate([jnp.where(low_half, bc(2 * p), bc(2 * p + 1)) for p in range(ng // 2)], axis=1)

    def expand_full(c):
        return jnp.concatenate([jnp.broadcast_to(c[:, j:j + 1], (c.shape[0], 2 * lc)) for j in range(ng)], axis=1)

    ce_re = jnp.tile(expand_half(ctre_ref[0:ns, :]), (S5_SW // ns, 1))
    ce_im = jnp.tile(expand_half(ctim_ref[0:ns, :]), (S5_SW // ns, 1))
    c2_re = expand_full(ctre_ref[0:ns, :])
    c2_im = expand_full(ctim_ref[0:ns, :])
    wy_ref[...] = jnp.where(odd_row, -(ce_re * py_im + ce_im * py_re),
                            ce_re * py_re - ce_im * py_im).astype(BF16)

    m_lane = lax.broadcasted_iota(jnp.int32, (1, 2 * lc), 1)
    xmat = None
    for d in range(2):
        rows = slice(d * 2 * ns, d * 2 * ns + ns)
        lag = m_lane - (lc - 1) if d == 0 else (lc - 1) - m_lane
        valid = (lag >= 0) & (m_lane < 2 * lc - 1)
        pm_re, pm_im = _cis_pow(lr_c[rows], li_c[rows], dt_c[rows], jnp.where(valid, lag, 0).astype(F32))
        pm_re = jnp.tile(jnp.where(valid, pm_re, 0.0), (1, ng))
        pm_im = jnp.tile(jnp.where(valid, pm_im, 0.0), (1, ng))
        p_re = c2_re * pm_re - c2_im * pm_im
        p_im = c2_re * pm_im + c2_im * pm_re
        lanes = slice(d * 2 * ns, d * 2 * ns + ns)
        term = _dot_hi(bb_re[:, lanes], p_re) - _dot_hi(bb_im[:, lanes], p_im)
        xmat = term if xmat is None else xmat + term

    for j in range(ng):
        xb = jnp.broadcast_to(xmat[j:j + 1, :], (lc, 2 * lc * ng))
        cols = []
        for p in range(ng // 2):
            even = xb[:, (2 * p) * 2 * lc:(2 * p + 1) * 2 * lc]
            odd = xb[:, (2 * p + 1) * 2 * lc:(2 * p + 2) * 2 * lc]
            cols.append(jnp.where(low_half,
                                  pltpu.roll(even, lc + 1, 1, stride=1, stride_axis=0),
                                  pltpu.roll(odd, 1, 1, stride=1, stride_axis=0)))
        toep_ref[j] = jnp.concatenate(cols, axis=1).astype(BF16)


def _s5_tables(a_re, a_im, log_dt, b_re, b_im, c_re, c_im):
    depth = a_re.shape[0]
    g, n, j, lc = S5_NGROUPS, S5_STATE, S5_GROUP, S5_CHUNK
    n_all = depth * g

    def parts(v):
        v = jnp.transpose(v, (0, 2, 1, 3))
        return jnp.concatenate([v[:, :, 0], v[:, :, 0], v[:, :, 1], v[:, :, 1]], axis=-1).reshape(n_all, S5_SW)

    rows = jnp.stack([parts(a_re), parts(a_im), parts(jnp.broadcast_to(log_dt[..., None], a_re.shape))], axis=1)
    p_row = jnp.pad(rows, ((0, 0), (0, 5), (0, 0)))
    p_col = jnp.swapaxes(p_row, 1, 2)
    bt = lambda b: jnp.tile(jnp.swapaxes(b.reshape(n_all, n, j), 1, 2), (1, 1, 4))
    ct = lambda c: jnp.tile(jnp.swapaxes(c.reshape(n_all, j, n), 1, 2), (1, 4, 1))
    own = lambda i: (i, 0, 0)
    width = lc * j
    return pl.pallas_call(
        _s5_tables_kernel,
        out_shape=[jax.ShapeDtypeStruct((n_all * j, lc, width), BF16),
                   jax.ShapeDtypeStruct((n_all * j, lc, S5_SW), BF16),
                   jax.ShapeDtypeStruct((n_all, S5_SW, width), BF16),
                   jax.ShapeDtypeStruct((n_all, 8, LANES), F32)],
        grid=(n_all,),
        in_specs=[pl.BlockSpec((None, 8, S5_SW), own), pl.BlockSpec((None, S5_SW, 8), own),
                  pl.BlockSpec((None, j, S5_SW), own), pl.BlockSpec((None, j, S5_SW), own),
                  pl.BlockSpec((None, S5_SW, j), own), pl.BlockSpec((None, S5_SW, j), own)],
        out_specs=[pl.BlockSpec((j, lc, width), own), pl.BlockSpec((j, lc, S5_SW), own),
                   pl.BlockSpec((None, S5_SW, width), own), pl.BlockSpec((None, 8, LANES), own)],
        compiler_params=_cparams(("arbitrary",)),
        name="s5_tables",
    )(p_row, p_col, bt(b_re), bt(b_im), ct(c_re), ct(c_im))


def _s5_local_kernel(u_ref, wz_ref, z_ref):
    acc = None
    for j in range(S5_GROUP):
        term = _dot(u_ref[j].astype(BF16), wz_ref[j])
        acc = term if acc is None else acc + term
    z_ref[...] = acc


def _s5_scan_kernel(z_ref, ca_ref, cb_ref, x_ref, xf_ref, xb_ref, *, n_batch, n_lat, n_ctx):
    ca = ca_ref[...]
    cb = cb_ref[...]
    zero = jnp.zeros(ca.shape, F32)
    ctx0 = n_batch * n_lat

    def step(x, z):
        return x * ca + pltpu.roll(x, LANES // 2, 1) * cb + z

    def body(i, carry):
        new = []
        for b in range(n_batch):
            rf = jnp.where(i < n_ctx, ctx0 + b * n_ctx + i, b * n_lat + i - n_ctx)
            rb = jnp.where(i < n_ctx, ctx0 + b * n_ctx + n_ctx - 1 - i, b * n_lat + n_lat - 1 - (i - n_ctx))
            xf, xb = carry[2 * b], carry[2 * b + 1]
            xf_ref[rf] = xf
            xb_ref[rb] = xb
            new += [step(xf, z_ref[rf]), step(xb, z_ref[rb])]
        return tuple(new)

    lax.fori_loop(0, n_ctx + n_lat, body, (zero,) * (2 * n_batch))
    row = lax.broadcasted_iota(jnp.int32, x_ref.shape, 1)
    x_ref[...] = jnp.where((row & 1) == 0, xf_ref[...], xb_ref[...])


def _s5_out_kernel(u_ref, t_ref, x_ref, wy_ref, y_ref):
    acc = _dot(x_ref[...].astype(BF16), wy_ref[...])
    for j in range(S5_GROUP):
        acc = acc + _dot(u_ref[j].astype(BF16), t_ref[j])
    for j in range(S5_GROUP):
        y_ref[j] = acc[:, j * S5_CHUNK:(j + 1) * S5_CHUNK]


def _s5_mix(ut, toep, wz, wy, ca, cb, layer, n_batch, n_lat, n_ctx):
    g, j, lc = S5_NGROUPS, S5_GROUP, S5_CHUNK
    nr = ut.shape[1] // lc
    u3 = ut.reshape(W_GROUP, nr, lc)
    grp = lambda i: (i, 0, 0)
    tab = lambda i: (layer * g + i, 0, 0)
    z = pl.pallas_call(
        _s5_local_kernel,
        out_shape=jax.ShapeDtypeStruct((nr, g * S5_SW), F32),
        grid=(g,),
        in_specs=[pl.BlockSpec((j, nr, lc), grp), pl.BlockSpec((j, lc, S5_SW), tab)],
        out_specs=pl.BlockSpec((nr, S5_SW), lambda i: (0, i)),
        compiler_params=_cparams(("arbitrary",)),
        name="s5_local_state",
    )(u3, wz)
    z3 = z.reshape(nr, 2 * g, LANES)
    whole = lambda shp: pl.BlockSpec(shp, lambda: tuple(0 for _ in shp))
    x3 = pl.pallas_call(
        functools.partial(_s5_scan_kernel, n_batch=n_batch, n_lat=n_lat, n_ctx=n_ctx),
        out_shape=jax.ShapeDtypeStruct(z3.shape, F32),
        in_specs=[whole(z3.shape), whole(ca.shape), whole(cb.shape)],
        out_specs=whole(z3.shape),
        scratch_shapes=[pltpu.VMEM(z3.shape, F32), pltpu.VMEM(z3.shape, F32)],
        compiler_params=pltpu.CompilerParams(vmem_limit_bytes=VMEM_LIMIT),
        name="s5_chunk_scan",
    )(z3, ca, cb)
    x = x3.reshape(nr, g * S5_SW)
    y3 = pl.pallas_call(
        _s5_out_kernel,
        out_shape=jax.ShapeDtypeStruct((W_GROUP, nr, lc), F32),
        grid=(g,),
        in_specs=[pl.BlockSpec((j, nr, lc), grp), pl.BlockSpec((j, lc, lc * j), tab),
                  pl.BlockSpec((nr, S5_SW), lambda i: (0, i)),
                  pl.BlockSpec((None, S5_SW, lc * j), tab)],
        out_specs=pl.BlockSpec((j, nr, lc), grp),
        compiler_params=_cparams(("arbitrary",)),
        name="s5_output",
    )(u3, toep, x, wy)
    return y3.reshape(W_GROUP, nr * lc)


def _outproj_kernel(x_ref, mod_ref, hf_ref, hb_ref, og_ref, yf_ref, yb_ref, xs_ref, z_ref,
                    ut_ref, s5_ref, attl_ref, attc_ref, pcol_ref, glu_ref, wo_ref, n2_ref, xo_ref, h2_ref,
                    *, n_lat_tiles):
    h = hf_ref[...] + hb_ref[...]
    parts = []
    for i in range(ML_HEADS):
        hh = h[i * ML_DH:(i + 1) * ML_DH, :]
        parts.append(hh * lax.rsqrt(jnp.mean(hh * hh, axis=0, keepdims=True) + EPS))
    ya = jnp.concatenate(parts, axis=0) * pcol_ref[:, 0:1] * _sigmoid(og_ref[...])
    yc = (pcol_ref[:, 1:2] * xs_ref[...] + yf_ref[...] + yb_ref[...]) * _silu(z_ref[...])
    yc = yc * lax.rsqrt(jnp.mean(yc * yc, axis=0, keepdims=True) + EPS) * pcol_ref[:, 2:3]
    yd = _gelu_tanh(pcol_ref[:, 3:4] * ut_ref[...] + s5_ref[...])
    yd = yd * _sigmoid(_dot(glu_ref[...], yd.astype(BF16)))
    att = jnp.where(pl.program_id(0) < n_lat_tiles, attl_ref[...], attc_ref[...])
    mixed = jnp.concatenate([ya.astype(BF16), att, yc.astype(BF16), yd.astype(BF16)], axis=0)
    y = _dot_tn(mixed, wo_ref[...])
    x = x_ref[...] + mod_ref[2:3, :] * y
    xo_ref[...] = x
    h2 = _rms(x, n2_ref[...]) * (1.0 + mod_ref[4:5, :]) + mod_ref[3:4, :]
    h2_ref[...] = h2.astype(BF16)


def _out_projection(x, mod, seg_of_tile, acts, y5, att_lat, att_ctx, params):
    r, d = x.shape
    n_lat_tiles = att_lat.shape[1] // ROW_TILE
    assert att_ctx.shape[1] == ROW_TILE and r == (n_lat_tiles + 1) * ROW_TILE
    row = lambda i: (i, 0)
    full = lambda a: pl.BlockSpec(a.shape, lambda i: (0, 0))
    return pl.pallas_call(
        functools.partial(_outproj_kernel, n_lat_tiles=n_lat_tiles),
        out_shape=[jax.ShapeDtypeStruct((r, d), F32), jax.ShapeDtypeStruct((r, d), BF16)],
        grid=(r // ROW_TILE,),
        in_specs=[pl.BlockSpec((ROW_TILE, d), row),
                  pl.BlockSpec((None, 8, d), lambda i: (seg_of_tile(i), 0, 0))]
        + [pl.BlockSpec((W_GROUP, ROW_TILE), lambda i: (0, i)) for _ in acts]
        + [pl.BlockSpec((W_GROUP, ROW_TILE), lambda i: (0, i)),
           pl.BlockSpec((W_GROUP, ROW_TILE), lambda i: (0, jnp.minimum(i, n_lat_tiles - 1))),
           pl.BlockSpec((W_GROUP, ROW_TILE), lambda i: (0, 0))]
        + [full(p) for p in params],
        out_specs=[pl.BlockSpec((ROW_TILE, d), row)] * 2,
        compiler_params=_cparams(("arbitrary",)),
        name="out_projection",
    )(x, mod, *acts, y5, att_lat, att_ctx, *params)


def _ffn_kernel(x_ref, mod_ref, h_ref, hp_ref, hn_ref, wup_ref, cw_ref, wdn_ref, fg_ref, o_ref, acc_ref,
                *, bounds, final_norm):
    tm = h_ref.shape[0]
    first, last = _segment_masks(pl.program_id(0) * tm, tm, bounds)
    h = h_ref[...]
    h_ext = jnp.concatenate([hp_ref[...], h, hn_ref[...]], axis=0)
    n_ext = tm + 2 * HALO
    n_chunks = D_FF // FF_CHUNK

    def up(j):
        c0 = j * FF_CHUNK
        return (_dot(h, wup_ref[:, c0:c0 + FF_CHUNK]),
                _dot(h_ext, wup_ref[:, D_FF + c0:D_FF + c0 + FF_CHUNK]))

    def down(j, u, g_ext):
        c0 = j * FF_CHUNK
        gp = jnp.where(first, 0.0, pltpu.roll(g_ext, 1, 0)[HALO:HALO + tm])
        gn = jnp.where(last, 0.0, pltpu.roll(g_ext, n_ext - 1, 0)[HALO:HALO + tm])
        gc = g_ext[HALO:HALO + tm]
        cw = cw_ref[:, c0:c0 + FF_CHUNK]
        conv = cw[0:1, :] * gp + cw[1:2, :] * gc + cw[2:3, :] * gn
        act = (_silu(conv) * u).astype(BF16)
        return _dot(act, wdn_ref[c0:c0 + FF_CHUNK, :])

    nxt = up(0)
    for j in range(n_chunks):
        cur = nxt
        if j + 1 < n_chunks:
            nxt = up(j + 1)
        y = down(j, *cur)
        if j == 0:
            acc_ref[...] = y
        else:
            acc_ref[...] += y
    y = x_ref[...] + mod_ref[5:6, :] * acc_ref[...]
    o_ref[...] = _rms(y, fg_ref[...]) if final_norm else y


def _conv_ffn(x, mod, seg_of_tile, h2, w_up, conv_w, w_down, bounds, n_out_rows, final_gain=None):
    r, d = x.shape
    row = lambda i: (i, 0)
    prev, nxt = _halo_specs(d, r)
    full = lambda a: pl.BlockSpec(a.shape, lambda i: (0, 0))
    gain = jnp.ones((1, d), F32) if final_gain is None else final_gain
    return pl.pallas_call(
        functools.partial(_ffn_kernel, bounds=bounds, final_norm=final_gain is not None),
        out_shape=jax.ShapeDtypeStruct((n_out_rows, d), F32),
        grid=(n_out_rows // ROW_TILE,),
        in_specs=[pl.BlockSpec((ROW_TILE, d), row),
                  pl.BlockSpec((None, 8, d), lambda i: (seg_of_tile(i), 0, 0)),
                  pl.BlockSpec((ROW_TILE, d), row), prev, nxt,
                  full(w_up), full(conv_w), full(w_down), full(gain)],
        out_specs=pl.BlockSpec((ROW_TILE, d), row),
        scratch_shapes=[pltpu.VMEM((ROW_TILE, d), F32)],
        compiler_params=_cparams(("arbitrary",)),
        name="conv_ffn",
    )(x, mod, h2, h2, h2, w_up, conv_w, w_down, gain)


def _rope_tables(t_lat, n_ctx_rows):
    rows = t_lat // GRID_W
    row = jnp.broadcast_to(jnp.arange(rows)[:, None], (rows, GRID_W)).reshape(-1).astype(F32)
    col = jnp.broadcast_to(jnp.arange(GRID_W)[None, :], (rows, GRID_W)).reshape(-1).astype(F32)
    n_freq = MLA_ROPE // 4
    inv = ROPE_BASE ** (-jnp.arange(n_freq, dtype=F32) / n_freq)
    ang = jnp.concatenate([row[:, None] * inv, col[:, None] * inv], axis=-1)
    half = MLA_ROPE // 2
    pad = LANES - MLA_NOPE - MLA_ROPE

    def table(t, lead):
        return jnp.concatenate([jnp.full((t_lat, MLA_NOPE), lead, F32), t, t, jnp.zeros((t_lat, pad), F32)], axis=-1)

    cos = table(jnp.cos(ang), 1.0)
    sin = table(jnp.sin(ang), 0.0)
    lane = np.arange(LANES)
    cos_ctx = jnp.broadcast_to(jnp.asarray((lane < MLA_NOPE + 2 * half).astype(np.float32)), (n_ctx_rows, LANES))
    sin_ctx = jnp.zeros((n_ctx_rows, LANES), F32)
    return cos, sin, cos_ctx, sin_ctx


def _mla_weights(w_uq, w_ukv):
    half = MLA_ROPE // 2
    qd = MLA_NOPE + MLA_ROPE
    kd = MLA_NOPE + MLA_V
    hw = MLA_HEADS * LANES
    qa_idx = np.zeros((hw,), np.int32); qa_s = np.zeros((hw,), np.float32)
    qb_idx = np.zeros((hw,), np.int32); qb_s = np.zeros((hw,), np.float32)
    ka_idx = np.zeros((hw,), np.int32); ka_s = np.zeros((hw,), np.float32)
    v_idx = np.zeros((MLA_HEADS * MLA_VROWS,), np.int32); v_s = np.zeros((MLA_HEADS * MLA_VROWS,), np.float32)
    for h in range(MLA_HEADS):
        for i in range(qd):
            qa_idx[h * LANES + i] = h * qd + i
            qa_s[h * LANES + i] = 1.0
        for i in range(half):
            qb_idx[h * LANES + MLA_NOPE + i] = h * qd + MLA_NOPE + half + i
            qb_s[h * LANES + MLA_NOPE + i] = -1.0
            qb_idx[h * LANES + MLA_NOPE + half + i] = h * qd + MLA_NOPE + i
            qb_s[h * LANES + MLA_NOPE + half + i] = 1.0
        for i in range(MLA_NOPE):
            ka_idx[h * LANES + i] = h * kd + i
            ka_s[h * LANES + i] = 1.0
        for i in range(MLA_V):
            v_idx[h * MLA_VROWS + i] = h * kd + MLA_NOPE + i
            v_s[h * MLA_VROWS + i] = 1.0
    pick = lambda w, idx, s: (w[..., idx] * s).astype(BF16)
    return (pick(w_uq, qa_idx, qa_s), pick(w_uq, qb_idx, qb_s), pick(w_ukv, ka_idx, ka_s),
            pick(w_ukv, v_idx, v_s).T)


def _lane_rows(vals, offset, n_rows=8):
    k, n = vals.shape
    out = jnp.zeros((n_rows, LANES), F32)
    return out.at[:k, offset:offset + n].set(vals)


def kernel(x, c, ctx, c_ctx, w_mod, b_mod, norm1, norm2, w_in, ml_gate_bias, ml_norm, mla_q_norm, mla_kv_norm, mla_w_uq, mla_w_ukv, ssd_conv_w, ssd_conv_b, ssd_a_log, ssd_dt_bias, ssd_d, ssd_norm, s5_a_re, s5_a_im, s5_log_dt, s5_b_re, s5_b_im, s5_c_re, s5_c_im, s5_d, s5_w_glu, w_out, ffn_w_up, ffn_conv_w, ffn_w_down, final_norm):
    n_batch, t_lat, d = x.shape
    t_ctx = ctx.shape[1]
    depth = w_mod.shape[0]
    assert d == D_MODEL and t_ctx == SCAN_CHUNK and t_lat % ROW_TILE == 0
    assert (n_batch * t_ctx) % ROW_TILE == 0 and t_lat % ATTN_TK == 0 and t_lat % GRID_W == 0
    n_lat = n_batch * t_lat
    n_rows = n_lat + n_batch * t_ctx
    n_scan = t_lat // SCAN_CHUNK
    bounds = tuple(b * t_lat for b in range(n_batch)) + tuple(n_lat + b * t_ctx for b in range(n_batch + 1))
    seg_of_tile = lambda i: jnp.minimum(i * ROW_TILE // t_lat, n_batch)

    c8 = jnp.zeros((8, d), F32).at[:n_batch].set(c).at[n_batch].set(c_ctx)
    mod_all = _modulation(c8, w_mod, b_mod)
    mod_all = mod_all.reshape(depth, 8, 6, d)[:, :n_batch + 1]
    mod_all = jnp.pad(mod_all, ((0, 0), (0, 0), (0, 2), (0, 0)))

    w_in_tok, w_in_chan = _inproj_weights(w_in)
    assert w_in_tok.shape[-1] == _IN_TOTAL and w_in_chan.shape[-2] == _IN_T_TOTAL
    gate_row = jax.vmap(lambda v: _lane_rows(v.reshape(1, -1), MISC_GATE, 1))(ml_gate_bias)
    gate_col = jnp.swapaxes(gate_row, 1, 2)
    ssd_a = -jnp.exp(ssd_a_log)
    ssd_row = jax.vmap(lambda bvec, avec: _lane_rows(jnp.stack([bvec.reshape(-1), avec.reshape(-1)]), MISC_DT))(
        ssd_dt_bias, ssd_a)
    ssd_col = jnp.swapaxes(ssd_row, 1, 2)
    ssd_cw = jnp.pad(ssd_conv_w, ((0, 0), (0, 8 - ssd_conv_w.shape[1]), (0, 0)))
    mix_cols = jnp.stack([ml_norm, jnp.repeat(ssd_d, SSD_HEADDIM, axis=-1), ssd_norm, s5_d], axis=-1)
    mix_cols = jnp.pad(mix_cols, ((0, 0), (0, 0), (0, 4)))
    ffn_cw = jnp.pad(ffn_conv_w, ((0, 0), (0, 8 - ffn_conv_w.shape[1]), (0, 0)))
    cos_l, sin_l, cos_c, sin_c = _rope_tables(t_lat, n_batch * t_ctx)
    cos = jnp.concatenate([cos_l] * n_batch + [cos_c], axis=0)
    sin = jnp.concatenate([sin_l] * n_batch + [sin_c], axis=0)
    toep, wz, wy, s5_decay = _s5_tables(s5_a_re, s5_a_im, s5_log_dt, s5_b_re, s5_b_im, s5_c_re, s5_c_im)
    s5_decay = s5_decay.reshape(depth, S5_NGROUPS, 8, LANES)
    s5_ca = s5_decay[:, :, 0:2].reshape(depth, 2 * S5_NGROUPS, LANES)
    s5_cb = s5_decay[:, :, 2:4].reshape(depth, 2 * S5_NGROUPS, LANES)
    n_s5_ctx = t_ctx // S5_CHUNK
    n_s5_lat = t_lat // S5_CHUNK

    xf = jnp.concatenate([x.reshape(n_lat, d), ctx.reshape(n_batch * t_ctx, d)], axis=0)
    for l in range(depth):
        mod = mod_all[l]
        mla = (cos, sin, mla_q_norm[l][None], mla_kv_norm[l][None]) + _mla_weights(mla_w_uq[l], mla_w_ukv[l])
        q, k, misc, xbc, v1t, ogt, zt, ut, qa, ka, vta = _in_projection(
            xf, mod, norm1[l][None], w_in_tok, w_in_chan, l, seg_of_tile, mla)
        act, xst = _ssd_prep(xbc, ssd_cw[l], ssd_conv_b[l][None], bounds)
        hf, hb, yf, yb = _scans(_mlstm_plan(q, k, v1t, misc, gate_col[l], n_batch, n_scan),
                                _ssd_plan(act, xst, misc, ssd_col[l], n_batch, n_scan),
                                n_rows, n_batch, n_scan)
        att_lat, att_ctx = _attention(qa, ka, vta, n_batch, t_lat, t_ctx)
        y5 = _s5_mix(ut, toep, wz, wy, s5_ca[l], s5_cb[l], l, n_batch, n_s5_lat, n_s5_ctx)
        acts = (hf, hb, ogt, yf, yb, xst, zt, ut)
        params = (mix_cols[l], s5_w_glu[l].T.astype(BF16), w_out[l].astype(BF16), norm2[l][None])
        xf, h2 = _out_projection(xf, mod, seg_of_tile, acts, y5, att_lat, att_ctx, params)
        last = l == depth - 1
        xf = _conv_ffn(xf, mod, seg_of_tile, h2, ffn_w_up[l].astype(BF16), ffn_cw[l], ffn_w_down[l].astype(BF16),
                       bounds, n_lat if last else n_rows, final_norm[None] if last else None)
    return xf.reshape(n_batch, t_lat, d)
```

```python
import functools
import math

import numpy as np
import jax
import jax.numpy as jnp
from jax import lax
from jax.experimental import pallas as pl
from jax.experimental.pallas import tpu as pltpu

F32 = jnp.float32
BF16 = jnp.bfloat16
HI = lax.Precision.HIGHEST

D_MODEL = 1024
W_GROUP = 256
EPS = 1e-6
GRID_W = 64
ROPE_BASE = 10000.0

ML_HEADS = 4
ML_DH = 64

MLA_HEADS = 4
MLA_NOPE = 64
MLA_ROPE = 32
MLA_V = 64
MLA_Q_LORA = 256
MLA_KV_LORA = 128
MLA_VROWS = 80

SSD_HEADS = 4
SSD_HEADDIM = 64
SSD_GROUPS = 2
SSD_STATE = 128

S5_GROUP = 16
S5_NGROUPS = 16
S5_STATE = 64
S5_CHUNK = 64
S5_SW = 4 * S5_STATE

D_FF = 2816
FF_CHUNK = 256

ROW_TILE = 512
SCAN_CHUNK = 256
ATTN_TQ = 512
ATTN_TK = 512
HALO = 16
LANES = 128
VMEM_LIMIT = 56 * 1024 * 1024

MISC_GATE = 0
MISC_DT = 16
MISC_KR = 24


def _cparams(sem):
    return pltpu.CompilerParams(dimension_semantics=sem, vmem_limit_bytes=VMEM_LIMIT)


def _dot(a, b):
    return jnp.dot(a, b, preferred_element_type=F32)


def _dot_nt(a, b):
    return lax.dot_general(a, b, (((1,), (1,)), ((), ())), preferred_element_type=F32)


def _dot_tn(a, b):
    return lax.dot_general(a, b, (((0,), (0,)), ((), ())), preferred_element_type=F32)


def _dot_hi(a, b):
    return jnp.dot(a, b, preferred_element_type=F32, precision=HI)


def _sigmoid(x):
    return 0.5 * (1.0 + jnp.tanh(0.5 * x))


def _silu(x):
    return x * _sigmoid(x)


def _log_sigmoid(x):
    return jnp.minimum(x, 0.0) - jnp.log(1.0 + jnp.exp(-jnp.abs(x)))


def _softplus(x):
    return jnp.maximum(x, 0.0) + jnp.log(1.0 + jnp.exp(-jnp.abs(x)))


def _gelu_tanh(x):
    return 0.5 * x * (1.0 + jnp.tanh(math.sqrt(2.0 / math.pi) * (x + 0.044715 * x * x * x)))


def _rms(x, g):
    return x * lax.rsqrt(jnp.mean(x * x, axis=-1, keepdims=True) + EPS) * g


def _segment_masks(row0, n_rows, bounds):
    r = row0 + lax.broadcasted_iota(jnp.int32, (n_rows, 1), 0)
    first = r == bounds[0]
    last = r == bounds[1] - 1
    for s in bounds[1:-1]:
        first = first | (r == s)
    for e in bounds[2:]:
        last = last | (r == e - 1)
    return first, last


def _mod_kernel(c_ref, w_ref, b_ref, o_ref):
    c = c_ref[...]
    s = _silu(c).astype(BF16)
    o_ref[...] = _dot(s, w_ref[...].astype(BF16)) + b_ref[...]


def _modulation(c8, w_mod, b_mod):
    depth, d, d6 = w_mod.shape
    return pl.pallas_call(
        _mod_kernel,
        out_shape=jax.ShapeDtypeStruct((depth, 8, d6), F32),
        grid=(depth, d6 // d),
        in_specs=[
            pl.BlockSpec((8, d), lambda l, j: (0, 0)),
            pl.BlockSpec((None, d, d), lambda l, j: (l, 0, j)),
            pl.BlockSpec((None, 1, d), lambda l, j: (l, 0, j)),
        ],
        out_specs=pl.BlockSpec((None, 8, d), lambda l, j: (l, 0, j)),
        compiler_params=_cparams(("arbitrary", "arbitrary")),
        name="modulation",
    )(c8, w_mod, b_mod.reshape(depth, 1, d6))


_IN_SPLITS = (("q", 256, BF16), ("k", 256, BF16), ("misc", 128, F32), ("mlac", 384, F32), ("xbc", 768, F32))
_IN_TOTAL = sum(s[1] for s in _IN_SPLITS)
_IN_T_SPLITS = (("v1t", ML_HEADS * LANES, BF16), ("ogt", 256, F32), ("zt", 256, F32), ("ut", 256, F32))
_IN_T_TOTAL = sum(s[1] for s in _IN_T_SPLITS)


def _inproj_weights(w_in):
    ml, mla, ssd, s5 = 0, 1040, 1456, 2488
    cols = lambda a, b: w_in[..., a:b]
    zeros = lambda n: jnp.zeros(w_in.shape[:-1] + (n,), w_in.dtype)
    tok = [cols(ml, ml + 256), cols(ml + 256, ml + 512) * ML_DH ** -0.5]
    tok += [cols(ml + 1024, ml + 1040), cols(ssd + 1024, ssd + 1032),
            cols(mla + 384, mla + 416), zeros(LANES - 56)]
    tok += [cols(mla, mla + 384), cols(ssd + 256, ssd + 1024)]
    chan = []
    for h in range(ML_HEADS):
        chan += [cols(ml + 512 + h * 64, ml + 512 + (h + 1) * 64), zeros(64)]
    chan += [cols(ml + 768, ml + 1024), cols(ssd, ssd + 256), cols(s5, s5 + 256)]
    w_tok = jnp.concatenate(tok, axis=-1).astype(BF16)
    w_chan = jnp.swapaxes(jnp.concatenate(chan, axis=-1), -1, -2).astype(BF16)
    return w_tok, w_chan


def _inproj_kernel(x_ref, mod_ref, g_ref, w_ref, wt_ref, cos_ref, sin_ref, qn_ref, kvn_ref,
                   wqa_ref, wqb_ref, wka_ref, wvt_ref,
                   q_ref, k_ref, misc_ref, xbc_ref, v1t_ref, ogt_ref, zt_ref, ut_ref, qa_ref, ka_ref, vta_ref):
    x = x_ref[...]
    h = _rms(x, g_ref[...]) * (1.0 + mod_ref[1:2, :]) + mod_ref[0:1, :]
    hb = h.astype(BF16)
    off = 0
    tok = {}
    for (name, width, _), o_ref in zip(_IN_SPLITS, (q_ref, k_ref, misc_ref, None, xbc_ref)):
        tok[name] = _dot(hb, w_ref[:, off:off + width])
        if o_ref is not None:
            o_ref[...] = tok[name].astype(o_ref.dtype)
        off += width
    _mla_project(tok["mlac"], tok["misc"], cos_ref[...], sin_ref[...], qn_ref[...], kvn_ref[...],
                 wqa_ref, wqb_ref, wka_ref, wvt_ref, qa_ref, ka_ref, vta_ref)
    off = 0
    for (name, width, _), o_ref in zip(_IN_T_SPLITS, (v1t_ref, ogt_ref, zt_ref, ut_ref)):
        y = _dot_nt(wt_ref[off:off + width, :], hb)
        if name == "v1t":
            row = lax.broadcasted_iota(jnp.int32, (width, 1), 0)
            y = y + jnp.where((row & (LANES - 1)) == ML_DH, 1.0, 0.0)
        o_ref[...] = y.astype(o_ref.dtype)
        off += width


def _in_projection(x, mod, g, w, wt, layer, seg_of_tile, mla):
    r, d = x.shape
    row = lambda i: (i, 0)
    col = lambda i: (0, i)
    full = lambda a: pl.BlockSpec(a.shape, lambda i: (0, 0))
    tok_out = [s for s in _IN_SPLITS if s[0] != "mlac"]
    hw = MLA_HEADS * LANES
    vr = MLA_HEADS * MLA_VROWS
    cos, sin = mla[0], mla[1]
    return pl.pallas_call(
        _inproj_kernel,
        out_shape=[jax.ShapeDtypeStruct((r, width), dt) for _, width, dt in tok_out]
        + [jax.ShapeDtypeStruct((width, r), dt) for _, width, dt in _IN_T_SPLITS]
        + [jax.ShapeDtypeStruct((r, hw), BF16)] * 2 + [jax.ShapeDtypeStruct((vr, r), BF16)],
        grid=(r // ROW_TILE,),
        in_specs=[
            pl.BlockSpec((ROW_TILE, d), row),
            pl.BlockSpec((None, 8, d), lambda i: (seg_of_tile(i), 0, 0)),
            pl.BlockSpec((1, d), lambda i: (0, 0)),
            pl.BlockSpec((None, d, _IN_TOTAL), lambda i: (layer, 0, 0)),
            pl.BlockSpec((None, _IN_T_TOTAL, d), lambda i: (layer, 0, 0)),
            pl.BlockSpec((ROW_TILE, LANES), row), pl.BlockSpec((ROW_TILE, LANES), row),
        ] + [full(a) for a in mla[2:]],
        out_specs=[pl.BlockSpec((ROW_TILE, width), row) for _, width, _ in tok_out]
        + [pl.BlockSpec((width, ROW_TILE), col) for _, width, _ in _IN_T_SPLITS]
        + [pl.BlockSpec((ROW_TILE, hw), row)] * 2 + [pl.BlockSpec((vr, ROW_TILE), col)],
        compiler_params=_cparams(("arbitrary",)),
        name="in_projection",
    )(x, mod, g, w, wt, cos, sin, *mla[2:])


def _tri_masks(n):
    row = lax.broadcasted_iota(jnp.int32, (n, n), 0)
    col = lax.broadcasted_iota(jnp.int32, (n, n), 1)
    return col <= row, col >= row


def _col_forms(rows16, n):
    return jnp.concatenate([rows16, jnp.zeros((LANES - rows16.shape[0], n), F32)], axis=0).T


def _mlstm_pass1(qf_ref, kf_ref, vf_ref, mf_ref, qb_ref, kb_ref, vb_ref, mb_ref,
                 gbc_ref, hf_ref, hb_ref, st_ref, m_ref):
    n = qf_ref.shape[0]

    @pl.when(pl.program_id(1) == 0)
    def _():
        st_ref[...] = jnp.zeros_like(st_ref)
        m_ref[...] = jnp.zeros_like(m_ref)

    tril, triu = _tri_masks(n)
    dirs = ((qf_ref, kf_ref, vf_ref, mf_ref, hf_ref, triu, n - 1),
            (qb_ref, kb_ref, vb_ref, mb_ref, hb_ref, tril, 0))
    pending = []
    for d, (q_ref, k_ref, vt_ref, misc_ref, out_ref, mask, last) in enumerate(dirs):
        gt = misc_ref[...].T + gbc_ref[...]
        brow = _dot_hi(_log_sigmoid(gt[0:16, :]), mask.astype(F32))
        ccol = _col_forms(gt[0:16, :] - pltpu.roll(brow, 16 - ML_HEADS, 0), n)
        for h in range(ML_HEADS):
            hd = d * ML_HEADS + h
            i_idx = MISC_GATE + 8 * d + h
            f_idx = i_idx + ML_HEADS
            ig_row = gt[i_idx:i_idx + 1, :]
            b_row = brow[f_idx:f_idx + 1, :]
            c_col = ccol[:, i_idx:i_idx + 1]
            b_last = b_row[:, last:last + 1]
            m_prev = m_ref[hd:hd + 1, 0:1]
            q = q_ref[:, h * ML_DH:(h + 1) * ML_DH]
            k = k_ref[:, h * ML_DH:(h + 1) * ML_DH]
            v1t = vt_ref[h * LANES:(h + 1) * LANES, :]
            state = st_ref[hd]

            qk = _dot_nt(k, q)
            carried = _dot_nt(state.astype(BF16), q)
            w_log = b_last - b_row + ig_row
            m_new = jnp.maximum(b_last + m_prev, jnp.max(w_log, axis=1, keepdims=True))
            decay = jnp.exp(b_last + m_prev - m_new)
            vw = (v1t.astype(F32) * jnp.exp(w_log - m_new)).astype(BF16)
            st_ref[hd] = decay * state + _dot(vw, k)
            m_ref[hd:hd + 1, :] = jnp.broadcast_to(m_new, (1, LANES))
            pending.append((out_ref, h, mask, b_row, c_col, m_prev, qk, carried, v1t))
    return pending


def _mlstm_pass2(pending):
    for out_ref, h, mask, b_row, c_col, m_prev, qk, carried, v1t in pending:
        dmat = jnp.where(mask, b_row + c_col, -jnp.inf)
        inter = b_row + m_prev
        m_t = jnp.maximum(inter, jnp.max(dmat, axis=0, keepdims=True))
        s = qk * jnp.exp(dmat - m_t)
        tot = _dot(v1t, s.astype(BF16)) + jnp.exp(inter - m_t) * carried
        den = tot[ML_DH:ML_DH + 1, :]
        out_ref[h * ML_DH:(h + 1) * ML_DH, :] = tot[0:ML_DH, :] / jnp.maximum(jnp.abs(den), jnp.exp(-m_t))


def _scan_index_maps(n_batch, n_chunks):
    ctx0 = n_batch * n_chunks
    fwd = lambda b, s: (jnp.where(s == 0, ctx0 + b, b * n_chunks + s - 1), 0)
    bwd = lambda b, s: (jnp.where(s == 0, ctx0 + b, b * n_chunks + n_chunks - s), 0)
    return fwd, bwd


def _mlstm_plan(q, k, v1t, misc, gb_col, n_batch, n_chunks):
    n = SCAN_CHUNK
    fwd, bwd = _scan_index_maps(n_batch, n_chunks)
    const = lambda b, s: (0, 0)
    specs = []
    for imap in (fwd, bwd):
        cols = lambda b, s, imap=imap: imap(b, s)[::-1]
        specs += [pl.BlockSpec((n, W_GROUP), imap), pl.BlockSpec((n, W_GROUP), imap),
                  pl.BlockSpec((ML_HEADS * LANES, n), cols), pl.BlockSpec((n, LANES), imap)]
    specs += [pl.BlockSpec((LANES, 1), const)]
    scratch = [pltpu.VMEM((2 * ML_HEADS, LANES, ML_DH), F32), pltpu.VMEM((2 * ML_HEADS, LANES), F32)]
    return (q, k, v1t, misc, q, k, v1t, misc, gb_col), specs, scratch


def _ssd_prep_kernel(x_ref, xp_ref, xn_ref, w_ref, b_ref, act_ref, xs_ref, *, bounds):
    tm = x_ref.shape[0]
    first, last = _segment_masks(pl.program_id(0) * tm, tm, bounds)
    x = x_ref[...]
    rid = lax.broadcasted_iota(jnp.int32, (tm, 1), 0)
    xp = jnp.where(rid == 0, xp_ref[HALO - 1:HALO, :], pltpu.roll(x, 1, 0))
    xn = jnp.where(rid == tm - 1, xn_ref[0:1, :], pltpu.roll(x, tm - 1, 0))
    xp = jnp.where(first, 0.0, xp)
    xn = jnp.where(last, 0.0, xn)
    y = _silu(w_ref[0:1, :] * xp + w_ref[1:2, :] * x + w_ref[2:3, :] * xn + b_ref[...])
    act_ref[...] = y.astype(act_ref.dtype)
    xs_ref[...] = y[:, 0:W_GROUP].T


def _halo_specs(width, n_rows):
    per = ROW_TILE // HALO
    n_blocks = n_rows // HALO
    prev = pl.BlockSpec((HALO, width), lambda i: (jnp.maximum(i * per - 1, 0), 0))
    nxt = pl.BlockSpec((HALO, width), lambda i: (jnp.minimum((i + 1) * per, n_blocks - 1), 0))
    return prev, nxt


def _ssd_prep(xbc, conv_w, conv_b, bounds):
    r, width = xbc.shape
    row = lambda i: (i, 0)
    prev, nxt = _halo_specs(width, r)
    return pl.pallas_call(
        functools.partial(_ssd_prep_kernel, bounds=bounds),
        out_shape=[jax.ShapeDtypeStruct((r, width), BF16), jax.ShapeDtypeStruct((W_GROUP, r), F32)],
        grid=(r // ROW_TILE,),
        in_specs=[pl.BlockSpec((ROW_TILE, width), row), prev, nxt,
                  pl.BlockSpec((8, width), lambda i: (0, 0)),
                  pl.BlockSpec((1, width), lambda i: (0, 0))],
        out_specs=[pl.BlockSpec((ROW_TILE, width), row), pl.BlockSpec((W_GROUP, ROW_TILE), lambda i: (0, i))],
        compiler_params=_cparams(("arbitrary",)),
        name="ssd_conv",
    )(xbc, xbc, xbc, conv_w, conv_b)


def _ssd_pass1(af_ref, xf_ref, mf_ref, ab_ref, xb_ref, mb_ref, pc_ref, yf_ref, yb_ref, st_ref):
    n = af_ref.shape[0]

    @pl.when(pl.program_id(1) == 0)
    def _():
        st_ref[...] = jnp.zeros_like(st_ref)

    tril, triu = _tri_masks(n)
    dirs = ((af_ref, xf_ref, mf_ref, yf_ref, triu, n - 1),
            (ab_ref, xb_ref, mb_ref, yb_ref, tril, 0))
    pending = []
    for d, (act_ref, xt_ref, misc_ref, out_ref, mask, last) in enumerate(dirs):
        dt_r = _softplus(misc_ref[...].T + pc_ref[:, 0:1])
        cs_r = _dot_hi((dt_r * pc_ref[:, 1:2])[0:32, :], mask.astype(F32))
        ncs_c = _col_forms(-cs_r, n)
        for g in range(SSD_GROUPS):
            bm = act_ref[:, W_GROUP + g * SSD_STATE:W_GROUP + (g + 1) * SSD_STATE]
            cm = act_ref[:, W_GROUP + SSD_GROUPS * SSD_STATE + g * SSD_STATE:
                         W_GROUP + SSD_GROUPS * SSD_STATE + (g + 1) * SSD_STATE]
            gmat = _dot_nt(bm, cm)
            for hh in range(SSD_HEADS // SSD_GROUPS):
                h = g * (SSD_HEADS // SSD_GROUPS) + hh
                hd = d * SSD_HEADS + h
                idx = MISC_DT + d * SSD_HEADS + h
                cs_row = cs_r[idx:idx + 1, :]
                ncs_col = ncs_c[:, idx:idx + 1]
                cs_last = cs_row[:, last:last + 1]
                dt_row = dt_r[idx:idx + 1, :]
                xt = xt_ref[h * SSD_HEADDIM:(h + 1) * SSD_HEADDIM, :]
                state = st_ref[hd]
                carried = _dot_nt(state.astype(BF16), cm)
                xw = (xt * (jnp.exp(cs_last - cs_row) * dt_row)).astype(BF16)
                st_ref[hd] = jnp.exp(cs_last) * state + _dot(xw, bm)
                pending.append((out_ref, h, mask, cs_row, ncs_col, gmat, carried, (xt * dt_row).astype(BF16)))
    return pending


def _ssd_pass2(pending):
    for out_ref, h, mask, cs_row, ncs_col, gmat, carried, xdt in pending:
        decay = jnp.exp(jnp.where(mask, cs_row + ncs_col, -jnp.inf))
        y = _dot(xdt, (gmat * decay).astype(BF16)) + jnp.exp(cs_row) * carried
        out_ref[h * SSD_HEADDIM:(h + 1) * SSD_HEADDIM, :] = y


def _ssd_plan(act, xst, misc, p_col, n_batch, n_chunks):
    width = act.shape[1]
    n = SCAN_CHUNK
    fwd, bwd = _scan_index_maps(n_batch, n_chunks)
    const = lambda b, s: (0, 0)
    fwd_c = lambda b, s: fwd(b, s)[::-1]
    bwd_c = lambda b, s: bwd(b, s)[::-1]
    specs = []
    for imap, cmap in ((fwd, fwd_c), (bwd, bwd_c)):
        specs += [pl.BlockSpec((n, width), imap), pl.BlockSpec((W_GROUP, n), cmap), pl.BlockSpec((n, LANES), imap)]
    specs += [pl.BlockSpec((LANES, 8), const)]
    scratch = [pltpu.VMEM((2 * SSD_HEADS, SSD_HEADDIM, SSD_STATE), F32)]
    return (act, xst, misc, act, xst, misc, p_col), specs, scratch


def _scan_pair_kernel(*refs, n_ml_in, n_ss_in):
    ml_in, ss_in = refs[:n_ml_in], refs[n_ml_in:n_ml_in + n_ss_in]
    outs = refs[n_ml_in + n_ss_in:n_ml_in + n_ss_in + 4]
    scr = refs[n_ml_in + n_ss_in + 4:]
    ml_pending = _mlstm_pass1(*ml_in, outs[0], outs[1], scr[0], scr[1])
    ss_pending = _ssd_pass1(*ss_in, outs[2], outs[3], scr[2])
    _mlstm_pass2(ml_pending)
    _ssd_pass2(ss_pending)


def _scans(ml_plan, ss_plan, n_rows, n_batch, n_chunks):
    n = SCAN_CHUNK
    fwd, bwd = _scan_index_maps(n_batch, n_chunks)
    fwd_c = lambda b, s: fwd(b, s)[::-1]
    bwd_c = lambda b, s: bwd(b, s)[::-1]
    (ml_ops, ml_specs, ml_scr), (ss_ops, ss_specs, ss_scr) = ml_plan, ss_plan
    return pl.pallas_call(
        functools.partial(_scan_pair_kernel, n_ml_in=len(ml_ops), n_ss_in=len(ss_ops)),
        out_shape=[jax.ShapeDtypeStruct((W_GROUP, n_rows), F32)] * 4,
        grid=(n_batch, n_chunks + 1),
        in_specs=ml_specs + ss_specs,
        out_specs=[pl.BlockSpec((W_GROUP, n), fwd_c), pl.BlockSpec((W_GROUP, n), bwd_c)] * 2,
        scratch_shapes=ml_scr + ss_scr,
        compiler_params=_cparams(("arbitrary", "arbitrary")),
        name="mlstm_ssd_scan",
    )(*ml_ops, *ss_ops)


def _mla_project(c, misc, cos, sin, qn, kvn, wqa_ref, wqb_ref, wka_ref, wvt_ref, q_ref, k_ref, vt_ref):
    scale = (MLA_NOPE + MLA_ROPE) ** -0.5 * math.log2(math.e)
    half = MLA_ROPE // 2
    cq = _rms(c[:, 0:MLA_Q_LORA], qn).astype(BF16)
    ckv = _rms(c[:, MLA_Q_LORA:MLA_Q_LORA + MLA_KV_LORA], kvn).astype(BF16)
    qa = _dot(cq, wqa_ref[...])
    qb = _dot(cq, wqb_ref[...])
    ka = _dot(ckv, wka_ref[...])
    lane = lax.broadcasted_iota(jnp.int32, (1, LANES), 1)
    in_rope = (lane >= MLA_NOPE) & (lane < MLA_NOPE + MLA_ROPE)
    kr_a = pltpu.roll(misc, MLA_NOPE - MISC_KR, 1)
    kr_b = jnp.where(lane < MLA_NOPE + half,
                     -pltpu.roll(misc, MLA_NOPE - MISC_KR - half, 1),
                     pltpu.roll(misc, MLA_NOPE - MISC_KR + half, 1))
    kr = jnp.where(in_rope, kr_a * cos + kr_b * sin, 0.0)
    for h in range(MLA_HEADS):
        sl = slice(h * LANES, (h + 1) * LANES)
        q_ref[:, sl] = ((qa[:, sl] * cos + qb[:, sl] * sin) * scale).astype(BF16)
        k_ref[:, sl] = (ka[:, sl] + kr).astype(BF16)
    vt = _dot_nt(wvt_ref[...], ckv)
    row = lax.broadcasted_iota(jnp.int32, (vt.shape[0], 1), 0)
    ones_row = row == MLA_V
    for h in range(1, MLA_HEADS):
        ones_row = ones_row | (row == h * MLA_VROWS + MLA_V)
    vt_ref[...] = (vt + jnp.where(ones_row, 1.0, 0.0)).astype(BF16)


_HEAD_LANES = tuple(slice(h * LANES, (h + 1) * LANES) for h in range(MLA_HEADS))
_HEAD_VROWS = tuple(slice(h * MLA_VROWS, (h + 1) * MLA_VROWS) for h in range(MLA_HEADS))


def _attn_step(q_ref, m_ref, acc_ref, scores=None, update=None):
    new = []
    for h in range(MLA_HEADS):
        if scores is not None:
            k_ref, rows, n_keys, (s_ref, mx_ref) = scores
            st = _dot_nt(k_ref[rows, _HEAD_LANES[h]], q_ref[:, _HEAD_LANES[h]])
            s_ref[h, 0:n_keys, :] = st
            mx_ref[h] = jnp.max(st, axis=0, keepdims=True)
        if update is not None:
            (s_ref, mx_ref), n_keys, vt_ref, cols = update
            m_old = m_ref[h]
            m_new = jnp.maximum(m_old, mx_ref[h])
            p = jnp.exp2(s_ref[h, 0:n_keys, :] - m_new).astype(BF16)
            new.append((m_new, jnp.exp2(m_old - m_new) * acc_ref[h] + _dot(vt_ref[_HEAD_VROWS[h], cols], p)))
    for h, (m_new, acc) in enumerate(new):
        m_ref[h] = m_new
        acc_ref[h] = acc


def _attn_init(m_ref, acc_ref):
    m_ref[...] = jnp.full_like(m_ref, -jnp.inf)
    acc_ref[...] = jnp.zeros_like(acc_ref)


def _attn_finish(o_ref, acc_ref):
    for h in range(MLA_HEADS):
        acc = acc_ref[h]
        o_ref[h * MLA_V:(h + 1) * MLA_V, :] = (acc[0:MLA_V] / acc[MLA_V:MLA_V + 1]).astype(o_ref.dtype)


def _attn_lat_kernel(q_ref, kl_ref, vlt_ref, kc_ref, vct_ref, o_ref, m_ref, acc_ref,
                     sa_s_ref, sa_m_ref, sb_s_ref, sb_m_ref):
    sa_ref = (sa_s_ref, sa_m_ref)
    sb_ref = (sb_s_ref, sb_m_ref)
    tk = ATTN_TK
    n_pairs = kl_ref.shape[0] // (2 * tk)
    n_ctx = kc_ref.shape[0]
    chunk = lambda c: pl.ds(pl.multiple_of(c * tk, tk), tk)
    step = functools.partial(_attn_step, q_ref, m_ref, acc_ref)
    _attn_init(m_ref, acc_ref)
    step(scores=(kl_ref, chunk(0), tk, sa_ref))

    def body(i, carry):
        c = 2 * i
        step(scores=(kl_ref, chunk(c + 1), tk, sb_ref), update=(sa_ref, tk, vlt_ref, chunk(c)))
        step(scores=(kl_ref, chunk(c + 2), tk, sa_ref), update=(sb_ref, tk, vlt_ref, chunk(c + 1)))
        return carry

    lax.fori_loop(0, n_pairs - 1, body, 0)
    c = 2 * (n_pairs - 1)
    step(scores=(kl_ref, chunk(c + 1), tk, sb_ref), update=(sa_ref, tk, vlt_ref, chunk(c)))
    step(scores=(kc_ref, slice(None), n_ctx, sa_ref), update=(sb_ref, tk, vlt_ref, chunk(c + 1)))
    step(update=(sa_ref, n_ctx, vct_ref, slice(None)))
    _attn_finish(o_ref, acc_ref)


def _attn_ctx_kernel(q_ref, kc_ref, vct_ref, o_ref, m_ref, acc_ref, s_ref, mx_ref):
    n_ctx = kc_ref.shape[0]
    _attn_init(m_ref, acc_ref)
    _attn_step(q_ref, m_ref, acc_ref, scores=(kc_ref, slice(None), n_ctx, (s_ref, mx_ref)))
    _attn_step(q_ref, m_ref, acc_ref, update=((s_ref, mx_ref), n_ctx, vct_ref, slice(None)))
    _attn_finish(o_ref, acc_ref)


def _attention(q, k, vt, n_batch, t_lat, t_ctx):
    r, hw = q.shape
    vr = vt.shape[0]
    nq = t_lat // ATTN_TQ
    ctx_blk = n_batch * t_lat // t_ctx
    assert t_lat % (2 * ATTN_TK) == 0 and t_ctx <= ATTN_TK
    scratch = lambda tq: [pltpu.VMEM((MLA_HEADS, 1, tq), F32), pltpu.VMEM((MLA_HEADS, MLA_VROWS, tq), F32)]
    scores = lambda tk, tq: [pltpu.VMEM((MLA_HEADS, tk, tq), F32), pltpu.VMEM((MLA_HEADS, 1, tq), F32)]
    lat = pl.pallas_call(
        _attn_lat_kernel,
        out_shape=jax.ShapeDtypeStruct((W_GROUP, n_batch * t_lat), BF16),
        grid=(n_batch, nq),
        in_specs=[pl.BlockSpec((ATTN_TQ, hw), lambda b, i: (b * nq + i, 0)),
                  pl.BlockSpec((t_lat, hw), lambda b, i: (b, 0)),
                  pl.BlockSpec((vr, t_lat), lambda b, i: (0, b)),
                  pl.BlockSpec((t_ctx, hw), lambda b, i: (ctx_blk + b, 0)),
                  pl.BlockSpec((vr, t_ctx), lambda b, i: (0, ctx_blk + b))],
        out_specs=pl.BlockSpec((W_GROUP, ATTN_TQ), lambda b, i: (0, b * nq + i)),
        scratch_shapes=scratch(ATTN_TQ) + scores(ATTN_TK, ATTN_TQ) * 2,
        compiler_params=_cparams(("arbitrary", "arbitrary")),
        name="mla_attention_latent",
    )(q, k, vt, k, vt)
    ctx = pl.pallas_call(
        _attn_ctx_kernel,
        out_shape=jax.ShapeDtypeStruct((W_GROUP, n_batch * t_ctx), BF16),
        grid=(n_batch,),
        in_specs=[pl.BlockSpec((t_ctx, hw), lambda b: (ctx_blk + b, 0)),
                  pl.BlockSpec((t_ctx, hw), lambda b: (ctx_blk + b, 0)),
                  pl.BlockSpec((vr, t_ctx), lambda b: (0, ctx_blk + b))],
        out_specs=pl.BlockSpec((W_GROUP, t_ctx), lambda b: (0, b)),
        scratch_shapes=scratch(t_ctx) + scores(t_ctx, t_ctx),
        compiler_params=_cparams(("arbitrary",)),
        name="mla_attention_context",
    )(q, k, vt)
    return lat, ctx


def _cis_pow(lr, li, dt, k):
    mag = jnp.exp(lr * dt * k)
    ang = li * dt * k
    return mag * jnp.cos(ang), mag * jnp.sin(ang)


def _s5_tables_kernel(pr_ref, pc_ref, btre_ref, btim_ref, ctre_ref, ctim_ref,
                      toep_ref, wz_ref, wy_ref, decay_ref):
    lc, ng, ns = S5_CHUNK, S5_GROUP, S5_STATE
    lr_r = jnp.minimum(pr_ref[0:1, :], -1e-4)
    li_r = pr_ref[1:2, :]
    dt_r = jnp.exp(pr_ref[2:3, :])
    lr_c = jnp.minimum(pc_ref[:, 0:1], -1e-4)
    li_c = pc_ref[:, 1:2]
    dt_c = jnp.exp(pc_ref[:, 2:3])
    lane4 = lax.broadcasted_iota(jnp.int32, (1, S5_SW), 1)
    odd_lane = (lane4 & ns) != 0
    row4 = lax.broadcasted_iota(jnp.int32, (S5_SW, 1), 0)
    odd_row = (row4 & ns) != 0

    ab_re, ab_im = _cis_pow(lr_r, li_r, dt_r, 1.0)
    den = lr_r * lr_r + li_r * li_r
    f_re = ((ab_re - 1.0) * lr_r + ab_im * li_r) / den
    f_im = (ab_im * lr_r - (ab_re - 1.0) * li_r) / den
    bb_re = f_re * btre_ref[...] - f_im * btim_ref[...]
    bb_im = f_re * btim_ref[...] + f_im * btre_ref[...]

    s_col = lax.broadcasted_iota(jnp.int32, (lc, 1), 0)
    expo = jnp.where(lane4 < 2 * ns, lc - 1 - s_col, s_col).astype(F32)
    pz_re, pz_im = _cis_pow(lr_r, li_r, dt_r, expo)
    xz = jnp.where(odd_lane, bb_im, bb_re)
    yz = jnp.where(odd_lane, bb_re, -bb_im)
    for j in range(ng):
        wz_ref[j] = (pz_re * xz[j:j + 1, :] + pz_im * yz[j:j + 1, :]).astype(BF16)

    al_re, al_im = _cis_pow(lr_r, li_r, dt_r, float(lc))
    al_sw = jnp.where(odd_lane, al_im, -al_im)
    decay_ref[...] = jnp.zeros_like(decay_ref)
    for d in range(2):
        decay_ref[d:d + 1, :] = al_re[:, d * LANES:(d + 1) * LANES]
        decay_ref[2 + d:3 + d, :] = al_sw[:, d * LANES:(d + 1) * LANES]

    t_lane = lax.broadcasted_iota(jnp.int32, (1, LANES), 1) & (lc - 1)
    expo = jnp.where(row4 < 2 * ns, t_lane + 1, lc - t_lane).astype(F32)
    py_re, py_im = _cis_pow(lr_c, li_c, dt_c, expo)
    reps = lc * ng // LANES
    py_re = jnp.tile(py_re, (1, reps))
    py_im = jnp.tile(py_im, (1, reps))
    low_half = lax.broadcasted_iota(jnp.int32, (1, 2 * lc), 1) < lc

    def expand_half(c):
        bc = lambda j: jnp.broadcast_to(c[:, j:j + 1], (c.shape[0], 2 * lc))
        return jnp.concatenate([jnp.where(low_half, bc(2 * p), bc(2 * p + 1)) for p in range(ng // 2)], axis=1)

    def expand_full(c):
        return jnp.concatenate([jnp.broadcast_to(c[:, j:j + 1], (c.shape[0], 2 * lc)) for j in range(ng)], axis=1)

    ce_re = jnp.tile(expand_half(ctre_ref[0:ns, :]), (S5_SW // ns, 1))
    ce_im = jnp.tile(expand_half(ctim_ref[0:ns, :]), (S5_SW // ns, 1))
    c2_re = expand_full(ctre_ref[0:ns, :])
    c2_im = expand_full(ctim_ref[0:ns, :])
    wy_ref[...] = jnp.where(odd_row, -(ce_re * py_im + ce_im * py_re),
                            ce_re * py_re - ce_im * py_im).astype(BF16)

    m_lane = lax.broadcasted_iota(jnp.int32, (1, 2 * lc), 1)
    xmat = None
    for d in range(2):
        rows = slice(d * 2 * ns, d * 2 * ns + ns)
        lag = m_lane - (lc - 1) if d == 0 else (lc - 1) - m_lane
        valid = (lag >= 0) & (m_lane < 2 * lc - 1)
        pm_re, pm_im = _cis_pow(lr_c[rows], li_c[rows], dt_c[rows], jnp.where(valid, lag, 0).astype(F32))
        pm_re = jnp.tile(jnp.where(valid, pm_re, 0.0), (1, ng))
        pm_im = jnp.tile(jnp.where(valid, pm_im, 0.0), (1, ng))
        p_re = c2_re * pm_re - c2_im * pm_im
        p_im = c2_re * pm_im + c2_im * pm_re
        lanes = slice(d * 2 * ns, d * 2 * ns + ns)
        term = _dot_hi(bb_re[:, lanes], p_re) - _dot_hi(bb_im[:, lanes], p_im)
        xmat = term if xmat is None else xmat + term

    for j in range(ng):
        xb = jnp.broadcast_to(xmat[j:j + 1, :], (lc, 2 * lc * ng))
        cols = []
        for p in range(ng // 2):
            even = xb[:, (2 * p) * 2 * lc:(2 * p + 1) * 2 * lc]
            odd = xb[:, (2 * p + 1) * 2 * lc:(2 * p + 2) * 2 * lc]
            cols.append(jnp.where(low_half,
                                  pltpu.roll(even, lc + 1, 1, stride=1, stride_axis=0),
                                  pltpu.roll(odd, 1, 1, stride=1, stride_axis=0)))
        toep_ref[j] = jnp.concatenate(cols, axis=1).astype(BF16)


def _s5_tables(a_re, a_im, log_dt, b_re, b_im, c_re, c_im):
    depth = a_re.shape[0]
    g, n, j, lc = S5_NGROUPS, S5_STATE, S5_GROUP, S5_CHUNK
    n_all = depth * g

    def parts(v):
        v = jnp.transpose(v, (0, 2, 1, 3))
        return jnp.concatenate([v[:, :, 0], v[:, :, 0], v[:, :, 1], v[:, :, 1]], axis=-1).reshape(n_all, S5_SW)

    rows = jnp.stack([parts(a_re), parts(a_im), parts(jnp.broadcast_to(log_dt[..., None], a_re.shape))], axis=1)
    p_row = jnp.pad(rows, ((0, 0), (0, 5), (0, 0)))
    p_col = jnp.swapaxes(p_row, 1, 2)
    bt = lambda b: jnp.tile(jnp.swapaxes(b.reshape(n_all, n, j), 1, 2), (1, 1, 4))
    ct = lambda c: jnp.tile(jnp.swapaxes(c.reshape(n_all, j, n), 1, 2), (1, 4, 1))
    own = lambda i: (i, 0, 0)
    width = lc * j
    return pl.pallas_call(
        _s5_tables_kernel,
        out_shape=[jax.ShapeDtypeStruct((n_all * j, lc, width), BF16),
                   jax.ShapeDtypeStruct((n_all * j, lc, S5_SW), BF16),
                   jax.ShapeDtypeStruct((n_all, S5_SW, width), BF16),
                   jax.ShapeDtypeStruct((n_all, 8, LANES), F32)],
        grid=(n_all,),
        in_specs=[pl.BlockSpec((None, 8, S5_SW), own), pl.BlockSpec((None, S5_SW, 8), own),
                  pl.BlockSpec((None, j, S5_SW), own), pl.BlockSpec((None, j, S5_SW), own),
                  pl.BlockSpec((None, S5_SW, j), own), pl.BlockSpec((None, S5_SW, j), own)],
        out_specs=[pl.BlockSpec((j, lc, width), own), pl.BlockSpec((j, lc, S5_SW), own),
                   pl.BlockSpec((None, S5_SW, width), own), pl.BlockSpec((None, 8, LANES), own)],
        compiler_params=_cparams(("arbitrary",)),
        name="s5_tables",
    )(p_row, p_col, bt(b_re), bt(b_im), ct(c_re), ct(c_im))


def _s5_local_kernel(u_ref, wz_ref, z_ref):
    acc = None
    for j in range(S5_GROUP):
        term = _dot(u_ref[j].astype(BF16), wz_ref[j])
        acc = term if acc is None else acc + term
    z_ref[...] = acc


def _s5_scan_kernel(z_ref, ca_ref, cb_ref, x_ref, xf_ref, xb_ref, *, n_batch, n_lat, n_ctx):
    ca = ca_ref[...]
    cb = cb_ref[...]
    zero = jnp.zeros(ca.shape, F32)
    ctx0 = n_batch * n_lat

    def step(x, z):
        return x * ca + pltpu.roll(x, LANES // 2, 1) * cb + z

    def body(i, carry):
        new = []
        for b in range(n_batch):
            rf = jnp.where(i < n_ctx, ctx0 + b * n_ctx + i, b * n_lat + i - n_ctx)
            rb = jnp.where(i < n_ctx, ctx0 + b * n_ctx + n_ctx - 1 - i, b * n_lat + n_lat - 1 - (i - n_ctx))
            xf, xb = carry[2 * b], carry[2 * b + 1]
            xf_ref[rf] = xf
            xb_ref[rb] = xb
            new += [step(xf, z_ref[rf]), step(xb, z_ref[rb])]
        return tuple(new)

    lax.fori_loop(0, n_ctx + n_lat, body, (zero,) * (2 * n_batch))
    row = lax.broadcasted_iota(jnp.int32, x_ref.shape, 1)
    x_ref[...] = jnp.where((row & 1) == 0, xf_ref[...], xb_ref[...])


def _s5_out_kernel(u_ref, t_ref, x_ref, wy_ref, y_ref):
    acc = _dot(x_ref[...].astype(BF16), wy_ref[...])
    for j in range(S5_GROUP):
        acc = acc + _dot(u_ref[j].astype(BF16), t_ref[j])
    for j in range(S5_GROUP):
        y_ref[j] = acc[:, j * S5_CHUNK:(j + 1) * S5_CHUNK]


def _s5_mix(ut, toep, wz, wy, ca, cb, layer, n_batch, n_lat, n_ctx):
    g, j, lc = S5_NGROUPS, S5_GROUP, S5_CHUNK
    nr = ut.shape[1] // lc
    u3 = ut.reshape(W_GROUP, nr, lc)
    grp = lambda i: (i, 0, 0)
    tab = lambda i: (layer * g + i, 0, 0)
    z = pl.pallas_call(
        _s5_local_kernel,
        out_shape=jax.ShapeDtypeStruct((nr, g * S5_SW), F32),
        grid=(g,),
        in_specs=[pl.BlockSpec((j, nr, lc), grp), pl.BlockSpec((j, lc, S5_SW), tab)],
        out_specs=pl.BlockSpec((nr, S5_SW), lambda i: (0, i)),
        compiler_params=_cparams(("arbitrary",)),
        name="s5_local_state",
    )(u3, wz)
    z3 = z.reshape(nr, 2 * g, LANES)
    whole = lambda shp: pl.BlockSpec(shp, lambda: tuple(0 for _ in shp))
    x3 = pl.pallas_call(
        functools.partial(_s5_scan_kernel, n_batch=n_batch, n_lat=n_lat, n_ctx=n_ctx),
        out_shape=jax.ShapeDtypeStruct(z3.shape, F32),
        in_specs=[whole(z3.shape), whole(ca.shape), whole(cb.shape)],
        out_specs=whole(z3.shape),
        scratch_shapes=[pltpu.VMEM(z3.shape, F32), pltpu.VMEM(z3.shape, F32)],
        compiler_params=pltpu.CompilerParams(vmem_limit_bytes=VMEM_LIMIT),
        name="s5_chunk_scan",
    )(z3, ca, cb)
    x = x3.reshape(nr, g * S5_SW)
    y3 = pl.pallas_call(
        _s5_out_kernel,
        out_shape=jax.ShapeDtypeStruct((W_GROUP, nr, lc), F32),
        grid=(g,),
        in_specs=[pl.BlockSpec((j, nr, lc), grp), pl.BlockSpec((j, lc, lc * j), tab),
                  pl.BlockSpec((nr, S5_SW), lambda i: (0, i)),
                  pl.BlockSpec((None, S5_SW, lc * j), tab)],
        out_specs=pl.BlockSpec((j, nr, lc), grp),
        compiler_params=_cparams(("arbitrary",)),
        name="s5_output",
    )(u3, toep, x, wy)
    return y3.reshape(W_GROUP, nr * lc)


def _outproj_kernel(x_ref, mod_ref, hf_ref, hb_ref, og_ref, yf_ref, yb_ref, xs_ref, z_ref,
                    ut_ref, s5_ref, attl_ref, attc_ref, pcol_ref, glu_ref, wo_ref, n2_ref, xo_ref, h2_ref,
                    *, n_lat_tiles):
    h = hf_ref[...] + hb_ref[...]
    parts = []
    for i in range(ML_HEADS):
        hh = h[i * ML_DH:(i + 1) * ML_DH, :]
        parts.append(hh * lax.rsqrt(jnp.mean(hh * hh, axis=0, keepdims=True) + EPS))
    ya = jnp.concatenate(parts, axis=0) * pcol_ref[:, 0:1] * _sigmoid(og_ref[...])
    yc = (pcol_ref[:, 1:2] * xs_ref[...] + yf_ref[...] + yb_ref[...]) * _silu(z_ref[...])
    yc = yc * lax.rsqrt(jnp.mean(yc * yc, axis=0, keepdims=True) + EPS) * pcol_ref[:, 2:3]
    yd = _gelu_tanh(pcol_ref[:, 3:4] * ut_ref[...] + s5_ref[...])
    yd = yd * _sigmoid(_dot(glu_ref[...], yd.astype(BF16)))
    att = jnp.where(pl.program_id(0) < n_lat_tiles, attl_ref[...], attc_ref[...])
    mixed = jnp.concatenate([ya.astype(BF16), att, yc.astype(BF16), yd.astype(BF16)], axis=0)
    y = _dot_tn(mixed, wo_ref[...])
    x = x_ref[...] + mod_ref[2:3, :] * y
    xo_ref[...] = x
    h2 = _rms(x, n2_ref[...]) * (1.0 + mod_ref[4:5, :]) + mod_ref[3:4, :]
    h2_ref[...] = h2.astype(BF16)


def _out_projection(x, mod, seg_of_tile, acts, y5, att_lat, att_ctx, params):
    r, d = x.shape
    n_lat_tiles = att_lat.shape[1] // ROW_TILE
    assert att_ctx.shape[1] == ROW_TILE and r == (n_lat_tiles + 1) * ROW_TILE
    row = lambda i: (i, 0)
    full = lambda a: pl.BlockSpec(a.shape, lambda i: (0, 0))
    return pl.pallas_call(
        functools.partial(_outproj_kernel, n_lat_tiles=n_lat_tiles),
        out_shape=[jax.ShapeDtypeStruct((r, d), F32), jax.ShapeDtypeStruct((r, d), BF16)],
        grid=(r // ROW_TILE,),
        in_specs=[pl.BlockSpec((ROW_TILE, d), row),
                  pl.BlockSpec((None, 8, d), lambda i: (seg_of_tile(i), 0, 0))]
        + [pl.BlockSpec((W_GROUP, ROW_TILE), lambda i: (0, i)) for _ in acts]
        + [pl.BlockSpec((W_GROUP, ROW_TILE), lambda i: (0, i)),
           pl.BlockSpec((W_GROUP, ROW_TILE), lambda i: (0, jnp.minimum(i, n_lat_tiles - 1))),
           pl.BlockSpec((W_GROUP, ROW_TILE), lambda i: (0, 0))]
        + [full(p) for p in params],
        out_specs=[pl.BlockSpec((ROW_TILE, d), row)] * 2,
        compiler_params=_cparams(("arbitrary",)),
        name="out_projection",
    )(x, mod, *acts, y5, att_lat, att_ctx, *params)


def _ffn_kernel(x_ref, mod_ref, h_ref, hp_ref, hn_ref, wup_ref, cw_ref, wdn_ref, fg_ref, o_ref, acc_ref,
                *, bounds, final_norm):
    tm = h_ref.shape[0]
    first, last = _segment_masks(pl.program_id(0) * tm, tm, bounds)
    h = h_ref[...]
    h_ext = jnp.concatenate([hp_ref[...], h, hn_ref[...]], axis=0)
    n_ext = tm + 2 * HALO
    n_chunks = D_FF // FF_CHUNK

    def up(j):
        c0 = j * FF_CHUNK
        return (_dot(h, wup_ref[:, c0:c0 + FF_CHUNK]),
                _dot(h_ext, wup_ref[:, D_FF + c0:D_FF + c0 + FF_CHUNK]))

    def down(j, u, g_ext):
        c0 = j * FF_CHUNK
        gp = jnp.where(first, 0.0, pltpu.roll(g_ext, 1, 0)[HALO:HALO + tm])
        gn = jnp.where(last, 0.0, pltpu.roll(g_ext, n_ext - 1, 0)[HALO:HALO + tm])
        gc = g_ext[HALO:HALO + tm]
        cw = cw_ref[:, c0:c0 + FF_CHUNK]
        conv = cw[0:1, :] * gp + cw[1:2, :] * gc + cw[2:3, :] * gn
        act = (_silu(conv) * u).astype(BF16)
        return _dot(act, wdn_ref[c0:c0 + FF_CHUNK, :])

    nxt = up(0)
    for j in range(n_chunks):
        cur = nxt
        if j + 1 < n_chunks:
            nxt = up(j + 1)
        y = down(j, *cur)
        if j == 0:
            acc_ref[...] = y
        else:
            acc_ref[...] += y
    y = x_ref[...] + mod_ref[5:6, :] * acc_ref[...]
    o_ref[...] = _rms(y, fg_ref[...]) if final_norm else y


def _conv_ffn(x, mod, seg_of_tile, h2, w_up, conv_w, w_down, bounds, n_out_rows, final_gain=None):
    r, d = x.shape
    row = lambda i: (i, 0)
    prev, nxt = _halo_specs(d, r)
    full = lambda a: pl.BlockSpec(a.shape, lambda i: (0, 0))
    gain = jnp.ones((1, d), F32) if final_gain is None else final_gain
    return pl.pallas_call(
        functools.partial(_ffn_kernel, bounds=bounds, final_norm=final_gain is not None),
        out_shape=jax.ShapeDtypeStruct((n_out_rows, d), F32),
        grid=(n_out_rows // ROW_TILE,),
        in_specs=[pl.BlockSpec((ROW_TILE, d), row),
                  pl.BlockSpec((None, 8, d), lambda i: (seg_of_tile(i), 0, 0)),
                  pl.BlockSpec((ROW_TILE, d), row), prev, nxt,
                  full(w_up), full(conv_w), full(w_down), full(gain)],
        out_specs=pl.BlockSpec((ROW_TILE, d), row),
        scratch_shapes=[pltpu.VMEM((ROW_TILE, d), F32)],
        compiler_params=_cparams(("arbitrary",)),
        name="conv_ffn",
    )(x, mod, h2, h2, h2, w_up, conv_w, w_down, gain)


def _rope_tables(t_lat, n_ctx_rows):
    rows = t_lat // GRID_W
    row = jnp.broadcast_to(jnp.arange(rows)[:, None], (rows, GRID_W)).reshape(-1).astype(F32)
    col = jnp.broadcast_to(jnp.arange(GRID_W)[None, :], (rows, GRID_W)).reshape(-1).astype(F32)
    n_freq = MLA_ROPE // 4
    inv = ROPE_BASE ** (-jnp.arange(n_freq, dtype=F32) / n_freq)
    ang = jnp.concatenate([row[:, None] * inv, col[:, None] * inv], axis=-1)
    half = MLA_ROPE // 2
    pad = LANES - MLA_NOPE - MLA_ROPE

    def table(t, lead):
        return jnp.concatenate([jnp.full((t_lat, MLA_NOPE), lead, F32), t, t, jnp.zeros((t_lat, pad), F32)], axis=-1)

    cos = table(jnp.cos(ang), 1.0)
    sin = table(jnp.sin(ang), 0.0)
    lane = np.arange(LANES)
    cos_ctx = jnp.broadcast_to(jnp.asarray((lane < MLA_NOPE + 2 * half).astype(np.float32)), (n_ctx_rows, LANES))
    sin_ctx = jnp.zeros((n_ctx_rows, LANES), F32)
    return cos, sin, cos_ctx, sin_ctx


def _mla_weights(w_uq, w_ukv):
    half = MLA_ROPE // 2
    qd = MLA_NOPE + MLA_ROPE
    kd = MLA_NOPE + MLA_V
    hw = MLA_HEADS * LANES
    qa_idx = np.zeros((hw,), np.int32); qa_s = np.zeros((hw,), np.float32)
    qb_idx = np.zeros((hw,), np.int32); qb_s = np.zeros((hw,), np.float32)
    ka_idx = np.zeros((hw,), np.int32); ka_s = np.zeros((hw,), np.float32)
    v_idx = np.zeros((MLA_HEADS * MLA_VROWS,), np.int32); v_s = np.zeros((MLA_HEADS * MLA_VROWS,), np.float32)
    for h in range(MLA_HEADS):
        for i in range(qd):
            qa_idx[h * LANES + i] = h * qd + i
            qa_s[h * LANES + i] = 1.0
        for i in range(half):
            qb_idx[h * LANES + MLA_NOPE + i] = h * qd + MLA_NOPE + half + i
            qb_s[h * LANES + MLA_NOPE + i] = -1.0
            qb_idx[h * LANES + MLA_NOPE + half + i] = h * qd + MLA_NOPE + i
            qb_s[h * LANES + MLA_NOPE + half + i] = 1.0
        for i in range(MLA_NOPE):
            ka_idx[h * LANES + i] = h * kd + i
            ka_s[h * LANES + i] = 1.0
        for i in range(MLA_V):
            v_idx[h * MLA_VROWS + i] = h * kd + MLA_NOPE + i
            v_s[h * MLA_VROWS + i] = 1.0
    pick = lambda w, idx, s: (w[..., idx] * s).astype(BF16)
    return (pick(w_uq, qa_idx, qa_s), pick(w_uq, qb_idx, qb_s), pick(w_ukv, ka_idx, ka_s),
            pick(w_ukv, v_idx, v_s).T)


def _lane_rows(vals, offset, n_rows=8):
    k, n = vals.shape
    out = jnp.zeros((n_rows, LANES), F32)
    return out.at[:k, offset:offset + n].set(vals)


def kernel(x, c, ctx, c_ctx, w_mod, b_mod, norm1, norm2, w_in, ml_gate_bias, ml_norm, mla_q_norm, mla_kv_norm, mla_w_uq, mla_w_ukv, ssd_conv_w, ssd_conv_b, ssd_a_log, ssd_dt_bias, ssd_d, ssd_norm, s5_a_re, s5_a_im, s5_log_dt, s5_b_re, s5_b_im, s5_c_re, s5_c_im, s5_d, s5_w_glu, w_out, ffn_w_up, ffn_conv_w, ffn_w_down, final_norm):
    n_batch, t_lat, d = x.shape
    t_ctx = ctx.shape[1]
    depth = w_mod.shape[0]
    assert d == D_MODEL and t_ctx == SCAN_CHUNK and t_lat % ROW_TILE == 0
    assert (n_batch * t_ctx) % ROW_TILE == 0 and t_lat % ATTN_TK == 0 and t_lat % GRID_W == 0
    n_lat = n_batch * t_lat
    n_rows = n_lat + n_batch * t_ctx
    n_scan = t_lat // SCAN_CHUNK
    bounds = tuple(b * t_lat for b in range(n_batch)) + tuple(n_lat + b * t_ctx for b in range(n_batch + 1))
    seg_of_tile = lambda i: jnp.minimum(i * ROW_TILE // t_lat, n_batch)

    c8 = jnp.zeros((8, d), F32).at[:n_batch].set(c).at[n_batch].set(c_ctx)
    mod_all = _modulation(c8, w_mod, b_mod)
    mod_all = mod_all.reshape(depth, 8, 6, d)[:, :n_batch + 1]
    mod_all = jnp.pad(mod_all, ((0, 0), (0, 0), (0, 2), (0, 0)))

    w_in_tok, w_in_chan = _inproj_weights(w_in)
    assert w_in_tok.shape[-1] == _IN_TOTAL and w_in_chan.shape[-2] == _IN_T_TOTAL
    gate_row = jax.vmap(lambda v: _lane_rows(v.reshape(1, -1), MISC_GATE, 1))(ml_gate_bias)
    gate_col = jnp.swapaxes(gate_row, 1, 2)
    ssd_a = -jnp.exp(ssd_a_log)
    ssd_row = jax.vmap(lambda bvec, avec: _lane_rows(jnp.stack([bvec.reshape(-1), avec.reshape(-1)]), MISC_DT))(
        ssd_dt_bias, ssd_a)
    ssd_col = jnp.swapaxes(ssd_row, 1, 2)
    ssd_cw = jnp.pad(ssd_conv_w, ((0, 0), (0, 8 - ssd_conv_w.shape[1]), (0, 0)))
    mix_cols = jnp.stack([ml_norm, jnp.repeat(ssd_d, SSD_HEADDIM, axis=-1), ssd_norm, s5_d], axis=-1)
    mix_cols = jnp.pad(mix_cols, ((0, 0), (0, 0), (0, 4)))
    ffn_cw = jnp.pad(ffn_conv_w, ((0, 0), (0, 8 - ffn_conv_w.shape[1]), (0, 0)))
    cos_l, sin_l, cos_c, sin_c = _rope_tables(t_lat, n_batch * t_ctx)
    cos = jnp.concatenate([cos_l] * n_batch + [cos_c], axis=0)
    sin = jnp.concatenate([sin_l] * n_batch + [sin_c], axis=0)
    toep, wz, wy, s5_decay = _s5_tables(s5_a_re, s5_a_im, s5_log_dt, s5_b_re, s5_b_im, s5_c_re, s5_c_im)
    s5_decay = s5_decay.reshape(depth, S5_NGROUPS, 8, LANES)
    s5_ca = s5_decay[:, :, 0:2].reshape(depth, 2 * S5_NGROUPS, LANES)
    s5_cb = s5_decay[:, :, 2:4].reshape(depth, 2 * S5_NGROUPS, LANES)
    n_s5_ctx = t_ctx // S5_CHUNK
    n_s5_lat = t_lat // S5_CHUNK

    xf = jnp.concatenate([x.reshape(n_lat, d), ctx.reshape(n_batch * t_ctx, d)], axis=0)
    for l in range(depth):
        mod = mod_all[l]
        mla = (cos, sin, mla_q_norm[l][None], mla_kv_norm[l][None]) + _mla_weights(mla_w_uq[l], mla_w_ukv[l])
        q, k, misc, xbc, v1t, ogt, zt, ut, qa, ka, vta = _in_projection(
            xf, mod, norm1[l][None], w_in_tok, w_in_chan, l, seg_of_tile, mla)
        act, xst = _ssd_prep(xbc, ssd_cw[l], ssd_conv_b[l][None], bounds)
        hf, hb, yf, yb = _scans(_mlstm_plan(q, k, v1t, misc, gate_col[l], n_batch, n_scan),
                                _ssd_plan(act, xst, misc, ssd_col[l], n_batch, n_scan),
                                n_rows, n_batch, n_scan)
        att_lat, att_ctx = _attention(qa, ka, vta, n_batch, t_lat, t_ctx)
        y5 = _s5_mix(ut, toep, wz, wy, s5_ca[l], s5_cb[l], l, n_batch, n_s5_lat, n_s5_ctx)
        acts = (hf, hb, ogt, yf, yb, xst, zt, ut)
        params = (mix_cols[l], s5_w_glu[l].T.astype(BF16), w_out[l].astype(BF16), norm2[l][None])
        xf, h2 = _out_projection(xf, mod, seg_of_tile, acts, y5, att_lat, att_ctx, params)
        last = l == depth - 1
        xf = _conv_ffn(xf, mod, seg_of_tile, h2, ffn_w_up[l].astype(BF16), ffn_cw[l], ffn_w_down[l].astype(BF16),
                       bounds, n_lat if last else n_rows, final_norm[None] if last else None)
    return xf.reshape(n_batch, t_lat, d)
```
